```python
import math
import jax, jax.numpy as jnp
from jax import lax
import numpy as np

D_MODEL = 2048
BATCH = 8
SEQ = 2048
DEPTH = 2
DEC_BATCH = 32
DEC_SEQ = 1
PAST_LEN = 8192
PAGE_SIZE = 128

HEAD_DIM = 128
ROPE_THETA = 10000.0
EPS = 1e-6
NEG = -1e30
FORCE = 1e4
SCALE = HEAD_DIM ** -0.5
MIX_WIDTH = D_MODEL
NSA_HEADS = MIX_WIDTH // 2 // HEAD_DIM
NSA_KV_HEADS = 2
NSA_HPG = NSA_HEADS // NSA_KV_HEADS
NSA_Q_W = NSA_HEADS * HEAD_DIM
NSA_KV_W = NSA_KV_HEADS * HEAD_DIM
CMP_BLOCK = 64
SEL_BLOCK = 64
SEL_TOPK = 16
WINDOW = 512
WIN_QBLK = 128
NSA_QCHUNK = 64
POOL_WIDTH = MIX_WIDTH // 4
POOL_WINDOWS = (2, 4, 8, 16)
POOL_GROUP = POOL_WIDTH // len(POOL_WINDOWS)
POOL_BUF = max(POOL_WINDOWS) - 1
GLA_WIDTH = MIX_WIDTH // 4
GLA_HEADS = 4
GLA_DK = GLA_WIDTH // 2 // GLA_HEADS
GLA_DV = GLA_WIDTH // GLA_HEADS
GLA_RANK = 16
GLA_TAU = 16.0
GLA_CHUNK = 64
MEM_TOKENS = 256
MEM_HEADS = 4
MEM_WIDTH = MEM_HEADS * HEAD_DIM
D_FF = 256 * ((8 * D_MODEL // 3 + 255) // 256)
IN_SPLITS = (NSA_Q_W, 6 * NSA_KV_W, 3 * NSA_HEADS, POOL_WIDTH, GLA_HEADS * GLA_DK, GLA_HEADS * GLA_DK, GLA_WIDTH, GLA_RANK, GLA_WIDTH)
N_IN = sum(IN_SPLITS)

kernel_name = 'hybrid_nsa_pool_gla_macaron_step'


def rms_norm(x, g):
    xf = x.astype(jnp.float32)
    y = xf * lax.rsqrt(jnp.mean(xf * xf, axis=-1, keepdims=True) + EPS)
    return (y * g.astype(jnp.float32)).astype(x.dtype)


def rope(x, pos):
    half = x.shape[-1] // 2
    inv = ROPE_THETA ** (-jnp.arange(half, dtype=jnp.float32) / half)
    ang = pos.astype(jnp.float32)[:, None] * inv[None, :]
    cos, sin = jnp.cos(ang)[:, None, :], jnp.sin(ang)[:, None, :]
    xf = x.astype(jnp.float32)
    x1, x2 = xf[..., :half], xf[..., half:]
    return jnp.concatenate([x1 * cos - x2 * sin, x2 * cos + x1 * sin], axis=-1).astype(x.dtype)


def half_ffn(x, g, w_gu, w_down):
    a, b = jnp.split(rms_norm(x, g) @ w_gu, 2, axis=-1)
    return x + 0.5 * ((jax.nn.silu(a) * b) @ w_down)


def project(x, pos, norm_g, w_in, q_gain, k_gain, gate_b, wa2, ba):
    N, T, _ = x.shape
    h = rms_norm(x, norm_g) @ w_in
    cuts = np.cumsum(IN_SPLITS)[:-1].tolist()
    q, kv, gt, u, gq, gk, gv, ga, gr = jnp.split(h, cuts, axis=-1)
    q = rope(rms_norm(q.reshape(N, T, NSA_HEADS, HEAD_DIM), q_gain), pos)
    kv = kv.reshape(N, T, 6, NSA_KV_HEADS, HEAD_DIM)
    cmp_kv = jnp.stack([rope(rms_norm(kv[:, :, 0], k_gain[0]), pos), kv[:, :, 1]], axis=2)
    sel_kv = jnp.stack([rope(rms_norm(kv[:, :, 2], k_gain[1]), pos), kv[:, :, 3]], axis=2)
    win_kv = jnp.stack([rope(rms_norm(kv[:, :, 4], k_gain[2]), pos), kv[:, :, 5]], axis=2)
    gates = jax.nn.sigmoid((gt + gate_b).astype(jnp.float32)).astype(x.dtype).reshape(N, T, NSA_HEADS, 3)
    gq = gq.reshape(N, T, GLA_HEADS, GLA_DK) * (GLA_DK ** -0.5)
    gk = gk.reshape(N, T, GLA_HEADS, GLA_DK)
    gv = gv.reshape(N, T, GLA_HEADS, GLA_DV)
    la = (jax.nn.log_sigmoid((ga @ wa2 + ba).astype(jnp.float32)) / GLA_TAU).reshape(N, T, GLA_HEADS, GLA_DK)
    return q, cmp_kv, sel_kv, win_kv, gates, u, gq, gk, gv, la, gr


def compress(k, v, wk, wv):
    N, T, G, HD = k.shape
    NB = T // CMP_BLOCK
    kb = k.reshape(N, NB, CMP_BLOCK, G, HD)
    vb = v.reshape(N, NB, CMP_BLOCK, G, HD)
    return jnp.einsum('nbjgd,gjd->nbgd', kb, wk), jnp.einsum('nbjgd,gjd->nbgd', vb, wv)


def cmp_attend(q, kc, vc, qpos):
    NB = kc.shape[1]
    s = jnp.einsum('nqghd,nbgd->nqghb', q, kc).astype(jnp.float32) * SCALE
    valid = (((jnp.arange(NB) + 1) * CMP_BLOCK - 1)[None, :] <= qpos[:, None])[None, :, None, None, :]
    s = jnp.where(valid, s, NEG)
    e = jnp.where(valid, jnp.exp(s - jnp.max(s, axis=-1, keepdims=True)), 0.0)
    p = e / jnp.maximum(jnp.sum(e, axis=-1, keepdims=True), 1e-30)
    o = jnp.einsum('nqghb,nbgd->nqghd', p.astype(vc.dtype), vc)
    return o, p


def select_blocks(p, qpos, n_blocks):
    imp = jnp.sum(p, axis=3)
    imp = jnp.pad(imp, ((0, 0), (0, 0), (0, 0), (0, n_blocks - imp.shape[-1])))
    b = jnp.arange(n_blocks)[None, :]
    cur = (qpos // SEL_BLOCK)[:, None]
    forced = ((b == cur) | (b == cur - 1) | (b == 0))[None, :, None, :]
    future = (b > cur)[None, :, None, :]
    score = jnp.where(future, -FORCE, jnp.where(forced, FORCE, imp))
    _, idx = lax.top_k(score, min(SEL_TOPK, n_blocks))
    return idx


def sel_attend(q, kblk, vblk, idx, qpos):
    K = idx.shape[-1]
    kpos = idx[..., None] * SEL_BLOCK + jnp.arange(SEL_BLOCK)
    valid = kpos <= qpos[None, :, None, None, None]
    s = jnp.einsum('nqghd,nqgkjd->nqghkj', q, kblk).astype(jnp.float32) * SCALE
    s = jnp.where(valid[:, :, :, None], s, NEG)
    sh = s.shape
    p = jax.nn.softmax(s.reshape(sh[:4] + (K * SEL_BLOCK,)), axis=-1).reshape(sh).astype(vblk.dtype)
    return jnp.einsum('nqghkj,nqgkjd->nqghd', p, vblk)


def win_attend(q, k, v, qpos, kpos):
    s = jnp.einsum('niqghd,nikgd->niqghk', q, k).astype(jnp.float32) * SCALE
    rel = qpos[:, :, None] - kpos[:, None, :]
    valid = (rel >= 0) & (rel <= WINDOW) & (kpos[:, None, :] >= 0)
    s = jnp.where(valid[None, :, :, None, None, :], s, NEG)
    p = jax.nn.softmax(s, axis=-1).astype(v.dtype)
    return jnp.einsum('niqghk,nikgd->niqghd', p, v)


def nsa_combine(gates, o_c, o_s, o_w):
    return gates[..., 0:1] * o_c + gates[..., 1:2] * o_s + gates[..., 2:3] * o_w


def nsa_prompt(q, cmp_kv, sel_kv, win_kv, gates, wk, wv):
    N, S = q.shape[:2]
    G, HPG, HD = NSA_KV_HEADS, NSA_HPG, HEAD_DIM
    NB = S // SEL_BLOCK
    kc, vc = compress(cmp_kv[:, :, 0], cmp_kv[:, :, 1], wk, wv)
    kb = sel_kv[:, :, 0].reshape(N, NB, SEL_BLOCK, G, HD)
    vb = sel_kv[:, :, 1].reshape(N, NB, SEL_BLOCK, G, HD)
    qg = q.reshape(N, S, G, HPG, HD)
    nidx = jnp.arange(N)[:, None, None, None]
    gidx = jnp.arange(G)[None, None, :, None]

    def chunk(args):
        qc, qpos = args
        o_c, p = cmp_attend(qc, kc, vc, qpos)
        idx = select_blocks(p, qpos, NB)
        o_s = sel_attend(qc, kb[nidx, idx, :, gidx], vb[nidx, idx, :, gidx], idx, qpos)
        return o_c, o_s

    NQC = S // NSA_QCHUNK
    qcs = jnp.moveaxis(qg.reshape(N, NQC, NSA_QCHUNK, G, HPG, HD), 1, 0)
    qps = jnp.arange(S).reshape(NQC, NSA_QCHUNK)
    o_c, o_s = lax.map(chunk, (qcs, qps))
    o_c = jnp.moveaxis(o_c, 0, 1).reshape(N, S, NSA_HEADS, HD)
    o_s = jnp.moveaxis(o_s, 0, 1).reshape(N, S, NSA_HEADS, HD)
    NQB = S // WIN_QBLK
    NPAD = WINDOW // WIN_QBLK
    kvp = jnp.pad(win_kv, ((0, 0), (WINDOW, 0), (0, 0), (0, 0), (0, 0))).reshape(N, NQB + NPAD, WIN_QBLK, 2, G, HD)
    band = jnp.concatenate([kvp[:, j:j + NQB] for j in range(NPAD + 1)], axis=2)
    qpos = jnp.arange(S).reshape(NQB, WIN_QBLK)
    kpos = (jnp.arange(NQB) * WIN_QBLK - WINDOW)[:, None] + jnp.arange((NPAD + 1) * WIN_QBLK)[None, :]
    o_w = win_attend(qg.reshape(N, NQB, WIN_QBLK, G, HPG, HD), band[..., 0, :, :], band[..., 1, :, :], qpos, kpos)
    return nsa_combine(gates, o_c, o_s, o_w.reshape(N, S, NSA_HEADS, HD))


def nsa_sample(q, cmp_new, sel_new, win_new, gates, wk, wv, cache_cmp, cache_sel, win_buf, page_table, layer):
    N, L = q.shape[:2]
    G, HPG, HD = NSA_KV_HEADS, NSA_HPG, HEAD_DIM
    past = page_table.shape[1] * PAGE_SIZE
    T = past + L
    qpos = past + jnp.arange(L)
    qg = q.reshape(N, L, G, HPG, HD)
    nidx = jnp.arange(N)[:, None, None, None]
    gidx = jnp.arange(G)[None, None, :, None]
    past_cmp = cache_cmp[layer, page_table].reshape(N, past, 2, G, HD)
    full_cmp = jnp.concatenate([past_cmp.astype(cmp_new.dtype), cmp_new], axis=1)
    NBC = T // CMP_BLOCK
    kc, vc = compress(full_cmp[:, :NBC * CMP_BLOCK, 0], full_cmp[:, :NBC * CMP_BLOCK, 1], wk, wv)
    o_c, p = cmp_attend(qg, kc, vc, qpos)
    NBT = -(-T // SEL_BLOCK)
    idx = select_blocks(p, qpos, NBT)
    NPB = past // SEL_BLOCK
    BPP = PAGE_SIZE // SEL_BLOCK
    pidx = jnp.minimum(idx, NPB - 1)
    phys = page_table[nidx, pidx // BPP]
    rows = (pidx % BPP)[..., None] * SEL_BLOCK + jnp.arange(SEL_BLOCK)
    past_blk = cache_sel[layer, phys[..., None], rows, :, gidx[..., None]]
    NNB = NBT - NPB
    new_b = jnp.pad(sel_new, ((0, 0), (0, NNB * SEL_BLOCK - L), (0, 0), (0, 0), (0, 0))).reshape(N, NNB, SEL_BLOCK, 2, G, HD)
    new_blk = new_b[nidx, jnp.clip(idx - NPB, 0, NNB - 1), :, :, gidx]
    blk = jnp.where((idx < NPB)[..., None, None, None], past_blk.astype(new_blk.dtype), new_blk)
    o_s = sel_attend(qg, blk[..., 0, :], blk[..., 1, :], idx, qpos)
    W_BUF = win_buf.shape[1]
    ctx = jnp.concatenate([win_buf.astype(win_new.dtype), win_new], axis=1)
    kpos = past - W_BUF + jnp.arange(W_BUF + L)
    o_w = win_attend(qg[:, None], ctx[:, None, :, 0], ctx[:, None, :, 1], qpos[None], kpos[None])[:, 0]
    o = nsa_combine(gates, o_c.reshape(N, L, NSA_HEADS, HD), o_s.reshape(N, L, NSA_HEADS, HD), o_w.reshape(N, L, NSA_HEADS, HD))
    return o, ctx[:, L:]


def pool_mix(u_ctx, n_out, w, scale):
    N, Lc, C = u_ctx.shape
    uf = u_ctx.astype(jnp.float32)
    cs = jnp.concatenate([jnp.zeros((N, 1, C), jnp.float32), jnp.cumsum(uf, axis=1)], axis=1)
    t = jnp.arange(Lc - n_out, Lc)
    ys = []
    for gi, win in enumerate(POOL_WINDOWS):
        lo = jnp.maximum(t + 1 - win, 0)
        c0, c1 = gi * POOL_GROUP, (gi + 1) * POOL_GROUP
        csg = cs[:, :, c0:c1]
        mean = (csg[:, t + 1] - csg[:, lo]) / (t + 1 - lo).astype(jnp.float32)[None, :, None]
        ys.append(jnp.einsum('ntc,cd->ntd', mean - uf[:, t, c0:c1], w[gi].astype(jnp.float32)))
    return (jnp.concatenate(ys, axis=-1) * scale.astype(jnp.float32)).astype(u_ctx.dtype)


def gla_scan(q, k, v, la, S0, chunk):
    N, T, H, DK = q.shape
    DV = v.shape[-1]
    NC = T // chunk

    def to_chunks(a):
        return jnp.moveaxis(a.reshape(N, NC, chunk, H, a.shape[-1]), 1, 0)

    causal = (jnp.arange(chunk)[:, None] >= jnp.arange(chunk)[None, :])[None, :, :, None, None]

    def step(S, inp):
        qc, kc, vc, lc = inp
        qf, kf, vf = qc.astype(jnp.float32), kc.astype(jnp.float32), vc.astype(jnp.float32)
        b = jnp.cumsum(lc, axis=1)
        decay = jnp.exp(jnp.where(causal, b[:, :, None] - b[:, None, :], NEG))
        A = jnp.einsum('ntshd,nthd,nshd->ntsh', decay, qf, kf)
        o = jnp.einsum('ntsh,nshv->nthv', A, vf) + jnp.einsum('nthd,nhdv->nthv', qf * jnp.exp(b), S)
        bl = b[:, -1]
        S = jnp.exp(bl)[..., None] * S + jnp.einsum('nshd,nshv->nhdv', kf * jnp.exp(bl[:, None] - b), vf)
        return S, o

    S, o = lax.scan(step, S0.astype(jnp.float32), (to_chunks(q), to_chunks(k), to_chunks(v), to_chunks(la)))
    return jnp.moveaxis(o, 0, 1).reshape(N, T, H, DV).astype(v.dtype), S


def mix_out(o_nsa, y_pool, o_gla, gr, gla_gain, w_out):
    N, T = y_pool.shape[:2]
    og = rms_norm(o_gla, gla_gain).reshape(N, T, GLA_WIDTH) * jax.nn.silu(gr)
    return jnp.concatenate([o_nsa.reshape(N, T, NSA_Q_W), y_pool, og], axis=-1) @ w_out


def mem_kv(mem, kv_norm, wkv, k_gain):
    N, M, _ = mem.shape
    kv = (rms_norm(mem, kv_norm) @ wkv).reshape(N, M, 2, MEM_HEADS, HEAD_DIM)
    return jnp.stack([rms_norm(kv[:, :, 0], k_gain), kv[:, :, 1]], axis=2)


def mem_attend(x, kv, q_norm, wq, q_gain, wo):
    N, T, _ = x.shape
    q = rms_norm((rms_norm(x, q_norm) @ wq).reshape(N, T, MEM_HEADS, HEAD_DIM), q_gain)
    kv = kv.astype(x.dtype)
    s = jnp.einsum('nthd,nmhd->nhtm', q, kv[:, :, 0]).astype(jnp.float32) * SCALE
    p = jax.nn.softmax(s, axis=-1).astype(x.dtype)
    o = jnp.einsum('nhtm,nmhd->nthd', p, kv[:, :, 1]).reshape(N, T, MEM_WIDTH)
    return x + o @ wo


def setup_inputs(seed: int = 0) -> dict:
    key = jax.random.key(seed)
    kit = iter(list(jax.random.split(key, 48)))
    f32 = jnp.float32

    def nrm(shape, scale=1.0):
        return jax.random.normal(next(kit), shape, f32) * scale

    def gain(shape):
        return 1.0 + 0.1 * jax.random.normal(next(kit), shape, f32)

    n_pages = PAST_LEN // PAGE_SIZE
    used = DEC_BATCH * n_pages
    n_pool = used + max(1, used // 4)
    w_buf = min(WINDOW, PAST_LEN)
    d = D_MODEL
    inp = {}
    inp['x_prompt'] = nrm((BATCH, SEQ, d))
    inp['x_sample'] = nrm((DEC_BATCH, DEC_SEQ, d))
    inp['cache_cmp_kv'] = nrm((DEPTH, n_pool, PAGE_SIZE, 2, NSA_KV_HEADS, HEAD_DIM))
    inp['cache_sel_kv'] = nrm((DEPTH, n_pool, PAGE_SIZE, 2, NSA_KV_HEADS, HEAD_DIM))
    inp['cache_win_kv'] = nrm((DEPTH, DEC_BATCH, w_buf, 2, NSA_KV_HEADS, HEAD_DIM))
    inp['cache_pool'] = nrm((DEPTH, DEC_BATCH, POOL_BUF, POOL_WIDTH))
    inp['state_gla'] = nrm((DEPTH, DEC_BATCH, GLA_HEADS, GLA_DK, GLA_DV))
    inp['cache_mem_kv'] = nrm((DEPTH, DEC_BATCH, MEM_TOKENS, 2, MEM_HEADS, HEAD_DIM))
    inp['page_table'] = jax.random.permutation(next(kit), n_pool)[:used].reshape(DEC_BATCH, n_pages).astype(jnp.int32)
    inp['mem_prompt'] = nrm((BATCH, MEM_TOKENS, d))
    inp['ffn1_norm'] = gain((DEPTH, d))
    inp['ffn1_w_gu'] = nrm((DEPTH, d, 2 * D_FF), d ** -0.5)
    inp['ffn1_w_down'] = nrm((DEPTH, D_FF, d), D_FF ** -0.5)
    inp['mix_norm'] = gain((DEPTH, d))
    inp['w_in'] = nrm((DEPTH, d, N_IN), d ** -0.5)
    inp['nsa_q_gain'] = gain((DEPTH, HEAD_DIM))
    inp['nsa_k_gain'] = gain((DEPTH, 3, HEAD_DIM))
    inp['nsa_gate_bias'] = nrm((DEPTH, 3 * NSA_HEADS), 0.1)
    inp['nsa_cmp_wk'] = gain((DEPTH, NSA_KV_HEADS, CMP_BLOCK, HEAD_DIM)) * CMP_BLOCK ** -0.5
    inp['nsa_cmp_wv'] = gain((DEPTH, NSA_KV_HEADS, CMP_BLOCK, HEAD_DIM)) / CMP_BLOCK
    inp['pool_w'] = nrm((DEPTH, len(POOL_WINDOWS), POOL_GROUP, POOL_GROUP), POOL_GROUP ** -0.5)
    inp['pool_scale'] = gain((DEPTH, POOL_WIDTH))
    inp['gla_wa2'] = nrm((DEPTH, GLA_RANK, GLA_HEADS * GLA_DK), GLA_RANK ** -0.5)
    inp['gla_ba'] = nrm((DEPTH, GLA_HEADS * GLA_DK), 0.1)
    inp['gla_o_gain'] = gain((DEPTH, GLA_HEADS, GLA_DV))
    inp['w_out'] = nrm((DEPTH, MIX_WIDTH, d), MIX_WIDTH ** -0.5)
    inp['mem_q_norm'] = gain((DEPTH, d))
    inp['mem_kv_norm'] = gain((DEPTH, d))
    inp['mem_wq'] = nrm((DEPTH, d, MEM_WIDTH), d ** -0.5)
    inp['mem_wkv'] = nrm((DEPTH, d, 2 * MEM_WIDTH), d ** -0.5)
    inp['mem_q_gain'] = gain((DEPTH, HEAD_DIM))
    inp['mem_k_gain'] = gain((DEPTH, HEAD_DIM))
    inp['mem_wo'] = nrm((DEPTH, MEM_WIDTH, d), MEM_WIDTH ** -0.5)
    inp['ffn2_norm'] = gain((DEPTH, d))
    inp['ffn2_w_gu'] = nrm((DEPTH, d, 2 * D_FF), d ** -0.5)
    inp['ffn2_w_down'] = nrm((DEPTH, D_FF, d), D_FF ** -0.5)
    return inp


def reference(x_prompt, x_sample, cache_cmp_kv, cache_sel_kv, cache_win_kv, cache_pool, state_gla, cache_mem_kv, page_table, mem_prompt, ffn1_norm, ffn1_w_gu, ffn1_w_down, mix_norm, w_in, nsa_q_gain, nsa_k_gain, nsa_gate_bias, nsa_cmp_wk, nsa_cmp_wv, pool_w, pool_scale, gla_wa2, gla_ba, gla_o_gain, w_out, mem_q_norm, mem_kv_norm, mem_wq, mem_wkv, mem_q_gain, mem_k_gain, mem_wo, ffn2_norm, ffn2_w_gu, ffn2_w_down):
    B, S, _ = x_prompt.shape
    L = x_sample.shape[1]
    past = page_table.shape[1] * PAGE_SIZE
    pos_p = jnp.arange(S)
    pos_s = past + jnp.arange(L)
    w_keep = min(WINDOW, S)
    xp, xs = x_prompt, x_sample
    p_cmp, p_sel, p_win, p_pool, p_gla, p_mem = [], [], [], [], [], []
    s_cmp, s_sel, s_win, s_pool, s_gla = [], [], [], [], []
    for l in range(DEPTH):
        xp = half_ffn(xp, ffn1_norm[l], ffn1_w_gu[l], ffn1_w_down[l])
        q, cmp_kv, sel_kv, win_kv, gates, u, gq, gk, gv, la, gr = project(xp, pos_p, mix_norm[l], w_in[l], nsa_q_gain[l], nsa_k_gain[l], nsa_gate_bias[l], gla_wa2[l], gla_ba[l])
        o_nsa = nsa_prompt(q, cmp_kv, sel_kv, win_kv, gates, nsa_cmp_wk[l], nsa_cmp_wv[l])
        y_pool = pool_mix(u, S, pool_w[l], pool_scale[l])
        o_gla, g_state = gla_scan(gq, gk, gv, la, jnp.zeros((B, GLA_HEADS, GLA_DK, GLA_DV), jnp.float32), GLA_CHUNK)
        xp = xp + mix_out(o_nsa, y_pool, o_gla, gr, gla_o_gain[l], w_out[l])
        mkv = mem_kv(mem_prompt, mem_kv_norm[l], mem_wkv[l], mem_k_gain[l])
        xp = mem_attend(xp, mkv, mem_q_norm[l], mem_wq[l], mem_q_gain[l], mem_wo[l])
        xp = half_ffn(xp, ffn2_norm[l], ffn2_w_gu[l], ffn2_w_down[l])
        p_cmp.append(cmp_kv)
        p_sel.append(sel_kv)
        p_win.append(win_kv[:, S - w_keep:])
        p_pool.append(u[:, S - POOL_BUF:])
        p_gla.append(g_state)
        p_mem.append(mkv)
        xs = half_ffn(xs, ffn1_norm[l], ffn1_w_gu[l], ffn1_w_down[l])
        q, cmp_kv, sel_kv, win_kv, gates, u, gq, gk, gv, la, gr = project(xs, pos_s, mix_norm[l], w_in[l], nsa_q_gain[l], nsa_k_gain[l], nsa_gate_bias[l], gla_wa2[l], gla_ba[l])
        o_nsa, win_buf = nsa_sample(q, cmp_kv, sel_kv, win_kv, gates, nsa_cmp_wk[l], nsa_cmp_wv[l], cache_cmp_kv, cache_sel_kv, cache_win_kv[l], page_table, l)
        u_ctx = jnp.concatenate([cache_pool[l].astype(u.dtype), u], axis=1)
        y_pool = pool_mix(u_ctx, L, pool_w[l], pool_scale[l])
        o_gla, g_state = gla_scan(gq, gk, gv, la, state_gla[l], L)
        xs = xs + mix_out(o_nsa, y_pool, o_gla, gr, gla_o_gain[l], w_out[l])
        xs = mem_attend(xs, cache_mem_kv[l], mem_q_norm[l], mem_wq[l], mem_q_gain[l], mem_wo[l])
        xs = half_ffn(xs, ffn2_norm[l], ffn2_w_gu[l], ffn2_w_down[l])
        s_cmp.append(cmp_kv)
        s_sel.append(sel_kv)
        s_win.append(win_buf)
        s_pool.append(u_ctx[:, u_ctx.shape[1] - POOL_BUF:])
        s_gla.append(g_state)
    return (xp, xs, jnp.stack(p_cmp), jnp.stack(p_sel), jnp.stack(p_win), jnp.stack(p_pool), jnp.stack(p_gla), jnp.stack(p_mem), jnp.stack(s_cmp), jnp.stack(s_sel), jnp.stack(s_win), jnp.stack(s_pool), jnp.stack(s_gla))
```

```python
import functools

import jax
import jax.numpy as jnp
import numpy as np
from jax import lax
from jax.experimental import pallas as pl
from jax.experimental.pallas import tpu as pltpu

F32 = jnp.float32
BF16 = jnp.bfloat16

D_MODEL = 2048
HEAD_DIM = 128
ROPE_THETA = 10000.0
EPS = 1e-6
NEG = -1e30
FORCE = 1e4
SCALE = HEAD_DIM ** -0.5
PAGE_SIZE = 128
NSA_HEADS = 8
NSA_KV_HEADS = 2
NSA_HPG = NSA_HEADS // NSA_KV_HEADS
NSA_Q_W = NSA_HEADS * HEAD_DIM
NSA_KV_W = NSA_KV_HEADS * HEAD_DIM
CMP_BLOCK = 64
SEL_BLOCK = 64
SEL_TOPK = 16
WINDOW = 512
POOL_WIDTH = 512
POOL_WINDOWS = (2, 4, 8, 16)
POOL_GROUP = POOL_WIDTH // len(POOL_WINDOWS)
POOL_BUF = max(POOL_WINDOWS) - 1
POOL_HIST = 16
GLA_WIDTH = 512
GLA_HEADS = 4
GLA_DK = 64
GLA_DV = 128
GLA_RANK = 16
GLA_TAU = 16.0
GLA_CHUNK = 64
GLA_SUB = 16
GLA_QK = GLA_HEADS * GLA_DK
MEM_HEADS = 4
MEM_WIDTH = MEM_HEADS * HEAD_DIM
LANE = 128
PAD_SCORE = -3.0e38

_C_Q = 0
_C_KV = _C_Q + NSA_Q_W
_C_GT = _C_KV + 6 * NSA_KV_W
_C_U = _C_GT + 3 * NSA_HEADS
_C_GQ = _C_U + POOL_WIDTH
_C_GK = _C_GQ + GLA_QK
_C_GV = _C_GK + GLA_QK
_C_GA = _C_GV + GLA_WIDTH
_C_GR = _C_GA + GLA_RANK
_C_END = _C_GR + GLA_WIDTH

VMEM_LIMIT = 48 * 1024 * 1024


def _cparams(sem):
    return pltpu.CompilerParams(dimension_semantics=sem, vmem_limit_bytes=VMEM_LIMIT)


def _dot(a, b):
    return jnp.dot(a, b, preferred_element_type=F32)


def _dot_nt(a, b):
    return lax.dot_general(a, b, (((1,), (1,)), ((), ())), preferred_element_type=F32)


def _rms(x, g):
    return x * lax.rsqrt(jnp.mean(x * x, axis=-1, keepdims=True) + EPS) * g


def _sigmoid(x):
    return 1.0 / (1.0 + jnp.exp(-x))


def _silu(x):
    return x * _sigmoid(x)


def _log_sigmoid(x):
    return jnp.minimum(x, 0.0) - jnp.log1p(jnp.exp(-jnp.abs(x)))


def _rope(y, cos, sin):
    return y * cos + pltpu.roll(y, HEAD_DIM // 2, axis=1) * sin


def _pad_cols(w, width):
    return jnp.pad(w, ((0, 0), (0, width - w.shape[1])))


def _ffn_body(x_ref, g_ref, wg_ref, wu_ref, wd_ref, o_ref, xn_ref):
    j = pl.program_id(1)

    @pl.when(j == 0)
    def _():
        xn_ref[...] = _rms(x_ref[...], g_ref[...]).astype(BF16)
        o_ref[...] = jnp.zeros_like(o_ref)

    xn = xn_ref[...]
    a = _dot(xn, wg_ref[...])
    b = _dot(xn, wu_ref[...])
    h = (_silu(a) * b).astype(BF16)
    o_ref[...] += _dot(h, wd_ref[...])

    @pl.when(j == pl.num_programs(1) - 1)
    def _():
        o_ref[...] = x_ref[...] + 0.5 * o_ref[...]


def _ffn(x, g, w_gu, w_down, *, tm, tf=512):
    m, d = x.shape
    f = w_down.shape[0]
    nf = f // tf
    return pl.pallas_call(
        _ffn_body,
        out_shape=jax.ShapeDtypeStruct((m, d), F32),
        grid=(m // tm, nf),
        in_specs=[
            pl.BlockSpec((tm, d), lambda i, j: (i, 0)),
            pl.BlockSpec((1, d), lambda i, j: (0, 0)),
            pl.BlockSpec((d, tf), lambda i, j: (0, j)),
            pl.BlockSpec((d, tf), lambda i, j: (0, j + nf)),
            pl.BlockSpec((tf, d), lambda i, j: (j, 0)),
        ],
        out_specs=pl.BlockSpec((tm, d), lambda i, j: (i, 0)),
        scratch_shapes=[pltpu.VMEM((tm, d), BF16)],
        compiler_params=_cparams(("parallel", "arbitrary")),
        name="ffn_half_step",
    )(x, g, w_gu, w_gu, w_down)


def _qkv_body(x_ref, g_ref, w_ref, cos_ref, sin_ref, qg_ref, kg_ref, gb_ref,
              q_ref, cmp_ref, sel_ref, win_ref, gt_ref):
    xn = _rms(x_ref[...], g_ref[...]).astype(BF16)
    h = _dot(xn, w_ref[...])
    cos = cos_ref[...]
    sin = sin_ref[...]

    def head(c0, gain):
        return _rope(_rms(h[:, c0:c0 + HEAD_DIM], gain), cos, sin)

    for hh in range(NSA_HEADS):
        c0 = _C_Q + hh * HEAD_DIM
        q_ref[:, hh * HEAD_DIM:(hh + 1) * HEAD_DIM] = head(c0, qg_ref[...]).astype(BF16)
    for br, out in enumerate((cmp_ref, sel_ref, win_ref)):
        base = _C_KV + br * 2 * NSA_KV_W
        for g in range(NSA_KV_HEADS):
            out[:, g * HEAD_DIM:(g + 1) * HEAD_DIM] = head(base + g * HEAD_DIM, kg_ref[br:br + 1, :])
        out[:, NSA_KV_W:2 * NSA_KV_W] = h[:, base + NSA_KV_W:base + 2 * NSA_KV_W]
    gt_ref[...] = _sigmoid(h[:, _C_GT:_C_GT + 2 * LANE] + gb_ref[...])


def _qkv_proj(x, g, w, cos, sin, q_gain, k_gain, gate_b, *, tm, tiles_per_seq):
    m, d = x.shape
    n = w.shape[1]
    row = lambda i: (i, 0)
    fixed = lambda i: (0, 0)
    pos = lambda i: (i % tiles_per_seq, 0)
    return pl.pallas_call(
        _qkv_body,
        out_shape=(
            jax.ShapeDtypeStruct((m, NSA_Q_W), BF16),
            jax.ShapeDtypeStruct((m, 2 * NSA_KV_W), F32),
            jax.ShapeDtypeStruct((m, 2 * NSA_KV_W), F32),
            jax.ShapeDtypeStruct((m, 2 * NSA_KV_W), F32),
            jax.ShapeDtypeStruct((m, 2 * LANE), F32),
        ),
        grid=(m // tm,),
        in_specs=[
            pl.BlockSpec((tm, d), row),
            pl.BlockSpec((1, d), fixed),
            pl.BlockSpec((d, n), fixed),
            pl.BlockSpec((tm, HEAD_DIM), pos),
            pl.BlockSpec((tm, HEAD_DIM), pos),
            pl.BlockSpec((1, HEAD_DIM), fixed),
            pl.BlockSpec((3, HEAD_DIM), fixed),
            pl.BlockSpec((1, 2 * LANE), fixed),
        ],
        out_specs=(
            pl.BlockSpec((tm, NSA_Q_W), row),
            pl.BlockSpec((tm, 2 * NSA_KV_W), row),
            pl.BlockSpec((tm, 2 * NSA_KV_W), row),
            pl.BlockSpec((tm, 2 * NSA_KV_W), row),
            pl.BlockSpec((tm, 2 * LANE), row),
        ),
        compiler_params=_cparams(("parallel",)),
        name="nsa_projection",
    )(x, g, w, cos, sin, q_gain, k_gain, gate_b)


def _softmax_av(s, v):
    e = jnp.exp(s - jnp.max(s, axis=-1, keepdims=True))
    return _dot(e.astype(BF16), v) / jnp.sum(e, axis=-1, keepdims=True)


def _nsa_prompt_body(q_ref, gt_ref, kc_ref, vc_ref, ks_ref, vs_ref, kw_ref, vw_ref, wk_ref, wv_ref, o_ref):
    i = pl.program_id(2)
    tq = q_ref.shape[0]
    s_len = kc_ref.shape[0]
    nb = s_len // CMP_BLOCK
    qpos = i * tq + lax.broadcasted_iota(jnp.int32, (tq, 1), 0)

    kc = jnp.sum(kc_ref[...].reshape(nb, CMP_BLOCK, HEAD_DIM) * wk_ref[...][None], axis=1).astype(BF16)
    vc = jnp.sum(vc_ref[...].reshape(nb, CMP_BLOCK, HEAD_DIM) * wv_ref[...][None], axis=1).astype(BF16)
    blk = lax.broadcasted_iota(jnp.int32, (1, nb), 1)
    valid_c = ((blk + 1) * CMP_BLOCK - 1) <= qpos
    qs = [q_ref[:, hh * HEAD_DIM:(hh + 1) * HEAD_DIM] for hh in range(NSA_HPG)]
    imp = jnp.zeros((tq, nb), F32)
    o_c = []
    for qh in qs:
        s = jnp.where(valid_c, _dot_nt(qh, kc) * SCALE, NEG)
        e = jnp.where(valid_c, jnp.exp(s - jnp.max(s, axis=-1, keepdims=True)), 0.0)
        p = e / jnp.maximum(jnp.sum(e, axis=-1, keepdims=True), 1e-30)
        imp = imp + p
        o_c.append(_dot(p.astype(BF16), vc))

    cur = qpos // SEL_BLOCK
    forced = jnp.where(blk == cur, 1, jnp.where(blk == cur - 1, 1, jnp.where(blk == 0, 1, 0)))
    score = jnp.where(blk > cur, -FORCE, jnp.where(forced > 0, FORCE, imp))
    rank = jnp.zeros((tq, nb), jnp.int32)
    for b2 in range(nb):
        col = score[:, b2:b2 + 1]
        tie = jnp.where(col == score, jnp.where(blk > b2, 1, 0), 0)
        rank = rank + jnp.where(col > score, 1, tie)
    keep = jnp.where(rank < min(SEL_TOPK, nb), 1.0, 0.0).astype(BF16)
    expand = jnp.where(
        lax.broadcasted_iota(jnp.int32, (nb, s_len), 1) // SEL_BLOCK
        == lax.broadcasted_iota(jnp.int32, (nb, s_len), 0), 1.0, 0.0).astype(BF16)
    keep_keys = _dot(keep, expand)
    kpos = lax.broadcasted_iota(jnp.int32, (1, s_len), 1)
    bias_s = jnp.where(keep_keys > 0.5, jnp.where(kpos <= qpos, 0.0, NEG), NEG)
    ks = ks_ref[...].astype(BF16)
    vs = vs_ref[...].astype(BF16)
    o_s = [_softmax_av(_dot_nt(qh, ks) * SCALE + bias_s, vs) for qh in qs]

    span = WINDOW + tq
    kstart = pl.multiple_of(jnp.maximum(i * tq - WINDOW, 0), tq)
    kw = kw_ref[pl.ds(kstart, span), :].astype(BF16)
    vw = vw_ref[pl.ds(kstart, span), :].astype(BF16)
    rel = qpos - (kstart + lax.broadcasted_iota(jnp.int32, (1, span), 1))
    bias_w = jnp.where(rel >= 0, jnp.where(rel <= WINDOW, 0.0, NEG), NEG)
    o_w = [_softmax_av(_dot_nt(qh, kw) * SCALE + bias_w, vw) for qh in qs]

    gt = gt_ref[...]
    for hh in range(NSA_HPG):
        o = (gt[:, 3 * hh:3 * hh + 1] * o_c[hh] + gt[:, 3 * hh + 1:3 * hh + 2] * o_s[hh]
             + gt[:, 3 * hh + 2:3 * hh + 3] * o_w[hh])
        o_ref[:, hh * HEAD_DIM:(hh + 1) * HEAD_DIM] = o.astype(BF16)


def _nsa_prompt(q, gates, cmp_kv, sel_kv, win_kv, wk, wv, *, n_seq, s_len, tq):
    m = q.shape[0]
    nt = s_len // tq
    assert s_len >= WINDOW + tq and s_len % tq == 0
    gw = NSA_HPG * HEAD_DIM
    row = lambda n, g, i: (n * nt + i, g)
    kspec = pl.BlockSpec((s_len, HEAD_DIM), lambda n, g, i: (n, g))
    vspec = pl.BlockSpec((s_len, HEAD_DIM), lambda n, g, i: (n, NSA_KV_HEADS + g))
    wspec = pl.BlockSpec((None, CMP_BLOCK, HEAD_DIM), lambda n, g, i: (g, 0, 0))
    return pl.pallas_call(
        _nsa_prompt_body,
        out_shape=jax.ShapeDtypeStruct((m, NSA_Q_W), BF16),
        grid=(n_seq, NSA_KV_HEADS, nt),
        in_specs=[
            pl.BlockSpec((tq, gw), row),
            pl.BlockSpec((tq, LANE), row),
            kspec, vspec, kspec, vspec, kspec, vspec, wspec, wspec,
        ],
        out_specs=pl.BlockSpec((tq, gw), row),
        compiler_params=_cparams(("parallel", "parallel", "parallel")),
        name="nsa_prompt",
    )(q, gates, cmp_kv, cmp_kv, sel_kv, sel_kv, win_kv, win_kv, wk, wv)


def _pool_prompt_body(x_ref, g_ref, w_ref, pw_ref, ps_ref, u_ref, y_ref, hist_ref):
    j = pl.program_id(1)
    tm = x_ref.shape[0]

    @pl.when(j == 0)
    def _():
        hist_ref[...] = jnp.zeros_like(hist_ref)

    u = _dot(_rms(x_ref[...], g_ref[...]).astype(BF16), w_ref[...])
    u_ref[...] = u
    ext = jnp.concatenate([hist_ref[...], u], axis=0)
    t = j * tm + lax.broadcasted_iota(jnp.int32, (tm, 1), 0)
    for gi, win in enumerate(POOL_WINDOWS):
        c0, c1 = gi * POOL_GROUP, (gi + 1) * POOL_GROUP
        acc = ext[:, c0:c1]
        span = 1
        while span < win:
            acc = acc + pltpu.roll(acc, span, axis=0)
            span *= 2
        mean = acc[POOL_HIST:] / jnp.minimum(t + 1, win).astype(F32)
        y = _dot((mean - u[:, c0:c1]).astype(BF16), pw_ref[gi].astype(BF16))
        y_ref[:, c0:c1] = (y * ps_ref[:, c0:c1]).astype(BF16)
    hist_ref[...] = u[tm - POOL_HIST:]


def _pool_prompt(x, g, w_u, pool_w, pool_scale, *, n_seq, s_len, tm):
    d = x.shape[1]
    nt = s_len // tm
    row = lambda n, j: (n * nt + j, 0)
    fixed = lambda n, j: (0, 0)
    return pl.pallas_call(
        _pool_prompt_body,
        out_shape=(jax.ShapeDtypeStruct((n_seq * s_len, POOL_WIDTH), F32),
                   jax.ShapeDtypeStruct((n_seq * s_len, POOL_WIDTH), BF16)),
        grid=(n_seq, nt),
        in_specs=[
            pl.BlockSpec((tm, d), row),
            pl.BlockSpec((1, d), fixed),
            pl.BlockSpec((d, POOL_WIDTH), fixed),
            pl.BlockSpec((len(POOL_WINDOWS), POOL_GROUP, POOL_GROUP), lambda n, j: (0, 0, 0)),
            pl.BlockSpec((1, POOL_WIDTH), fixed),
        ],
        out_specs=(pl.BlockSpec((tm, POOL_WIDTH), row), pl.BlockSpec((tm, POOL_WIDTH), row)),
        scratch_shapes=[pltpu.VMEM((POOL_HIST, POOL_WIDTH), F32)],
        compiler_params=_cparams(("parallel", "arbitrary")),
        name="pool_prompt",
    )(x, g, w_u, pool_w, pool_scale)


def _gla_prompt_body(x_ref, g_ref, w_ref, wa2_ref, ba_ref, gain_ref, og_ref, st_ref, state_ref):
    j = pl.program_id(1)
    tm = x_ref.shape[0]

    @pl.when(j == 0)
    def _():
        state_ref[...] = jnp.zeros_like(state_ref)

    h = _dot(_rms(x_ref[...], g_ref[...]).astype(BF16), w_ref[...])
    c_k, c_v, c_r, c_a = GLA_QK, 2 * GLA_QK, 2 * GLA_QK + GLA_WIDTH, 2 * GLA_QK + 2 * GLA_WIDTH
    z = _dot(h[:, c_a:c_a + LANE].astype(BF16), wa2_ref[...]) + ba_ref[...]
    la_all = _log_sigmoid(z) / GLA_TAU

    lane = lax.broadcasted_iota(jnp.int32, (1, GLA_QK), 1)
    head_mask = [jnp.where(lane // GLA_DK == hd, 1.0, 0.0) for hd in range(GLA_HEADS)]
    rows = lax.broadcasted_iota(jnp.int32, (GLA_CHUNK, 1), 0)
    causal = rows >= lax.broadcasted_iota(jnp.int32, (1, GLA_CHUNK), 1)
    n_sub = GLA_CHUNK // GLA_SUB

    for c in range(tm // GLA_CHUNK):
        r0, r1 = c * GLA_CHUNK, (c + 1) * GLA_CHUNK
        q = h[r0:r1, 0:GLA_QK] * (GLA_DK ** -0.5)
        k = h[r0:r1, c_k:c_k + GLA_QK]
        b = la_all[r0:r1]
        sh = 1
        while sh < GLA_CHUNK:
            b = b + jnp.where(rows >= sh, pltpu.roll(b, sh, axis=0), 0.0)
            sh *= 2
        state = state_ref[...]
        state_b = state.astype(BF16)
        q_in = q * jnp.exp(b)
        a_parts = [[] for _ in range(GLA_HEADS)]
        for sb in range(n_sub):
            s0, s1 = sb * GLA_SUB, (sb + 1) * GLA_SUB
            ref_row = b[s0 - 1:s0] if sb > 0 else jnp.zeros((1, GLA_QK), F32)
            q_t = q[s0:s1] * jnp.exp(b[s0:s1] - ref_row)
            k_t = (k * jnp.exp(jnp.where(rows < s1, ref_row - b, 0.0))).astype(BF16)
            for hd in range(GLA_HEADS):
                a_parts[hd].append(_dot_nt((q_t * head_mask[hd]).astype(BF16), k_t))
        b_last = b[GLA_CHUNK - 1:GLA_CHUNK]
        k_out = k * jnp.exp(b_last - b)
        new_state = state * jnp.exp(b_last)
        for hd in range(GLA_HEADS):
            v = h[r0:r1, c_v + hd * GLA_DV:c_v + (hd + 1) * GLA_DV]
            vb = v.astype(BF16)
            a = jnp.where(causal, jnp.concatenate(a_parts[hd], axis=0), 0.0)
            o = _dot(a.astype(BF16), vb) + _dot_nt((q_in * head_mask[hd]).astype(BF16), state_b)
            gr = h[r0:r1, c_r + hd * GLA_DV:c_r + (hd + 1) * GLA_DV]
            og = _rms(o, gain_ref[:, hd * GLA_DV:(hd + 1) * GLA_DV]) * _silu(gr)
            og_ref[r0:r1, hd * GLA_DV:(hd + 1) * GLA_DV] = og.astype(BF16)
            new_state = new_state + _dot(v.T.astype(BF16), (k_out * head_mask[hd]).astype(BF16))
        state_ref[...] = new_state

    @pl.when(j == pl.num_programs(1) - 1)
    def _():
        st_ref[...] = state_ref[...]


def _gla_prompt(x, g, w_gla, wa2, ba, gain, *, n_seq, s_len, tm):
    d = x.shape[1]
    nt = s_len // tm
    row = lambda n, j: (n * nt + j, 0)
    fixed = lambda n, j: (0, 0)
    return pl.pallas_call(
        _gla_prompt_body,
        out_shape=(jax.ShapeDtypeStruct((n_seq * s_len, GLA_WIDTH), BF16),
                   jax.ShapeDtypeStruct((n_seq, GLA_DV, GLA_QK), F32)),
        grid=(n_seq, nt),
        in_specs=[
            pl.BlockSpec((tm, d), row),
            pl.BlockSpec((1, d), fixed),
            pl.BlockSpec((d, w_gla.shape[1]), fixed),
            pl.BlockSpec((LANE, GLA_QK), fixed),
            pl.BlockSpec((1, GLA_QK), fixed),
            pl.BlockSpec((1, GLA_WIDTH), fixed),
        ],
        out_specs=(pl.BlockSpec((tm, GLA_WIDTH), row),
                   pl.BlockSpec((None, GLA_DV, GLA_QK), lambda n, j: (n, 0, 0))),
        scratch_shapes=[pltpu.VMEM((GLA_DV, GLA_QK), F32)],
        compiler_params=_cparams(("parallel", "arbitrary")),
        name="gla_prompt",
    )(x, g, w_gla, wa2, ba, gain)


def _norm_matmul_body(x_ref, g_ref, w_ref, hg_ref, o_ref, *, n_norm_heads):
    h = _dot(_rms(x_ref[...], g_ref[...]).astype(BF16), w_ref[...])
    if n_norm_heads:
        for hd in range(n_norm_heads):
            c0 = hd * HEAD_DIM
            o_ref[:, c0:c0 + HEAD_DIM] = _rms(h[:, c0:c0 + HEAD_DIM], hg_ref[...])
        c0 = n_norm_heads * HEAD_DIM
        o_ref[:, c0:] = h[:, c0:]
    else:
        o_ref[...] = h


def _norm_matmul(x, g, w, head_gain, *, tm, n_norm_heads=0):
    m, d = x.shape
    n = w.shape[1]
    return pl.pallas_call(
        functools.partial(_norm_matmul_body, n_norm_heads=n_norm_heads),
        out_shape=jax.ShapeDtypeStruct((m, n), F32),
        grid=(m // tm,),
        in_specs=[
            pl.BlockSpec((tm, d), lambda i: (i, 0)),
            pl.BlockSpec((1, d), lambda i: (0, 0)),
            pl.BlockSpec((d, n), lambda i: (0, 0)),
            pl.BlockSpec((1, HEAD_DIM), lambda i: (0, 0)),
        ],
        out_specs=pl.BlockSpec((tm, n), lambda i: (i, 0)),
        compiler_params=_cparams(("parallel",)),
        name="norm_matmul",
    )(x, g, w, head_gain)


def _post_body(x_ref, on_ref, yp_ref, og_ref, wout_ref, mk_ref, mv_ref, qn_ref, wq_ref, qg_ref, wo_ref, o_ref):
    c1, c2 = NSA_Q_W, NSA_Q_W + POOL_WIDTH
    x1 = (x_ref[...] + _dot(on_ref[...], wout_ref[0:c1, :]) + _dot(yp_ref[...], wout_ref[c1:c2, :])
          + _dot(og_ref[...], wout_ref[c2:, :]))
    q = _dot(_rms(x1, qn_ref[...]).astype(BF16), wq_ref[...])
    outs = []
    for hd in range(MEM_HEADS):
        c0 = hd * HEAD_DIM
        qh = _rms(q[:, c0:c0 + HEAD_DIM], qg_ref[...]).astype(BF16)
        s = _dot_nt(qh, mk_ref[:, c0:c0 + HEAD_DIM].astype(BF16)) * SCALE
        outs.append(_softmax_av(s, mv_ref[:, c0:c0 + HEAD_DIM].astype(BF16)).astype(BF16))
    o_ref[...] = x1 + _dot(jnp.concatenate(outs, axis=1), wo_ref[...])


def _post_mix(x, o_nsa, y_pool, o_gla, w_out, mem_kv, q_norm, wq, q_gain, wo, *, n_seq, t_len, tm):
    d = x.shape[1]
    nt = t_len // tm
    mt = mem_kv.shape[1]
    row = lambda n, j: (n * nt + j, 0)
    fixed = lambda n, j: (0, 0)
    return pl.pallas_call(
        _post_body,
        out_shape=jax.ShapeDtypeStruct(x.shape, F32),
        grid=(n_seq, nt),
        in_specs=[
            pl.BlockSpec((tm, d), row),
            pl.BlockSpec((tm, NSA_Q_W), row),
            pl.BlockSpec((tm, POOL_WIDTH), row),
            pl.BlockSpec((tm, GLA_WIDTH), row),
            pl.BlockSpec(w_out.shape, fixed),
            pl.BlockSpec((None, mt, MEM_WIDTH), lambda n, j: (n, 0, 0)),
            pl.BlockSpec((None, mt, MEM_WIDTH), lambda n, j: (n, 0, 1)),
            pl.BlockSpec((1, d), fixed),
            pl.BlockSpec((d, MEM_WIDTH), fixed),
            pl.BlockSpec((1, HEAD_DIM), fixed),
            pl.BlockSpec((MEM_WIDTH, d), fixed),
        ],
        out_specs=pl.BlockSpec((tm, d), row),
        compiler_params=_cparams(("parallel", "parallel")),
        name="mix_out_mem_attn",
    )(x, o_nsa, y_pool, o_gla, w_out, mem_kv, mem_kv, q_norm, wq, q_gain, wo)


PAGES_PER_STEP = 8


def _cmp_pages_body(pt_ref, *refs):
    del pt_ref
    pages, w_ref, o_ref = refs[:PAGES_PER_STEP], refs[PAGES_PER_STEP], refs[PAGES_PER_STEP + 1]
    w = w_ref[...]
    rows = []
    for page in pages:
        for half in range(PAGE_SIZE // CMP_BLOCK):
            blk = page[half * CMP_BLOCK:(half + 1) * CMP_BLOCK, :]
            rows.append(jnp.sum(blk * w, axis=0, keepdims=True))
    o_ref[...] = jnp.concatenate(rows, axis=0)


def _cmp_pages(cache, page_table, w_cmp, layer):
    n_seq, n_pages = page_table.shape
    width = cache.shape[-1]
    bpp = PAGE_SIZE // CMP_BLOCK
    assert n_pages % PAGES_PER_STEP == 0

    def page_spec(k):
        return pl.BlockSpec((None, None, PAGE_SIZE, width),
                            lambda n, j, pt: (layer, pt[n, j * PAGES_PER_STEP + k], 0, 0))

    return pl.pallas_call(
        _cmp_pages_body,
        out_shape=jax.ShapeDtypeStruct((n_seq, n_pages * bpp, width), F32),
        grid_spec=pltpu.PrefetchScalarGridSpec(
            num_scalar_prefetch=1,
            grid=(n_seq, n_pages // PAGES_PER_STEP),
            in_specs=[page_spec(k) for k in range(PAGES_PER_STEP)]
            + [pl.BlockSpec((CMP_BLOCK, width), lambda n, j, pt: (0, 0))],
            out_specs=pl.BlockSpec((None, PAGES_PER_STEP * bpp, width), lambda n, j, pt: (n, j, 0)),
        ),
        compiler_params=_cparams(("parallel", "arbitrary")),
        name="cmp_pages",
    )(page_table, *([cache] * PAGES_PER_STEP), w_cmp)


def _cmp_select_body(q_ref, c_ref, oc_ref, idx_ref, *, n_past_blocks):
    n_seq = q_ref.shape[0]
    nbp = n_past_blocks
    width = idx_ref.shape[-1]
    cur = nbp
    q = q_ref[...]
    lane = lax.broadcasted_iota(jnp.int32, (n_seq, width), 1)
    lane_f = lane.astype(F32)
    row_head = lax.broadcasted_iota(jnp.int32, (1, NSA_HEADS, 1), 1)
    o_c = jnp.zeros((n_seq, NSA_HEADS, HEAD_DIM), F32)
    for g in range(NSA_KV_HEADS):
        kc = c_ref[:, :, g * HEAD_DIM:(g + 1) * HEAD_DIM].astype(BF16)
        vc = c_ref[:, :, NSA_KV_W + g * HEAD_DIM:NSA_KV_W + (g + 1) * HEAD_DIM].astype(BF16)
        s = lax.dot_general(q, kc, (((2,), (2,)), ((0,), (0,))), preferred_element_type=F32) * SCALE
        e = jnp.exp(s - jnp.max(s, axis=-1, keepdims=True))
        p = e / jnp.maximum(jnp.sum(e, axis=-1, keepdims=True), 1e-30)
        o_g = lax.dot_general(p.astype(BF16), vc, (((2,), (1,)), ((0,), (0,))), preferred_element_type=F32)
        in_group = row_head // NSA_HPG == g
        o_c = jnp.where(in_group, o_g, o_c)
        imp = jnp.sum(jnp.where(in_group, p, 0.0), axis=1)
        imp = jnp.concatenate([imp, jnp.zeros((n_seq, width - nbp), F32)], axis=1)
        forced = jnp.where(lane == cur, 1, jnp.where(lane == cur - 1, 1, jnp.where(lane == 0, 1, 0)))
        score = jnp.where(lane > cur, PAD_SCORE, jnp.where(forced > 0, FORCE, imp))
        picks = jnp.zeros((n_seq, width), F32)
        for it in range(SEL_TOPK):
            top = jnp.max(score, axis=-1, keepdims=True)
            first = jnp.min(jnp.where(score == top, lane_f, float(width)), axis=-1, keepdims=True)
            picks = jnp.where(lane == it, first, picks)
            score = jnp.where(lane_f == first, PAD_SCORE, score)
        idx_ref[g] = picks.astype(jnp.int32)
    oc_ref[...] = o_c


def _cmp_select(q3, cmpc):
    n_seq, nbp, width = cmpc.shape
    assert nbp + 1 >= SEL_TOPK and nbp % LANE == 0
    idx_w = nbp + LANE
    return pl.pallas_call(
        functools.partial(_cmp_select_body, n_past_blocks=nbp),
        out_shape=(jax.ShapeDtypeStruct((n_seq, NSA_HEADS, HEAD_DIM), F32),
                   jax.ShapeDtypeStruct((NSA_KV_HEADS, n_seq, idx_w), jnp.int32)),
        grid=(1,),
        in_specs=[
            pl.BlockSpec((n_seq, NSA_HEADS, HEAD_DIM), lambda i: (0, 0, 0)),
            pl.BlockSpec((n_seq, nbp, width), lambda i: (0, 0, 0)),
        ],
        out_specs=(pl.BlockSpec((n_seq, NSA_HEADS, HEAD_DIM), lambda i: (0, 0, 0)),
                   pl.BlockSpec((NSA_KV_HEADS, n_seq, idx_w), lambda i: (0, 0, 0))),
        compiler_params=_cparams(("arbitrary",)),
        name="cmp_select",
    )(q3, cmpc)


def _one_token_attn(q, k_past, v_past, k_new, v_new, bias):
    s = _dot_nt(q, k_past) * SCALE
    if bias is not None:
        s = s + bias
    s_new = jnp.sum(q.astype(F32) * k_new, axis=-1, keepdims=True) * SCALE
    top = jnp.maximum(jnp.max(s, axis=-1, keepdims=True), s_new)
    e = jnp.exp(s - top)
    e_new = jnp.exp(s_new - top)
    return (_dot(e.astype(BF16), v_past) + e_new * v_new) / (jnp.sum(e, axis=-1, keepdims=True) + e_new)


def _sel_win_body(blk_ref, new_ref, *refs):
    del blk_ref
    n = pl.program_id(0)
    nk = SEL_TOPK
    sel_refs = refs[:2 * NSA_KV_HEADS * nk]
    win_refs = refs[2 * NSA_KV_HEADS * nk:2 * NSA_KV_HEADS * nk + 2 * NSA_KV_HEADS]
    q_ref, sn_ref, wn_ref, oc_ref, gt_ref, o_ref = refs[2 * NSA_KV_HEADS * nk + 2 * NSA_KV_HEADS:]
    q = q_ref[...]
    sel_new = sn_ref[...]
    win_new = wn_ref[...]
    gt = gt_ref[...]
    row_head = lax.broadcasted_iota(jnp.int32, (NSA_HEADS, 1), 0)
    key_blk = lax.broadcasted_iota(jnp.int32, (1, nk * SEL_BLOCK), 1) // SEL_BLOCK
    o_s = jnp.zeros((NSA_HEADS, HEAD_DIM), F32)
    o_w = jnp.zeros((NSA_HEADS, HEAD_DIM), F32)
    for g in range(NSA_KV_HEADS):
        k0, v0 = g * HEAD_DIM, NSA_KV_W + g * HEAD_DIM
        kr = sel_refs[(2 * g) * nk:(2 * g + 1) * nk]
        vr = sel_refs[(2 * g + 1) * nk:(2 * g + 2) * nk]
        k_cat = jnp.concatenate([r[...] for r in kr], axis=0).astype(BF16)
        v_cat = jnp.concatenate([r[...] for r in vr], axis=0).astype(BF16)
        bias = jnp.where(key_blk == new_ref[n * NSA_KV_HEADS + g], NEG, 0.0)
        o_sg = _one_token_attn(q, k_cat, v_cat, sel_new[:, k0:k0 + HEAD_DIM], sel_new[:, v0:v0 + HEAD_DIM], bias)
        o_wg = _one_token_attn(q, win_refs[2 * g][...].astype(BF16), win_refs[2 * g + 1][...].astype(BF16),
                               win_new[:, k0:k0 + HEAD_DIM], win_new[:, v0:v0 + HEAD_DIM], None)
        in_group = row_head // NSA_HPG == g
        o_s = jnp.where(in_group, o_sg, o_s)
        o_w = jnp.where(in_group, o_wg, o_w)
    o_ref[...] = (gt[:, 0:1] * oc_ref[...] + gt[:, 1:2] * o_s + gt[:, 2:3] * o_w).astype(BF16)


def _sel_win(blk_rows, new_slot, cache_sel, cache_win, q3, sel_new, win_new, o_c, gates3, layer):
    n_seq = q3.shape[0]
    nk = SEL_TOPK
    w_buf = cache_win.shape[2]

    def sel_spec(g, k, col):
        return pl.BlockSpec((None, None, SEL_BLOCK, HEAD_DIM),
                            lambda n, br, ns: (layer, br[(n * NSA_KV_HEADS + g) * nk + k], 0, col))

    def win_spec(col):
        return pl.BlockSpec((None, None, w_buf, HEAD_DIM), lambda n, br, ns: (layer, n, 0, col))

    sel_specs, win_specs = [], []
    for g in range(NSA_KV_HEADS):
        sel_specs += [sel_spec(g, k, g) for k in range(nk)]
        sel_specs += [sel_spec(g, k, NSA_KV_HEADS + g) for k in range(nk)]
        win_specs += [win_spec(g), win_spec(NSA_KV_HEADS + g)]
    tok = lambda n, br, ns: (n, 0, 0)
    return pl.pallas_call(
        _sel_win_body,
        out_shape=jax.ShapeDtypeStruct((n_seq, NSA_HEADS, HEAD_DIM), BF16),
        grid_spec=pltpu.PrefetchScalarGridSpec(
            num_scalar_prefetch=2,
            grid=(n_seq,),
            in_specs=sel_specs + win_specs + [
                pl.BlockSpec((None, NSA_HEADS, HEAD_DIM), tok),
                pl.BlockSpec((None, 1, 2 * NSA_KV_W), tok),
                pl.BlockSpec((None, 1, 2 * NSA_KV_W), tok),
                pl.BlockSpec((None, NSA_HEADS, HEAD_DIM), tok),
                pl.BlockSpec((None, NSA_HEADS, 3), tok),
            ],
            out_specs=pl.BlockSpec((None, NSA_HEADS, HEAD_DIM), tok),
        ),
        compiler_params=_cparams(("parallel",)),
        name="sel_win_sample",
    )(blk_rows, new_slot, *([cache_sel] * len(sel_specs)), *([cache_win] * len(win_specs)),
      q3, sel_new, win_new, o_c, gates3)


def _pool_sample_body(uc_ref, pw_ref, ps_ref, y_ref):
    rows = uc_ref.shape[0]
    for gi, win in enumerate(POOL_WINDOWS):
        c0, c1 = gi * POOL_GROUP, (gi + 1) * POOL_GROUP
        acc = uc_ref[rows - 1, :, c0:c1]
        for r in range(rows - 2, rows - 1 - win, -1):
            acc = acc + uc_ref[r, :, c0:c1]
        d = acc / float(win) - uc_ref[rows - 1, :, c0:c1]
        y_ref[:, c0:c1] = (_dot(d.astype(BF16), pw_ref[gi].astype(BF16)) * ps_ref[:, c0:c1]).astype(BF16)


def _pool_sample(u_ctx_t, pool_w, pool_scale):
    rows, n_seq, width = u_ctx_t.shape
    assert rows >= max(POOL_WINDOWS)
    return pl.pallas_call(
        _pool_sample_body,
        out_shape=jax.ShapeDtypeStruct((n_seq, width), BF16),
        name="pool_sample",
    )(u_ctx_t, pool_w, pool_scale)


def _gla_sample_body(q_ref, k_ref, v_ref, ga_ref, gr_ref, s_ref, wa2t_ref, ba_ref, gain_ref, og_ref, so_ref):
    z = jnp.sum(wa2t_ref[...] * ga_ref[...], axis=-1, keepdims=True) + ba_ref[...]
    decay = jnp.exp(_log_sigmoid(z) / GLA_TAU)
    v = v_ref[...]
    v_rows = jnp.concatenate(
        [jnp.broadcast_to(v[:, hd * GLA_DV:(hd + 1) * GLA_DV], (GLA_DK, GLA_DV)) for hd in range(GLA_HEADS)], axis=0)
    state = decay * s_ref[...] + k_ref[...] * v_rows
    so_ref[...] = state
    qs = (q_ref[...] * (GLA_DK ** -0.5)) * state
    gr = gr_ref[...]
    for hd in range(GLA_HEADS):
        c0, c1 = hd * GLA_DV, (hd + 1) * GLA_DV
        o = jnp.sum(qs[hd * GLA_DK:(hd + 1) * GLA_DK], axis=0, keepdims=True)
        og_ref[:, c0:c1] = (_rms(o, gain_ref[:, c0:c1]) * _silu(gr[:, c0:c1])).astype(BF16)


def _gla_sample(q_col, k_col, v, ga, gr, state, wa2_t, ba_col, gain):
    n_seq = v.shape[0]
    tok = lambda n: (n, 0, 0)
    fixed = lambda n: (0, 0)
    return pl.pallas_call(
        _gla_sample_body,
        out_shape=(jax.ShapeDtypeStruct((n_seq, 1, GLA_WIDTH), BF16),
                   jax.ShapeDtypeStruct((n_seq, GLA_QK, GLA_DV), F32)),
        grid=(n_seq,),
        in_specs=[
            pl.BlockSpec((None, GLA_QK, 1), tok),
            pl.BlockSpec((None, GLA_QK, 1), tok),
            pl.BlockSpec((None, 1, GLA_WIDTH), tok),
            pl.BlockSpec((None, 1, LANE), tok),
            pl.BlockSpec((None, 1, GLA_WIDTH), tok),
            pl.BlockSpec((None, GLA_QK, GLA_DV), tok),
            pl.BlockSpec((GLA_QK, LANE), fixed),
            pl.BlockSpec((GLA_QK, 1), fixed),
            pl.BlockSpec((1, GLA_WIDTH), fixed),
        ],
        out_specs=(pl.BlockSpec((None, 1, GLA_WIDTH), tok), pl.BlockSpec((None, GLA_QK, GLA_DV), tok)),
        compiler_params=_cparams(("parallel",)),
        name="gla_sample",
    )(q_col, k_col, v, ga, gr, state, wa2_t, ba_col, gain)


def _rope_tables(pos):
    half = HEAD_DIM // 2
    inv = ROPE_THETA ** (-jnp.arange(half, dtype=F32) / half)
    ang = pos.astype(F32)[:, None] * inv[None, :]
    cos, sin = jnp.cos(ang), jnp.sin(ang)
    return jnp.concatenate([cos, cos], axis=1), jnp.concatenate([-sin, sin], axis=1)


def _layer_weights(p, l):
    w_in = p['w_in'][l]
    n_gate = 3 * NSA_HPG
    gate_cols = [_pad_cols(w_in[:, _C_GT + g * n_gate:_C_GT + (g + 1) * n_gate], LANE) for g in range(NSA_KV_HEADS)]
    gate_b = p['nsa_gate_bias'][l]
    ga_cols = _pad_cols(w_in[:, _C_GA:_C_GR], LANE)
    gla_cols = [w_in[:, _C_GQ:_C_GA], w_in[:, _C_GR:_C_END], ga_cols]
    row = lambda a: a.reshape(1, -1)
    return dict(
        ffn1_norm=row(p['ffn1_norm'][l]), ffn1_gu=p['ffn1_w_gu'][l].astype(BF16), ffn1_down=p['ffn1_w_down'][l].astype(BF16),
        ffn2_norm=row(p['ffn2_norm'][l]), ffn2_gu=p['ffn2_w_gu'][l].astype(BF16), ffn2_down=p['ffn2_w_down'][l].astype(BF16),
        mix_norm=row(p['mix_norm'][l]),
        w_qkv=jnp.concatenate([w_in[:, _C_Q:_C_GT]] + gate_cols, axis=1).astype(BF16),
        gate_b=jnp.concatenate([_pad_cols(row(gate_b[g * n_gate:(g + 1) * n_gate]), LANE) for g in range(NSA_KV_HEADS)], axis=1),
        q_gain=row(p['nsa_q_gain'][l]), k_gain=p['nsa_k_gain'][l],
        wk=p['nsa_cmp_wk'][l], wv=p['nsa_cmp_wv'][l],
        w_cmp=jnp.concatenate([p['nsa_cmp_wk'][l, g] for g in range(NSA_KV_HEADS)]
                              + [p['nsa_cmp_wv'][l, g] for g in range(NSA_KV_HEADS)], axis=1),
        w_u=w_in[:, _C_U:_C_GQ].astype(BF16),
        w_gla=jnp.concatenate(gla_cols, axis=1).astype(BF16),
        w_sample=jnp.concatenate([w_in[:, _C_U:_C_GQ]] + gla_cols, axis=1).astype(BF16),
        pool_w=p['pool_w'][l], pool_scale=row(p['pool_scale'][l]),
        wa2=jnp.pad(p['gla_wa2'][l], ((0, LANE - GLA_RANK), (0, 0))).astype(BF16),
        wa2_t=_pad_cols(p['gla_wa2'][l].T, LANE),
        ba=row(p['gla_ba'][l]), ba_col=p['gla_ba'][l].reshape(-1, 1),
        gla_gain=row(p['gla_o_gain'][l]),
        w_out=p['w_out'][l].astype(BF16),
        mem_q_norm=row(p['mem_q_norm'][l]), mem_kv_norm=row(p['mem_kv_norm'][l]),
        mem_wq=p['mem_wq'][l].astype(BF16), mem_wkv=p['mem_wkv'][l].astype(BF16),
        mem_q_gain=row(p['mem_q_gain'][l]), mem_k_gain=row(p['mem_k_gain'][l]),
        mem_wo=p['mem_wo'][l].astype(BF16),
    )


def _prompt_layer(x, mem, w, tables, *, n_seq, s_len, tm):
    cos, sin = tables
    x = _ffn(x, w['ffn1_norm'], w['ffn1_gu'], w['ffn1_down'], tm=min(512, n_seq * s_len))
    q, cmp_kv, sel_kv, win_kv, gates = _qkv_proj(
        x, w['mix_norm'], w['w_qkv'], cos, sin, w['q_gain'], w['k_gain'], w['gate_b'], tm=tm, tiles_per_seq=s_len // tm)
    o_nsa = _nsa_prompt(q, gates, cmp_kv, sel_kv, win_kv, w['wk'], w['wv'], n_seq=n_seq, s_len=s_len, tq=tm)
    u, y_pool = _pool_prompt(x, w['mix_norm'], w['w_u'], w['pool_w'], w['pool_scale'], n_seq=n_seq, s_len=s_len, tm=tm)
    o_gla, state_t = _gla_prompt(x, w['mix_norm'], w['w_gla'], w['wa2'], w['ba'], w['gla_gain'],
                                 n_seq=n_seq, s_len=s_len, tm=tm)
    mkv = _norm_matmul(mem, w['mem_kv_norm'], w['mem_wkv'], w['mem_k_gain'], tm=tm, n_norm_heads=MEM_HEADS)
    mt = mem.shape[0] // n_seq
    x = _post_mix(x, o_nsa, y_pool, o_gla, w['w_out'], mkv.reshape(n_seq, mt, 2 * MEM_WIDTH), w['mem_q_norm'],
                  w['mem_wq'], w['mem_q_gain'], w['mem_wo'], n_seq=n_seq, t_len=s_len, tm=tm)
    x = _ffn(x, w['ffn2_norm'], w['ffn2_gu'], w['ffn2_down'], tm=min(512, n_seq * s_len))
    w_keep = min(WINDOW, s_len)
    kv_shape = (n_seq, s_len, 2, NSA_KV_HEADS, HEAD_DIM)
    state = state_t.reshape(n_seq, GLA_DV, GLA_HEADS, GLA_DK).transpose(0, 2, 3, 1)
    outs = (cmp_kv.reshape(kv_shape), sel_kv.reshape(kv_shape), win_kv.reshape(kv_shape)[:, s_len - w_keep:],
            u.reshape(n_seq, s_len, POOL_WIDTH)[:, s_len - POOL_BUF:], state,
            mkv.reshape(n_seq, mt, 2, MEM_HEADS, HEAD_DIM))
    return x, outs


SAMPLE_PAD = 8


def _sample_layer(x, w, tables, caches, page_table, layer):
    cache_cmp, cache_sel, cache_win, cache_pool, state_gla, cache_mem = caches
    cos, sin = tables
    n_seq = x.shape[0]
    n_pages = page_table.shape[1]
    bpp = PAGE_SIZE // SEL_BLOCK
    nbp = n_pages * bpp
    kvw = 2 * NSA_KV_W
    x = _ffn(x, w['ffn1_norm'], w['ffn1_gu'], w['ffn1_down'], tm=n_seq)
    q, cmp_new, sel_new, win_new, gates = _qkv_proj(
        x, w['mix_norm'], w['w_qkv'], cos, sin, w['q_gain'], w['k_gain'], w['gate_b'], tm=n_seq, tiles_per_seq=1)
    raw = _norm_matmul(x, w['mix_norm'], w['w_sample'], w['q_gain'], tm=n_seq)
    u = raw[:, :POOL_WIDTH]
    gq, gk = raw[:, POOL_WIDTH:POOL_WIDTH + GLA_QK], raw[:, POOL_WIDTH + GLA_QK:POOL_WIDTH + 2 * GLA_QK]
    c0 = POOL_WIDTH + 2 * GLA_QK
    gv, gr, ga = raw[:, c0:c0 + GLA_WIDTH], raw[:, c0 + GLA_WIDTH:c0 + 2 * GLA_WIDTH], raw[:, c0 + 2 * GLA_WIDTH:]

    q3 = q.reshape(n_seq, NSA_HEADS, HEAD_DIM)
    n_pool = cache_cmp.shape[1]
    cmpc = _cmp_pages(cache_cmp.reshape(-1, n_pool, PAGE_SIZE, kvw), page_table, w['w_cmp'], layer)
    o_c, idx = _cmp_select(q3, cmpc)
    idx = idx[:, :, :SEL_TOPK].transpose(1, 0, 2)
    pidx = jnp.minimum(idx, nbp - 1)
    phys = jnp.take_along_axis(page_table[:, None, :], pidx // bpp, axis=2)
    blk_rows = (phys * bpp + pidx % bpp).reshape(-1).astype(jnp.int32)
    new_slot = jnp.argmax(idx >= nbp, axis=-1).reshape(-1).astype(jnp.int32)
    gates3 = jnp.concatenate([gates[:, :3 * NSA_HPG], gates[:, LANE:LANE + 3 * NSA_HPG]], axis=1)
    w_buf = cache_win.shape[2]
    o_nsa = _sel_win(blk_rows, new_slot, cache_sel.reshape(-1, n_pool * bpp, SEL_BLOCK, kvw),
                     cache_win.reshape(-1, n_seq, w_buf, kvw), q3, sel_new.reshape(n_seq, 1, kvw),
                     win_new.reshape(n_seq, 1, kvw), o_c, gates3.reshape(n_seq, NSA_HEADS, 3), layer)

    u_ctx = jnp.concatenate([cache_pool[layer], u[:, None, :]], axis=1)
    y_pool = _pool_sample(u_ctx.transpose(1, 0, 2), w['pool_w'], w['pool_scale'])
    o_gla, new_state = _gla_sample(
        gq.reshape(n_seq, GLA_QK, 1), gk.reshape(n_seq, GLA_QK, 1), gv.reshape(n_seq, 1, GLA_WIDTH),
        ga.reshape(n_seq, 1, LANE), gr.reshape(n_seq, 1, GLA_WIDTH), state_gla[layer].reshape(n_seq, GLA_QK, GLA_DV),
        w['wa2_t'], w['ba_col'], w['gla_gain'])

    def pad_rows(a):
        a = a.reshape(n_seq, 1, -1)
        return jnp.pad(a, ((0, 0), (0, SAMPLE_PAD - 1), (0, 0))).reshape(n_seq * SAMPLE_PAD, -1)

    mt = cache_mem.shape[2]
    xo = _post_mix(pad_rows(x), pad_rows(o_nsa), pad_rows(y_pool), pad_rows(o_gla), w['w_out'],
                   cache_mem[layer].reshape(n_seq, mt, 2 * MEM_WIDTH), w['mem_q_norm'], w['mem_wq'], w['mem_q_gain'],
                   w['mem_wo'], n_seq=n_seq, t_len=SAMPLE_PAD, tm=SAMPLE_PAD)
    x = xo.reshape(n_seq, SAMPLE_PAD, -1)[:, 0]
    x = _ffn(x, w['ffn2_norm'], w['ffn2_gu'], w['ffn2_down'], tm=n_seq)
    kv_shape = (n_seq, 1, 2, NSA_KV_HEADS, HEAD_DIM)
    win_buf = jnp.concatenate([cache_win[layer], win_new.reshape(kv_shape)], axis=1)[:, 1:]
    outs = (cmp_new.reshape(kv_shape), sel_new.reshape(kv_shape), win_buf, u_ctx[:, 1:],
            new_state.reshape(n_seq, GLA_HEADS, GLA_DK, GLA_DV))
    return x, outs


def kernel(x_prompt, x_sample, cache_cmp_kv, cache_sel_kv, cache_win_kv, cache_pool, state_gla, cache_mem_kv, page_table, mem_prompt, ffn1_norm, ffn1_w_gu, ffn1_w_down, mix_norm, w_in, nsa_q_gain, nsa_k_gain, nsa_gate_bias, nsa_cmp_wk, nsa_cmp_wv, pool_w, pool_scale, gla_wa2, gla_ba, gla_o_gain, w_out, mem_q_norm, mem_kv_norm, mem_wq, mem_wkv, mem_q_gain, mem_k_gain, mem_wo, ffn2_norm, ffn2_w_gu, ffn2_w_down):
    params = dict(ffn1_norm=ffn1_norm, ffn1_w_gu=ffn1_w_gu, ffn1_w_down=ffn1_w_down, mix_norm=mix_norm, w_in=w_in,
                  nsa_q_gain=nsa_q_gain, nsa_k_gain=nsa_k_gain, nsa_gate_bias=nsa_gate_bias, nsa_cmp_wk=nsa_cmp_wk,
                  nsa_cmp_wv=nsa_cmp_wv, pool_w=pool_w, pool_scale=pool_scale, gla_wa2=gla_wa2, gla_ba=gla_ba,
                  gla_o_gain=gla_o_gain, w_out=w_out, mem_q_norm=mem_q_norm, mem_kv_norm=mem_kv_norm, mem_wq=mem_wq,
                  mem_wkv=mem_wkv, mem_q_gain=mem_q_gain, mem_k_gain=mem_k_gain, mem_wo=mem_wo, ffn2_norm=ffn2_norm,
                  ffn2_w_gu=ffn2_w_gu, ffn2_w_down=ffn2_w_down)
    depth = w_in.shape[0]
    n_seq, s_len, d = x_prompt.shape
    n_dec, dec_len, _ = x_sample.shape
    assert dec_len == 1
    past = page_table.shape[1] * PAGE_SIZE
    tables_p = _rope_tables(jnp.arange(s_len))
    tables_s = _rope_tables(jnp.full((n_dec,), past))
    caches = (cache_cmp_kv, cache_sel_kv, cache_win_kv, cache_pool, state_gla, cache_mem_kv)
    xp = x_prompt.reshape(n_seq * s_len, d)
    xs = x_sample.reshape(n_dec, d)
    mem = mem_prompt.reshape(-1, d)
    p_outs, s_outs = [], []
    for l in range(depth):
        w = _layer_weights(params, l)
        xp, po = _prompt_layer(xp, mem, w, tables_p, n_seq=n_seq, s_len=s_len, tm=256)
        xs, so = _sample_layer(xs, w, tables_s, caches, page_table, l)
        p_outs.append(po)
        s_outs.append(so)
    stack = lambda outs, k: jnp.stack([o[k] for o in outs])
    return ((xp.reshape(n_seq, s_len, d), xs.reshape(n_dec, 1, d))
            + tuple(stack(p_outs, k) for k in range(6)) + tuple(stack(s_outs, k) for k in range(5)))
```

```python
import functools

import jax
import jax.numpy as jnp
import numpy as np
from jax import lax
from jax.experimental import pallas as pl
from jax.experimental.pallas import tpu as pltpu

F32 = jnp.float32
BF16 = jnp.bfloat16

D_MODEL = 2048
HEAD_DIM = 128
ROPE_THETA = 10000.0
EPS = 1e-6
NEG = -1e30
FORCE = 1e4
SCALE = HEAD_DIM ** -0.5
PAGE_SIZE = 128
NSA_HEADS = 8
NSA_KV_HEADS = 2
NSA_HPG = NSA_HEADS // NSA_KV_HEADS
NSA_Q_W = NSA_HEADS * HEAD_DIM
NSA_KV_W = NSA_KV_HEADS * HEAD_DIM
CMP_BLOCK = 64
SEL_BLOCK = 64
SEL_TOPK = 16
WINDOW = 512
POOL_WIDTH = 512
POOL_WINDOWS = (2, 4, 8, 16)
POOL_GROUP = POOL_WIDTH // len(POOL_WINDOWS)
POOL_BUF = max(POOL_WINDOWS) - 1
POOL_HIST = 16
GLA_WIDTH = 512
GLA_HEADS = 4
GLA_DK = 64
GLA_DV = 128
GLA_RANK = 16
GLA_TAU = 16.0
GLA_CHUNK = 64
GLA_SUB = 16
GLA_QK = GLA_HEADS * GLA_DK
MEM_HEADS = 4
MEM_WIDTH = MEM_HEADS * HEAD_DIM
LANE = 128
PAD_SCORE = -3.0e38

_C_Q = 0
_C_KV = _C_Q + NSA_Q_W
_C_GT = _C_KV + 6 * NSA_KV_W
_C_U = _C_GT + 3 * NSA_HEADS
_C_GQ = _C_U + POOL_WIDTH
_C_GK = _C_GQ + GLA_QK
_C_GV = _C_GK + GLA_QK
_C_GA = _C_GV + GLA_WIDTH
_C_GR = _C_GA + GLA_RANK
_C_END = _C_GR + GLA_WIDTH

VMEM_LIMIT = 48 * 1024 * 1024


def _cparams(sem):
    return pltpu.CompilerParams(dimension_semantics=sem, vmem_limit_bytes=VMEM_LIMIT)


def _dot(a, b):
    return jnp.dot(a, b, preferred_element_type=F32)


def _dot_nt(a, b):
    return lax.dot_general(a, b, (((1,), (1,)), ((), ())), preferred_element_type=F32)


def _rms(x, g):
    return x * lax.rsqrt(jnp.mean(x * x, axis=-1, keepdims=True) + EPS) * g


def _sigmoid(x):
    return 1.0 / (1.0 + jnp.exp(-x))


def _silu(x):
    return x * _sigmoid(x)


def _log_sigmoid(x):
    return jnp.minimum(x, 0.0) - jnp.log1p(jnp.exp(-jnp.abs(x)))


def _rope(y, cos, sin):
    return y * cos + pltpu.roll(y, HEAD_DIM // 2, axis=1) * sin


def _pad_cols(w, width):
    return jnp.pad(w, ((0, 0), (0, width - w.shape[1])))


def _ffn_body(x_ref, g_ref, wg_ref, wu_ref, wd_ref, o_ref, xn_ref):
    j = pl.program_id(1)

    @pl.when(j == 0)
    def _():
        xn_ref[...] = _rms(x_ref[...], g_ref[...]).astype(BF16)
        o_ref[...] = jnp.zeros_like(o_ref)

    xn = xn_ref[...]
    a = _dot(xn, wg_ref[...])
    b = _dot(xn, wu_ref[...])
    h = (_silu(a) * b).astype(BF16)
    o_ref[...] += _dot(h, wd_ref[...])

    @pl.when(j == pl.num_programs(1) - 1)
    def _():
        o_ref[...] = x_ref[...] + 0.5 * o_ref[...]


def _ffn(x, g, w_gu, w_down, *, tm, tf=512):
    m, d = x.shape
    f = w_down.shape[0]
    nf = f // tf
    return pl.pallas_call(
        _ffn_body,
        out_shape=jax.ShapeDtypeStruct((m, d), F32),
        grid=(m // tm, nf),
        in_specs=[
            pl.BlockSpec((tm, d), lambda i, j: (i, 0)),
            pl.BlockSpec((1, d), lambda i, j: (0, 0)),
            pl.BlockSpec((d, tf), lambda i, j: (0, j)),
            pl.BlockSpec((d, tf), lambda i, j: (0, j + nf)),
            pl.BlockSpec((tf, d), lambda i, j: (j, 0)),
        ],
        out_specs=pl.BlockSpec((tm, d), lambda i, j: (i, 0)),
        scratch_shapes=[pltpu.VMEM((tm, d), BF16)],
        compiler_params=_cparams(("parallel", "arbitrary")),
        name="ffn_half_step",
    )(x, g, w_gu, w_gu, w_down)


def _qkv_body(x_ref, g_ref, w_ref, cos_ref, sin_ref, qg_ref, kg_ref, gb_ref,
              q_ref, cmp_ref, sel_ref, win_ref, gt_ref):
    xn = _rms(x_ref[...], g_ref[...]).astype(BF16)
    h = _dot(xn, w_ref[...])
    cos = cos_ref[...]
    sin = sin_ref[...]

    def head(c0, gain):
        return _rope(_rms(h[:, c0:c0 + HEAD_DIM], gain), cos, sin)

    for hh in range(NSA_HEADS):
        c0 = _C_Q + hh * HEAD_DIM
        q_ref[:, hh * HEAD_DIM:(hh + 1) * HEAD_DIM] = head(c0, qg_ref[...]).astype(BF16)
    for br, out in enumerate((cmp_ref, sel_ref, win_ref)):
        base = _C_KV + br * 2 * NSA_KV_W
        for g in range(NSA_KV_HEADS):
            out[:, g * HEAD_DIM:(g + 1) * HEAD_DIM] = head(base + g * HEAD_DIM, kg_ref[br:br + 1, :])
        out[:, NSA_KV_W:2 * NSA_KV_W] = h[:, base + NSA_KV_W:base + 2 * NSA_KV_W]
    gt_ref[...] = _sigmoid(h[:, _C_GT:_C_GT + 2 * LANE] + gb_ref[...])


def _qkv_proj(x, g, w, cos, sin, q_gain, k_gain, gate_b, *, tm, tiles_per_seq):
    m, d = x.shape
    n = w.shape[1]
    row = lambda i: (i, 0)
    fixed = lambda i: (0, 0)
    pos = lambda i: (i % tiles_per_seq, 0)
    return pl.pallas_call(
        _qkv_body,
        out_shape=(
            jax.ShapeDtypeStruct((m, NSA_Q_W), BF16),
            jax.ShapeDtypeStruct((m, 2 * NSA_KV_W), F32),
            jax.ShapeDtypeStruct((m, 2 * NSA_KV_W), F32),
            jax.ShapeDtypeStruct((m, 2 * NSA_KV_W), F32),
            jax.ShapeDtypeStruct((m, 2 * LANE), F32),
        ),
        grid=(m // tm,),
        in_specs=[
            pl.BlockSpec((tm, d), row),
            pl.BlockSpec((1, d), fixed),
            pl.BlockSpec((d, n), fixed),
            pl.BlockSpec((tm, HEAD_DIM), pos),
            pl.BlockSpec((tm, HEAD_DIM), pos),
            pl.BlockSpec((1, HEAD_DIM), fixed),
            pl.BlockSpec((3, HEAD_DIM), fixed),
            pl.BlockSpec((1, 2 * LANE), fixed),
        ],
        out_specs=(
            pl.BlockSpec((tm, NSA_Q_W), row),
            pl.BlockSpec((tm, 2 * NSA_KV_W), row),
            pl.BlockSpec((tm, 2 * NSA_KV_W), row),
            pl.BlockSpec((tm, 2 * NSA_KV_W), row),
            pl.BlockSpec((tm, 2 * LANE), row),
        ),
        compiler_params=_cparams(("parallel",)),
        name="nsa_projection",
    )(x, g, w, cos, sin, q_gain, k_gain, gate_b)


def _softmax_av(s, v):
    e = jnp.exp(s - jnp.max(s, axis=-1, keepdims=True))
    return _dot(e.astype(BF16), v) / jnp.sum(e, axis=-1, keepdims=True)


EXP2_SCALE = SCALE * float(np.log2(np.e))


def _nsa_prompt_body(q_ref, gt_ref, kc_ref, vc_ref, ks_ref, vs_ref, kw_ref, vw_ref, wk_ref, wv_ref, o_ref,
                     kcb_ref, vcb_ref, ksb_ref, vsb_ref, kwb_ref, vwb_ref, m_ref, l_ref, acc_ref):
    i = pl.program_id(2)
    tq = q_ref.shape[0]
    s_len = kc_ref.shape[0]
    nb = s_len // CMP_BLOCK

    @pl.when(i == 0)
    def _():
        kcb_ref[...] = jnp.sum(kc_ref[...].reshape(nb, CMP_BLOCK, HEAD_DIM) * wk_ref[...][None], axis=1).astype(BF16)
        vcb_ref[...] = jnp.sum(vc_ref[...].reshape(nb, CMP_BLOCK, HEAD_DIM) * wv_ref[...][None], axis=1).astype(BF16)
        ksb_ref[...] = ks_ref[...].astype(BF16)
        vsb_ref[...] = vs_ref[...].astype(BF16)
        kwb_ref[...] = kw_ref[...].astype(BF16)
        vwb_ref[...] = vw_ref[...].astype(BF16)

    qpos = i * tq + lax.broadcasted_iota(jnp.int32, (tq, 1), 0)
    qpos_t = i * tq + lax.broadcasted_iota(jnp.int32, (1, tq), 1)
    kc = kcb_ref[...]
    vc = vcb_ref[...]
    blk = lax.broadcasted_iota(jnp.int32, (1, nb), 1)
    blk_t = lax.broadcasted_iota(jnp.int32, (nb, 1), 0)
    valid_c = ((blk + 1) * CMP_BLOCK - 1) <= qpos
    valid_t = ((blk_t + 1) * CMP_BLOCK - 1) <= qpos_t
    imp_t = jnp.zeros((nb, tq), F32)
    o_c = []
    for hh in range(NSA_HPG):
        qh = q_ref[:, hh * HEAD_DIM:(hh + 1) * HEAD_DIM]
        s = jnp.where(valid_c, _dot_nt(qh, kc) * SCALE, NEG)
        e = jnp.where(valid_c, jnp.exp(s - jnp.max(s, axis=-1, keepdims=True)), 0.0)
        p = e / jnp.maximum(jnp.sum(e, axis=-1, keepdims=True), 1e-30)
        o_c.append(_dot(p.astype(BF16), vc))
        s_t = jnp.where(valid_t, _dot_nt(kc, qh) * SCALE, NEG)
        e_t = jnp.where(valid_t, jnp.exp(s_t - jnp.max(s_t, axis=0, keepdims=True)), 0.0)
        imp_t = imp_t + e_t / jnp.maximum(jnp.sum(e_t, axis=0, keepdims=True), 1e-30)

    cur_t = qpos_t // SEL_BLOCK
    forced = jnp.where(blk_t == cur_t, 1, jnp.where(blk_t == cur_t - 1, 1, jnp.where(blk_t == 0, 1, 0)))
    score = jnp.where(blk_t > cur_t, -FORCE, jnp.where(forced > 0, FORCE, imp_t))
    rank = jnp.zeros((nb, tq), jnp.int32)
    for b2 in range(nb):
        other = score[b2:b2 + 1, :]
        tie = jnp.where(other == score, jnp.where(blk_t > b2, 1, 0), 0)
        rank = rank + jnp.where(other > score, 1, tie)
    keep = jnp.where(rank < min(SEL_TOPK, nb), 1.0, 0.0).T.astype(BF16)

    m_ref[...] = jnp.full(m_ref.shape, NEG, F32)
    l_ref[...] = jnp.zeros(l_ref.shape, F32)
    acc_ref[...] = jnp.zeros(acc_ref.shape, F32)
    blocks_per_tile = tq // SEL_BLOCK

    def key_tile(j, carry):
        k0 = pl.multiple_of(j * tq, tq)
        kt = ksb_ref[pl.ds(k0, tq), :]
        vt = vsb_ref[pl.ds(k0, tq), :]
        expand = jnp.where(
            lax.broadcasted_iota(jnp.int32, (nb, tq), 1) // SEL_BLOCK + j * blocks_per_tile
            == lax.broadcasted_iota(jnp.int32, (nb, tq), 0), 1.0, 0.0).astype(BF16)
        keep_keys = _dot(keep, expand)
        kpos = k0 + lax.broadcasted_iota(jnp.int32, (1, tq), 1)
        bias = jnp.where(keep_keys > 0.5, jnp.where(kpos <= qpos, 0.0, NEG), NEG)
        for hh in range(NSA_HPG):
            s = _dot_nt(q_ref[:, hh * HEAD_DIM:(hh + 1) * HEAD_DIM], kt) + bias
            m_old = m_ref[hh]
            m_new = jnp.maximum(m_old, jnp.max(s, axis=-1, keepdims=True))
            p = jnp.exp2((s - m_new) * EXP2_SCALE)
            alpha = jnp.exp2((m_old - m_new) * EXP2_SCALE)
            l_ref[hh] = l_ref[hh] * alpha + jnp.sum(p, axis=-1, keepdims=True)
            acc_ref[hh] = acc_ref[hh] * alpha + _dot(p.astype(BF16), vt)
            m_ref[hh] = m_new
        return carry

    lax.fori_loop(0, i + 1, key_tile, 0)

    span = WINDOW + tq
    kstart = pl.multiple_of(jnp.maximum(i * tq - WINDOW, 0), tq)
    kw = kwb_ref[pl.ds(kstart, span), :]
    vw = vwb_ref[pl.ds(kstart, span), :]
    rel = qpos - (kstart + lax.broadcasted_iota(jnp.int32, (1, span), 1))
    bias_w = jnp.where(rel >= 0, jnp.where(rel <= WINDOW, 0.0, NEG), NEG)

    gt = gt_ref[...]
    for hh in range(NSA_HPG):
        s = _dot_nt(q_ref[:, hh * HEAD_DIM:(hh + 1) * HEAD_DIM], kw) + bias_w
        e = jnp.exp2((s - jnp.max(s, axis=-1, keepdims=True)) * EXP2_SCALE)
        o_w = _dot(e.astype(BF16), vw) / jnp.sum(e, axis=-1, keepdims=True)
        o_s = acc_ref[hh] / l_ref[hh]
        o = (gt[:, 3 * hh:3 * hh + 1] * o_c[hh] + gt[:, 3 * hh + 1:3 * hh + 2] * o_s
             + gt[:, 3 * hh + 2:3 * hh + 3] * o_w)
        o_ref[:, hh * HEAD_DIM:(hh + 1) * HEAD_DIM] = o.astype(BF16)


def _nsa_prompt(q, gates, cmp_kv, sel_kv, win_kv, wk, wv, *, n_seq, s_len, tq):
    m = q.shape[0]
    nt = s_len // tq
    nb = s_len // CMP_BLOCK
    assert s_len >= WINDOW + tq and s_len % tq == 0 and tq % SEL_BLOCK == 0
    gw = NSA_HPG * HEAD_DIM
    row = lambda n, g, i: (n * nt + i, g)
    kspec = pl.BlockSpec((s_len, HEAD_DIM), lambda n, g, i: (n, g))
    vspec = pl.BlockSpec((s_len, HEAD_DIM), lambda n, g, i: (n, NSA_KV_HEADS + g))
    wspec = pl.BlockSpec((None, CMP_BLOCK, HEAD_DIM), lambda n, g, i: (g, 0, 0))
    return pl.pallas_call(
        _nsa_prompt_body,
        out_shape=jax.ShapeDtypeStruct((m, NSA_Q_W), BF16),
        grid=(n_seq, NSA_KV_HEADS, nt),
        in_specs=[
            pl.BlockSpec((tq, gw), row),
            pl.BlockSpec((tq, LANE), row),
            kspec, vspec, kspec, vspec, kspec, vspec, wspec, wspec,
        ],
        out_specs=pl.BlockSpec((tq, gw), row),
        scratch_shapes=[pltpu.VMEM((nb, HEAD_DIM), BF16), pltpu.VMEM((nb, HEAD_DIM), BF16)]
        + [pltpu.VMEM((s_len, HEAD_DIM), BF16)] * 4
        + [pltpu.VMEM((NSA_HPG, tq, 1), F32), pltpu.VMEM((NSA_HPG, tq, 1), F32),
           pltpu.VMEM((NSA_HPG, tq, HEAD_DIM), F32)],
        compiler_params=_cparams(("parallel", "parallel", "arbitrary")),
        name="nsa_prompt",
    )(q, gates, cmp_kv, cmp_kv, sel_kv, sel_kv, win_kv, win_kv, wk, wv)


def _pool_prompt_body(x_ref, g_ref, w_ref, pw_ref, ps_ref, u_ref, y_ref, hist_ref):
    j = pl.program_id(1)
    tm = x_ref.shape[0]

    @pl.when(j == 0)
    def _():
        hist_ref[...] = jnp.zeros_like(hist_ref)

    u = _dot(_rms(x_ref[...], g_ref[...]).astype(BF16), w_ref[...])
    u_ref[...] = u
    ext = jnp.concatenate([hist_ref[...], u], axis=0)
    t = j * tm + lax.broadcasted_iota(jnp.int32, (tm, 1), 0)
    for gi, win in enumerate(POOL_WINDOWS):
        c0, c1 = gi * POOL_GROUP, (gi + 1) * POOL_GROUP
        acc = ext[:, c0:c1]
        span = 1
        while span < win:
            acc = acc + pltpu.roll(acc, span, axis=0)
            span *= 2
        mean = acc[POOL_HIST:] / jnp.minimum(t + 1, win).astype(F32)
        y = _dot((mean - u[:, c0:c1]).astype(BF16), pw_ref[gi].astype(BF16))
        y_ref[:, c0:c1] = (y * ps_ref[:, c0:c1]).astype(BF16)
    hist_ref[...] = u[tm - POOL_HIST:]


def _pool_prompt(x, g, w_u, pool_w, pool_scale, *, n_seq, s_len, tm):
    d = x.shape[1]
    nt = s_len // tm
    row = lambda n, j: (n * nt + j, 0)
    fixed = lambda n, j: (0, 0)
    return pl.pallas_call(
        _pool_prompt_body,
        out_shape=(jax.ShapeDtypeStruct((n_seq * s_len, POOL_WIDTH), F32),
                   jax.ShapeDtypeStruct((n_seq * s_len, POOL_WIDTH), BF16)),
        grid=(n_seq, nt),
        in_specs=[
            pl.BlockSpec((tm, d), row),
            pl.BlockSpec((1, d), fixed),
            pl.BlockSpec((d, POOL_WIDTH), fixed),
            pl.BlockSpec((len(POOL_WINDOWS), POOL_GROUP, POOL_GROUP), lambda n, j: (0, 0, 0)),
            pl.BlockSpec((1, POOL_WIDTH), fixed),
        ],
        out_specs=(pl.BlockSpec((tm, POOL_WIDTH), row), pl.BlockSpec((tm, POOL_WIDTH), row)),
        scratch_shapes=[pltpu.VMEM((POOL_HIST, POOL_WIDTH), F32)],
        compiler_params=_cparams(("parallel", "arbitrary")),
        name="pool_prompt",
    )(x, g, w_u, pool_w, pool_scale)


def _gla_prompt_body(x_ref, g_ref, w_ref, wa2_ref, ba_ref, gain_ref, og_ref, st_ref, state_ref):
    j = pl.program_id(1)
    tm = x_ref.shape[0]

    @pl.when(j == 0)
    def _():
        state_ref[...] = jnp.zeros_like(state_ref)

    h = _dot(_rms(x_ref[...], g_ref[...]).astype(BF16), w_ref[...])
    c_k, c_v, c_r, c_a = GLA_QK, 2 * GLA_QK, 2 * GLA_QK + GLA_WIDTH, 2 * GLA_QK + 2 * GLA_WIDTH
    z = _dot(h[:, c_a:c_a + LANE].astype(BF16), wa2_ref[...]) + ba_ref[...]
    la_all = _log_sigmoid(z) / GLA_TAU

    lane = lax.broadcasted_iota(jnp.int32, (1, GLA_QK), 1)
    head_mask = [jnp.where(lane // GLA_DK == hd, 1.0, 0.0) for hd in range(GLA_HEADS)]
    rows = lax.broadcasted_iota(jnp.int32, (GLA_CHUNK, 1), 0)
    causal = rows >= lax.broadcasted_iota(jnp.int32, (1, GLA_CHUNK), 1)
    n_sub = GLA_CHUNK // GLA_SUB

    for c in range(tm // GLA_CHUNK):
        r0, r1 = c * GLA_CHUNK, (c + 1) * GLA_CHUNK
        q = h[r0:r1, 0:GLA_QK] * (GLA_DK ** -0.5)
        k = h[r0:r1, c_k:c_k + GLA_QK]
        b = la_all[r0:r1]
        sh = 1
        while sh < GLA_CHUNK:
            b = b + jnp.where(rows >= sh, pltpu.roll(b, sh, axis=0), 0.0)
            sh *= 2
        state = state_ref[...]
        state_b = state.astype(BF16)
        q_in = q * jnp.exp(b)
        a_parts = [[] for _ in range(GLA_HEADS)]
        for sb in range(n_sub):
            s0, s1 = sb * GLA_SUB, (sb + 1) * GLA_SUB
            ref_row = b[s0 - 1:s0] if sb > 0 else jnp.zeros((1, GLA_QK), F32)
            q_t = q[s0:s1] * jnp.exp(b[s0:s1] - ref_row)
            k_t = (k * jnp.exp(jnp.where(rows < s1, ref_row - b, 0.0))).astype(BF16)
            for hd in range(GLA_HEADS):
                a_parts[hd].append(_dot_nt((q_t * head_mask[hd]).astype(BF16), k_t))
        b_last = b[GLA_CHUNK - 1:GLA_CHUNK]
        k_out = k * jnp.exp(b_last - b)
        new_state = state * jnp.exp(b_last)
        for hd in range(GLA_HEADS):
            v = h[r0:r1, c_v + hd * GLA_DV:c_v + (hd + 1) * GLA_DV]
            vb = v.astype(BF16)
            a = jnp.where(causal, jnp.concatenate(a_parts[hd], axis=0), 0.0)
            o = _dot(a.astype(BF16), vb) + _dot_nt((q_in * head_mask[hd]).astype(BF16), state_b)
            gr = h[r0:r1, c_r + hd * GLA_DV:c_r + (hd + 1) * GLA_DV]
            og = _rms(o, gain_ref[:, hd * GLA_DV:(hd + 1) * GLA_DV]) * _silu(gr)
            og_ref[r0:r1, hd * GLA_DV:(hd + 1) * GLA_DV] = og.astype(BF16)
            new_state = new_state + _dot(v.T.astype(BF16), (k_out * head_mask[hd]).astype(BF16))
        state_ref[...] = new_state

    @pl.when(j == pl.num_programs(1) - 1)
    def _():
        st_ref[...] = state_ref[...]


def _gla_prompt(x, g, w_gla, wa2, ba, gain, *, n_seq, s_len, tm):
    d = x.shape[1]
    nt = s_len // tm
    row = lambda n, j: (n * nt + j, 0)
    fixed = lambda n, j: (0, 0)
    return pl.pallas_call(
        _gla_prompt_body,
        out_shape=(jax.ShapeDtypeStruct((n_seq * s_len, GLA_WIDTH), BF16),
                   jax.ShapeDtypeStruct((n_seq, GLA_DV, GLA_QK), F32)),
        grid=(n_seq, nt),
        in_specs=[
            pl.BlockSpec((tm, d), row),
            pl.BlockSpec((1, d), fixed),
            pl.BlockSpec((d, w_gla.shape[1]), fixed),
            pl.BlockSpec((LANE, GLA_QK), fixed),
            pl.BlockSpec((1, GLA_QK), fixed),
            pl.BlockSpec((1, GLA_WIDTH), fixed),
        ],
        out_specs=(pl.BlockSpec((tm, GLA_WIDTH), row),
                   pl.BlockSpec((None, GLA_DV, GLA_QK), lambda n, j: (n, 0, 0))),
        scratch_shapes=[pltpu.VMEM((GLA_DV, GLA_QK), F32)],
        compiler_params=_cparams(("parallel", "arbitrary")),
        name="gla_prompt",
    )(x, g, w_gla, wa2, ba, gain)


def _norm_matmul_body(x_ref, g_ref, w_ref, hg_ref, o_ref, *, n_norm_heads):
    h = _dot(_rms(x_ref[...], g_ref[...]).astype(BF16), w_ref[...])
    if n_norm_heads:
        for hd in range(n_norm_heads):
            c0 = hd * HEAD_DIM
            o_ref[:, c0:c0 + HEAD_DIM] = _rms(h[:, c0:c0 + HEAD_DIM], hg_ref[...])
        c0 = n_norm_heads * HEAD_DIM
        o_ref[:, c0:] = h[:, c0:]
    else:
        o_ref[...] = h


def _norm_matmul(x, g, w, head_gain, *, tm, n_norm_heads=0):
    m, d = x.shape
    n = w.shape[1]
    return pl.pallas_call(
        functools.partial(_norm_matmul_body, n_norm_heads=n_norm_heads),
        out_shape=jax.ShapeDtypeStruct((m, n), F32),
        grid=(m // tm,),
        in_specs=[
            pl.BlockSpec((tm, d), lambda i: (i, 0)),
            pl.BlockSpec((1, d), lambda i: (0, 0)),
            pl.BlockSpec((d, n), lambda i: (0, 0)),
            pl.BlockSpec((1, HEAD_DIM), lambda i: (0, 0)),
        ],
        out_specs=pl.BlockSpec((tm, n), lambda i: (i, 0)),
        compiler_params=_cparams(("parallel",)),
        name="norm_matmul",
    )(x, g, w, head_gain)


def _post_body(x_ref, on_ref, yp_ref, og_ref, wout_ref, mk_ref, mv_ref, qn_ref, wq_ref, qg_ref, wo_ref, o_ref):
    c1, c2 = NSA_Q_W, NSA_Q_W + POOL_WIDTH
    x1 = (x_ref[...] + _dot(on_ref[...], wout_ref[0:c1, :]) + _dot(yp_ref[...], wout_ref[c1:c2, :])
          + _dot(og_ref[...], wout_ref[c2:, :]))
    q = _dot(_rms(x1, qn_ref[...]).astype(BF16), wq_ref[...])
    outs = []
    for hd in range(MEM_HEADS):
        c0 = hd * HEAD_DIM
        qh = _rms(q[:, c0:c0 + HEAD_DIM], qg_ref[...]).astype(BF16)
        s = _dot_nt(qh, mk_ref[:, c0:c0 + HEAD_DIM].astype(BF16)) * SCALE
        outs.append(_softmax_av(s, mv_ref[:, c0:c0 + HEAD_DIM].astype(BF16)).astype(BF16))
    o_ref[...] = x1 + _dot(jnp.concatenate(outs, axis=1), wo_ref[...])


def _post_mix(x, o_nsa, y_pool, o_gla, w_out, mem_kv, q_norm, wq, q_gain, wo, *, n_seq, t_len, tm):
    d = x.shape[1]
    nt = t_len // tm
    mt = mem_kv.shape[1]
    row = lambda n, j: (n * nt + j, 0)
    fixed = lambda n, j: (0, 0)
    return pl.pallas_call(
        _post_body,
        out_shape=jax.ShapeDtypeStruct(x.shape, F32),
        grid=(n_seq, nt),
        in_specs=[
            pl.BlockSpec((tm, d), row),
            pl.BlockSpec((tm, NSA_Q_W), row),
            pl.BlockSpec((tm, POOL_WIDTH), row),
            pl.BlockSpec((tm, GLA_WIDTH), row),
            pl.BlockSpec(w_out.shape, fixed),
            pl.BlockSpec((None, mt, MEM_WIDTH), lambda n, j: (n, 0, 0)),
            pl.BlockSpec((None, mt, MEM_WIDTH), lambda n, j: (n, 0, 1)),
            pl.BlockSpec((1, d), fixed),
            pl.BlockSpec((d, MEM_WIDTH), fixed),
            pl.BlockSpec((1, HEAD_DIM), fixed),
            pl.BlockSpec((MEM_WIDTH, d), fixed),
        ],
        out_specs=pl.BlockSpec((tm, d), row),
        compiler_params=_cparams(("parallel", "parallel")),
        name="mix_out_mem_attn",
    )(x, o_nsa, y_pool, o_gla, w_out, mem_kv, mem_kv, q_norm, wq, q_gain, wo)


PAGES_PER_STEP = 8


KV_ROWS = 2 * NSA_KV_HEADS
SUBLANES = 8


def _cmp_pages_body(pt_ref, *refs):
    del pt_ref
    pages, w_ref, o_ref = refs[:PAGES_PER_STEP], refs[PAGES_PER_STEP], refs[PAGES_PER_STEP + 1]
    w = w_ref[...]
    blk_rows = CMP_BLOCK * KV_ROWS
    sums = []
    for page in pages:
        for half in range(PAGE_SIZE // CMP_BLOCK):
            prod = page[half * blk_rows:(half + 1) * blk_rows, :] * w
            part = jnp.sum(prod.reshape(blk_rows // SUBLANES, SUBLANES, HEAD_DIM), axis=0)
            sums.append(part[0:KV_ROWS] + part[KV_ROWS:2 * KV_ROWS])
    for r in range(KV_ROWS):
        o_ref[r] = jnp.concatenate([s[r:r + 1] for s in sums], axis=0)


def _cmp_pages(cache, page_table, w_cmp, layer):
    n_seq, n_pages = page_table.shape
    page_rows = cache.shape[2]
    bpp = PAGE_SIZE // CMP_BLOCK
    assert n_pages % PAGES_PER_STEP == 0 and 2 * KV_ROWS == SUBLANES

    def page_spec(k):
        return pl.BlockSpec((None, None, page_rows, HEAD_DIM),
                            lambda n, j, pt: (layer, pt[n, j * PAGES_PER_STEP + k], 0, 0))

    return pl.pallas_call(
        _cmp_pages_body,
        out_shape=jax.ShapeDtypeStruct((n_seq, KV_ROWS, n_pages * bpp, HEAD_DIM), F32),
        grid_spec=pltpu.PrefetchScalarGridSpec(
            num_scalar_prefetch=1,
            grid=(n_seq, n_pages // PAGES_PER_STEP),
            in_specs=[page_spec(k) for k in range(PAGES_PER_STEP)]
            + [pl.BlockSpec((CMP_BLOCK * KV_ROWS, HEAD_DIM), lambda n, j, pt: (0, 0))],
            out_specs=pl.BlockSpec((None, KV_ROWS, PAGES_PER_STEP * bpp, HEAD_DIM), lambda n, j, pt: (n, 0, j, 0)),
        ),
        compiler_params=_cparams(("parallel", "arbitrary")),
        name="cmp_pages",
    )(page_table, *([cache] * PAGES_PER_STEP), w_cmp)


def _cmp_select_body(q_ref, c_ref, oc_ref, idx_ref, *, n_past_blocks):
    n_seq = q_ref.shape[0]
    nbp = n_past_blocks
    width = idx_ref.shape[-1]
    cur = nbp
    q = q_ref[...]
    lane = lax.broadcasted_iota(jnp.int32, (n_seq, width), 1)
    lane_f = lane.astype(F32)
    row_head = lax.broadcasted_iota(jnp.int32, (1, NSA_HEADS, 1), 1)
    o_c = jnp.zeros((n_seq, NSA_HEADS, HEAD_DIM), F32)
    for g in range(NSA_KV_HEADS):
        kc = c_ref[:, g].astype(BF16)
        vc = c_ref[:, NSA_KV_HEADS + g].astype(BF16)
        s = lax.dot_general(q, kc, (((2,), (2,)), ((0,), (0,))), preferred_element_type=F32) * SCALE
        e = jnp.exp(s - jnp.max(s, axis=-1, keepdims=True))
        p = e / jnp.maximum(jnp.sum(e, axis=-1, keepdims=True), 1e-30)
        o_g = lax.dot_general(p.astype(BF16), vc, (((2,), (1,)), ((0,), (0,))), preferred_element_type=F32)
        in_group = row_head // NSA_HPG == g
        o_c = jnp.where(in_group, o_g, o_c)
        imp = jnp.sum(jnp.where(in_group, p, 0.0), axis=1)
        imp = jnp.concatenate([imp, jnp.zeros((n_seq, width - nbp), F32)], axis=1)
        forced = jnp.where(lane == cur, 1, jnp.where(lane == cur - 1, 1, jnp.where(lane == 0, 1, 0)))
        score = jnp.where(lane > cur, PAD_SCORE, jnp.where(forced > 0, FORCE, imp))
        picks = jnp.zeros((n_seq, width), F32)
        for it in range(SEL_TOPK):
            top = jnp.max(score, axis=-1, keepdims=True)
            first = jnp.min(jnp.where(score == top, lane_f, float(width)), axis=-1, keepdims=True)
            picks = jnp.where(lane == it, first, picks)
            score = jnp.where(lane_f == first, PAD_SCORE, score)
        idx_ref[g] = picks.astype(jnp.int32)
    oc_ref[...] = o_c


def _cmp_select(q3, cmpc):
    n_seq, _, nbp, _ = cmpc.shape
    assert nbp + 1 >= SEL_TOPK and nbp % LANE == 0
    idx_w = nbp + LANE
    return pl.pallas_call(
        functools.partial(_cmp_select_body, n_past_blocks=nbp),
        out_shape=(jax.ShapeDtypeStruct((n_seq, NSA_HEADS, HEAD_DIM), F32),
                   jax.ShapeDtypeStruct((NSA_KV_HEADS, n_seq, idx_w), jnp.int32)),
        grid=(1,),
        in_specs=[
            pl.BlockSpec((n_seq, NSA_HEADS, HEAD_DIM), lambda i: (0, 0, 0)),
            pl.BlockSpec(cmpc.shape, lambda i: (0, 0, 0, 0)),
        ],
        out_specs=(pl.BlockSpec((n_seq, NSA_HEADS, HEAD_DIM), lambda i: (0, 0, 0)),
                   pl.BlockSpec((NSA_KV_HEADS, n_seq, idx_w), lambda i: (0, 0, 0))),
        compiler_params=_cparams(("arbitrary",)),
        name="cmp_select",
    )(q3, cmpc)


def _one_token_attn(q, rows, g, k_new, v_new, extra_bias):
    n_rows = rows.shape[0]
    kind = lax.broadcasted_iota(jnp.int32, (1, n_rows), 1) % KV_ROWS
    s = _dot_nt(q, rows) * SCALE + jnp.where(kind == g, 0.0, NEG)
    if extra_bias is not None:
        s = s + extra_bias
    s_new = jnp.sum(q.astype(F32) * k_new, axis=-1, keepdims=True) * SCALE
    top = jnp.maximum(jnp.max(s, axis=-1, keepdims=True), s_new)
    e = jnp.exp(s - top)
    e_new = jnp.exp(s_new - top)
    e_val = pltpu.roll(e, NSA_KV_HEADS, axis=1)
    return (_dot(e_val.astype(BF16), rows) + e_new * v_new) / (jnp.sum(e, axis=-1, keepdims=True) + e_new)


def _sel_win_body(blk_ref, new_ref, *refs):
    del blk_ref
    n = pl.program_id(0)
    nk = SEL_TOPK
    n_sel = NSA_KV_HEADS * nk
    sel_refs = refs[:n_sel]
    win_ref, q_ref, sn_ref, wn_ref, oc_ref, gt_ref, o_ref = refs[n_sel:]
    q = q_ref[...]
    sel_new = sn_ref[...]
    win_new = wn_ref[...]
    gt = gt_ref[...]
    row_head = lax.broadcasted_iota(jnp.int32, (NSA_HEADS, 1), 0)
    blk_rows = SEL_BLOCK * KV_ROWS
    key_blk = lax.broadcasted_iota(jnp.int32, (1, nk * blk_rows), 1) // blk_rows
    win_rows = win_ref[...].astype(BF16)
    o_s = jnp.zeros((NSA_HEADS, HEAD_DIM), F32)
    o_w = jnp.zeros((NSA_HEADS, HEAD_DIM), F32)
    for g in range(NSA_KV_HEADS):
        k0, v0 = g * HEAD_DIM, NSA_KV_W + g * HEAD_DIM
        rows = jnp.concatenate([r[...] for r in sel_refs[g * nk:(g + 1) * nk]], axis=0).astype(BF16)
        bias = jnp.where(key_blk == new_ref[n * NSA_KV_HEADS + g], NEG, 0.0)
        o_sg = _one_token_attn(q, rows, g, sel_new[:, k0:k0 + HEAD_DIM], sel_new[:, v0:v0 + HEAD_DIM], bias)
        o_wg = _one_token_attn(q, win_rows, g, win_new[:, k0:k0 + HEAD_DIM], win_new[:, v0:v0 + HEAD_DIM], None)
        in_group = row_head // NSA_HPG == g
        o_s = jnp.where(in_group, o_sg, o_s)
        o_w = jnp.where(in_group, o_wg, o_w)
    o_ref[...] = (gt[:, 0:1] * oc_ref[...] + gt[:, 1:2] * o_s + gt[:, 2:3] * o_w).astype(BF16)


def _sel_win(blk_rows, new_slot, cache_sel, cache_win, q3, sel_new, win_new, o_c, gates3, layer):
    n_seq = q3.shape[0]
    nk = SEL_TOPK

    def sel_spec(g, k):
        return pl.BlockSpec((None, None, SEL_BLOCK * KV_ROWS, HEAD_DIM),
                            lambda n, br, ns: (layer, br[(n * NSA_KV_HEADS + g) * nk + k], 0, 0))

    sel_specs = [sel_spec(g, k) for g in range(NSA_KV_HEADS) for k in range(nk)]
    win_specs = [pl.BlockSpec((None, None, cache_win.shape[2], HEAD_DIM), lambda n, br, ns: (layer, n, 0, 0))]
    tok = lambda n, br, ns: (n, 0, 0)
    return pl.pallas_call(
        _sel_win_body,
        out_shape=jax.ShapeDtypeStruct((n_seq, NSA_HEADS, HEAD_DIM), BF16),
        grid_spec=pltpu.PrefetchScalarGridSpec(
            num_scalar_prefetch=2,
            grid=(n_seq,),
            in_specs=sel_specs + win_specs + [
                pl.BlockSpec((None, NSA_HEADS, HEAD_DIM), tok),
                pl.BlockSpec((None, 1, 2 * NSA_KV_W), tok),
                pl.BlockSpec((None, 1, 2 * NSA_KV_W), tok),
                pl.BlockSpec((None, NSA_HEADS, HEAD_DIM), tok),
                pl.BlockSpec((None, NSA_HEADS, 3), tok),
            ],
            out_specs=pl.BlockSpec((None, NSA_HEADS, HEAD_DIM), tok),
        ),
        compiler_params=_cparams(("parallel",)),
        name="sel_win_sample",
    )(blk_rows, new_slot, *([cache_sel] * len(sel_specs)), *([cache_win] * len(win_specs)),
      q3, sel_new, win_new, o_c, gates3)


def _pool_sample_body(uc_ref, pw_ref, ps_ref, y_ref):
    rows = uc_ref.shape[0]
    for gi, win in enumerate(POOL_WINDOWS):
        c0, c1 = gi * POOL_GROUP, (gi + 1) * POOL_GROUP
        acc = uc_ref[rows - 1, :, c0:c1]
        for r in range(rows - 2, rows - 1 - win, -1):
            acc = acc + uc_ref[r, :, c0:c1]
        d = acc / float(win) - uc_ref[rows - 1, :, c0:c1]
        y_ref[:, c0:c1] = (_dot(d.astype(BF16), pw_ref[gi].astype(BF16)) * ps_ref[:, c0:c1]).astype(BF16)


def _pool_sample(u_ctx_t, pool_w, pool_scale):
    rows, n_seq, width = u_ctx_t.shape
    assert rows >= max(POOL_WINDOWS)
    return pl.pallas_call(
        _pool_sample_body,
        out_shape=jax.ShapeDtypeStruct((n_seq, width), BF16),
        name="pool_sample",
    )(u_ctx_t, pool_w, pool_scale)


def _gla_sample_body(q_ref, k_ref, v_ref, ga_ref, gr_ref, s_ref, wa2t_ref, ba_ref, gain_ref, og_ref, so_ref):
    z = jnp.sum(wa2t_ref[...] * ga_ref[...], axis=-1, keepdims=True) + ba_ref[...]
    decay = jnp.exp(_log_sigmoid(z) / GLA_TAU)
    v = v_ref[...]
    v_rows = jnp.concatenate(
        [jnp.broadcast_to(v[:, hd * GLA_DV:(hd + 1) * GLA_DV], (GLA_DK, GLA_DV)) for hd in range(GLA_HEADS)], axis=0)
    state = decay * s_ref[...] + k_ref[...] * v_rows
    so_ref[...] = state
    qs = (q_ref[...] * (GLA_DK ** -0.5)) * state
    gr = gr_ref[...]
    for hd in range(GLA_HEADS):
        c0, c1 = hd * GLA_DV, (hd + 1) * GLA_DV
        o = jnp.sum(qs[hd * GLA_DK:(hd + 1) * GLA_DK], axis=0, keepdims=True)
        og_ref[:, c0:c1] = (_rms(o, gain_ref[:, c0:c1]) * _silu(gr[:, c0:c1])).astype(BF16)


def _gla_sample(q_col, k_col, v, ga, gr, state, wa2_t, ba_col, gain):
    n_seq = v.shape[0]
    tok = lambda n: (n, 0, 0)
    fixed = lambda n: (0, 0)
    return pl.pallas_call(
        _gla_sample_body,
        out_shape=(jax.ShapeDtypeStruct((n_seq, 1, GLA_WIDTH), BF16),
                   jax.ShapeDtypeStruct((n_seq, GLA_QK, GLA_DV), F32)),
        grid=(n_seq,),
        in_specs=[
            pl.BlockSpec((None, GLA_QK, 1), tok),
            pl.BlockSpec((None, GLA_QK, 1), tok),
            pl.BlockSpec((None, 1, GLA_WIDTH), tok),
            pl.BlockSpec((None, 1, LANE), tok),
            pl.BlockSpec((None, 1, GLA_WIDTH), tok),
            pl.BlockSpec((None, GLA_QK, GLA_DV), tok),
            pl.BlockSpec((GLA_QK, LANE), fixed),
            pl.BlockSpec((GLA_QK, 1), fixed),
            pl.BlockSpec((1, GLA_WIDTH), fixed),
        ],
        out_specs=(pl.BlockSpec((None, 1, GLA_WIDTH), tok), pl.BlockSpec((None, GLA_QK, GLA_DV), tok)),
        compiler_params=_cparams(("parallel",)),
        name="gla_sample",
    )(q_col, k_col, v, ga, gr, state, wa2_t, ba_col, gain)


def _rope_tables(pos):
    half = HEAD_DIM // 2
    inv = ROPE_THETA ** (-jnp.arange(half, dtype=F32) / half)
    ang = pos.astype(F32)[:, None] * inv[None, :]
    cos, sin = jnp.cos(ang), jnp.sin(ang)
    return jnp.concatenate([cos, cos], axis=1), jnp.concatenate([-sin, sin], axis=1)


def _layer_weights(p, l):
    w_in = p['w_in'][l]
    n_gate = 3 * NSA_HPG
    gate_cols = [_pad_cols(w_in[:, _C_GT + g * n_gate:_C_GT + (g + 1) * n_gate], LANE) for g in range(NSA_KV_HEADS)]
    gate_b = p['nsa_gate_bias'][l]
    ga_cols = _pad_cols(w_in[:, _C_GA:_C_GR], LANE)
    gla_cols = [w_in[:, _C_GQ:_C_GA], w_in[:, _C_GR:_C_END], ga_cols]
    row = lambda a: a.reshape(1, -1)
    return dict(
        ffn1_norm=row(p['ffn1_norm'][l]), ffn1_gu=p['ffn1_w_gu'][l].astype(BF16), ffn1_down=p['ffn1_w_down'][l].astype(BF16),
        ffn2_norm=row(p['ffn2_norm'][l]), ffn2_gu=p['ffn2_w_gu'][l].astype(BF16), ffn2_down=p['ffn2_w_down'][l].astype(BF16),
        mix_norm=row(p['mix_norm'][l]),
        w_qkv=jnp.concatenate([w_in[:, _C_Q:_C_GT]] + gate_cols, axis=1).astype(BF16),
        gate_b=jnp.concatenate([_pad_cols(row(gate_b[g * n_gate:(g + 1) * n_gate]), LANE) for g in range(NSA_KV_HEADS)], axis=1),
        q_gain=row(p['nsa_q_gain'][l]), k_gain=p['nsa_k_gain'][l],
        wk=p['nsa_cmp_wk'][l], wv=p['nsa_cmp_wv'][l],
        w_cmp=jnp.stack([p['nsa_cmp_wk'][l], p['nsa_cmp_wv'][l]]).transpose(2, 0, 1, 3).reshape(-1, HEAD_DIM),
        w_u=w_in[:, _C_U:_C_GQ].astype(BF16),
        w_gla=jnp.concatenate(gla_cols, axis=1).astype(BF16),
        w_sample=jnp.concatenate([w_in[:, _C_U:_C_GQ]] + gla_cols, axis=1).astype(BF16),
        pool_w=p['pool_w'][l], pool_scale=row(p['pool_scale'][l]),
        wa2=jnp.pad(p['gla_wa2'][l], ((0, LANE - GLA_RANK), (0, 0))).astype(BF16),
        wa2_t=_pad_cols(p['gla_wa2'][l].T, LANE),
        ba=row(p['gla_ba'][l]), ba_col=p['gla_ba'][l].reshape(-1, 1),
        gla_gain=row(p['gla_o_gain'][l]),
        w_out=p['w_out'][l].astype(BF16),
        mem_q_norm=row(p['mem_q_norm'][l]), mem_kv_norm=row(p['mem_kv_norm'][l]),
        mem_wq=p['mem_wq'][l].astype(BF16), mem_wkv=p['mem_wkv'][l].astype(BF16),
        mem_q_gain=row(p['mem_q_gain'][l]), mem_k_gain=row(p['mem_k_gain'][l]),
        mem_wo=p['mem_wo'][l].astype(BF16),
    )


def _prompt_layer(x, mem, w, tables, *, n_seq, s_len, tm):
    cos, sin = tables
    x = _ffn(x, w['ffn1_norm'], w['ffn1_gu'], w['ffn1_down'], tm=min(512, n_seq * s_len))
    q, cmp_kv, sel_kv, win_kv, gates = _qkv_proj(
        x, w['mix_norm'], w['w_qkv'], cos, sin, w['q_gain'], w['k_gain'], w['gate_b'], tm=tm, tiles_per_seq=s_len // tm)
    o_nsa = _nsa_prompt(q, gates, cmp_kv, sel_kv, win_kv, w['wk'], w['wv'], n_seq=n_seq, s_len=s_len, tq=tm)
    u, y_pool = _pool_prompt(x, w['mix_norm'], w['w_u'], w['pool_w'], w['pool_scale'], n_seq=n_seq, s_len=s_len, tm=tm)
    o_gla, state_t = _gla_prompt(x, w['mix_norm'], w['w_gla'], w['wa2'], w['ba'], w['gla_gain'],
                                 n_seq=n_seq, s_len=s_len, tm=tm)
    mkv = _norm_matmul(mem, w['mem_kv_norm'], w['mem_wkv'], w['mem_k_gain'], tm=tm, n_norm_heads=MEM_HEADS)
    mt = mem.shape[0] // n_seq
    x = _post_mix(x, o_nsa, y_pool, o_gla, w['w_out'], mkv.reshape(n_seq, mt, 2 * MEM_WIDTH), w['mem_q_norm'],
                  w['mem_wq'], w['mem_q_gain'], w['mem_wo'], n_seq=n_seq, t_len=s_len, tm=tm)
    x = _ffn(x, w['ffn2_norm'], w['ffn2_gu'], w['ffn2_down'], tm=min(512, n_seq * s_len))
    w_keep = min(WINDOW, s_len)
    kv_shape = (n_seq, s_len, 2, NSA_KV_HEADS, HEAD_DIM)
    state = state_t.reshape(n_seq, GLA_DV, GLA_HEADS, GLA_DK).transpose(0, 2, 3, 1)
    outs = (cmp_kv.reshape(kv_shape), sel_kv.reshape(kv_shape), win_kv.reshape(kv_shape)[:, s_len - w_keep:],
            u.reshape(n_seq, s_len, POOL_WIDTH)[:, s_len - POOL_BUF:], state,
            mkv.reshape(n_seq, mt, 2, MEM_HEADS, HEAD_DIM))
    return x, outs


SAMPLE_PAD = 8


def _sample_layer(x, w, tables, caches, page_table, layer):
    cache_cmp, cache_sel, cache_win, cache_pool, state_gla, cache_mem = caches
    cos, sin = tables
    n_seq = x.shape[0]
    n_pages = page_table.shape[1]
    bpp = PAGE_SIZE // SEL_BLOCK
    nbp = n_pages * bpp
    kvw = 2 * NSA_KV_W
    x = _ffn(x, w['ffn1_norm'], w['ffn1_gu'], w['ffn1_down'], tm=n_seq)
    q, cmp_new, sel_new, win_new, gates = _qkv_proj(
        x, w['mix_norm'], w['w_qkv'], cos, sin, w['q_gain'], w['k_gain'], w['gate_b'], tm=n_seq, tiles_per_seq=1)
    raw = _norm_matmul(x, w['mix_norm'], w['w_sample'], w['q_gain'], tm=n_seq)
    u = raw[:, :POOL_WIDTH]
    gq, gk = raw[:, POOL_WIDTH:POOL_WIDTH + GLA_QK], raw[:, POOL_WIDTH + GLA_QK:POOL_WIDTH + 2 * GLA_QK]
    c0 = POOL_WIDTH + 2 * GLA_QK
    gv, gr, ga = raw[:, c0:c0 + GLA_WIDTH], raw[:, c0 + GLA_WIDTH:c0 + 2 * GLA_WIDTH], raw[:, c0 + 2 * GLA_WIDTH:]

    q3 = q.reshape(n_seq, NSA_HEADS, HEAD_DIM)
    n_pool = cache_cmp.shape[1]
    cmpc = _cmp_pages(cache_cmp.reshape(-1, n_pool, PAGE_SIZE * KV_ROWS, HEAD_DIM), page_table, w['w_cmp'], layer)
    o_c, idx = _cmp_select(q3, cmpc)
    idx = idx[:, :, :SEL_TOPK].transpose(1, 0, 2)
    pidx = jnp.minimum(idx, nbp - 1)
    phys = jnp.take_along_axis(page_table[:, None, :], pidx // bpp, axis=2)
    blk_rows = (phys * bpp + pidx % bpp).reshape(-1).astype(jnp.int32)
    new_slot = jnp.argmax(idx >= nbp, axis=-1).reshape(-1).astype(jnp.int32)
    gates3 = jnp.concatenate([gates[:, :3 * NSA_HPG], gates[:, LANE:LANE + 3 * NSA_HPG]], axis=1)
    w_buf = cache_win.shape[2]
    o_nsa = _sel_win(blk_rows, new_slot, cache_sel.reshape(-1, n_pool * bpp, SEL_BLOCK * KV_ROWS, HEAD_DIM),
                     cache_win.reshape(-1, n_seq, w_buf * KV_ROWS, HEAD_DIM), q3, sel_new.reshape(n_seq, 1, kvw),
                     win_new.reshape(n_seq, 1, kvw), o_c, gates3.reshape(n_seq, NSA_HEADS, 3), layer)

    u_ctx = jnp.concatenate([cache_pool[layer], u[:, None, :]], axis=1)
    y_pool = _pool_sample(u_ctx.transpose(1, 0, 2), w['pool_w'], w['pool_scale'])
    o_gla, new_state = _gla_sample(
        gq.reshape(n_seq, GLA_QK, 1), gk.reshape(n_seq, GLA_QK, 1), gv.reshape(n_seq, 1, GLA_WIDTH),
        ga.reshape(n_seq, 1, LANE), gr.reshape(n_seq, 1, GLA_WIDTH), state_gla[layer].reshape(n_seq, GLA_QK, GLA_DV),
        w['wa2_t'], w['ba_col'], w['gla_gain'])

    def pad_rows(a):
        a = a.reshape(n_seq, 1, -1)
        return jnp.pad(a, ((0, 0), (0, SAMPLE_PAD - 1), (0, 0))).reshape(n_seq * SAMPLE_PAD, -1)

    mt = cache_mem.shape[2]
    xo = _post_mix(pad_rows(x), pad_rows(o_nsa), pad_rows(y_pool), pad_rows(o_gla), w['w_out'],
                   cache_mem[layer].reshape(n_seq, mt, 2 * MEM_WIDTH), w['mem_q_norm'], w['mem_wq'], w['mem_q_gain'],
                   w['mem_wo'], n_seq=n_seq, t_len=SAMPLE_PAD, tm=SAMPLE_PAD)
    x = xo.reshape(n_seq, SAMPLE_PAD, -1)[:, 0]
    x = _ffn(x, w['ffn2_norm'], w['ffn2_gu'], w['ffn2_down'], tm=n_seq)
    kv_shape = (n_seq, 1, 2, NSA_KV_HEADS, HEAD_DIM)
    win_buf = jnp.concatenate([cache_win[layer], win_new.reshape(kv_shape)], axis=1)[:, 1:]
    outs = (cmp_new.reshape(kv_shape), sel_new.reshape(kv_shape), win_buf, u_ctx[:, 1:],
            new_state.reshape(n_seq, GLA_HEADS, GLA_DK, GLA_DV))
    return x, outs


def kernel(x_prompt, x_sample, cache_cmp_kv, cache_sel_kv, cache_win_kv, cache_pool, state_gla, cache_mem_kv, page_table, mem_prompt, ffn1_norm, ffn1_w_gu, ffn1_w_down, mix_norm, w_in, nsa_q_gain, nsa_k_gain, nsa_gate_bias, nsa_cmp_wk, nsa_cmp_wv, pool_w, pool_scale, gla_wa2, gla_ba, gla_o_gain, w_out, mem_q_norm, mem_kv_norm, mem_wq, mem_wkv, mem_q_gain, mem_k_gain, mem_wo, ffn2_norm, ffn2_w_gu, ffn2_w_down):
    params = dict(ffn1_norm=ffn1_norm, ffn1_w_gu=ffn1_w_gu, ffn1_w_down=ffn1_w_down, mix_norm=mix_norm, w_in=w_in,
                  nsa_q_gain=nsa_q_gain, nsa_k_gain=nsa_k_gain, nsa_gate_bias=nsa_gate_bias, nsa_cmp_wk=nsa_cmp_wk,
                  nsa_cmp_wv=nsa_cmp_wv, pool_w=pool_w, pool_scale=pool_scale, gla_wa2=gla_wa2, gla_ba=gla_ba,
                  gla_o_gain=gla_o_gain, w_out=w_out, mem_q_norm=mem_q_norm, mem_kv_norm=mem_kv_norm, mem_wq=mem_wq,
                  mem_wkv=mem_wkv, mem_q_gain=mem_q_gain, mem_k_gain=mem_k_gain, mem_wo=mem_wo, ffn2_norm=ffn2_norm,
                  ffn2_w_gu=ffn2_w_gu, ffn2_w_down=ffn2_w_down)
    depth = w_in.shape[0]
    n_seq, s_len, d = x_prompt.shape
    n_dec, dec_len, _ = x_sample.shape
    assert dec_len == 1
    past = page_table.shape[1] * PAGE_SIZE
    tables_p = _rope_tables(jnp.arange(s_len))
    tables_s = _rope_tables(jnp.full((n_dec,), past))
    caches = (cache_cmp_kv, cache_sel_kv, cache_win_kv, cache_pool, state_gla, cache_mem_kv)
    xp = x_prompt.reshape(n_seq * s_len, d)
    xs = x_sample.reshape(n_dec, d)
    mem = mem_prompt.reshape(-1, d)
    p_outs, s_outs = [], []
    for l in range(depth):
        w = _layer_weights(params, l)
        xp, po = _prompt_layer(xp, mem, w, tables_p, n_seq=n_seq, s_len=s_len, tm=256)
        xs, so = _sample_layer(xs, w, tables_s, caches, page_table, l)
        p_outs.append(po)
        s_outs.append(so)
    stack = lambda outs, k: jnp.stack([o[k] for o in outs])
    return ((xp.reshape(n_seq, s_len, d), xs.reshape(n_dec, 1, d))
            + tuple(stack(p_outs, k) for k in range(6)) + tuple(stack(s_outs, k) for k in range(5)))
```

```python
import functools

import jax
import jax.numpy as jnp
import numpy as np
from jax import lax
from jax.experimental import pallas as pl
from jax.experimental.pallas import tpu as pltpu

F32 = jnp.float32
BF16 = jnp.bfloat16

D_MODEL = 2048
HEAD_DIM = 128
ROPE_THETA = 10000.0
EPS = 1e-6
NEG = -1e30
FORCE = 1e4
SCALE = HEAD_DIM ** -0.5
PAGE_SIZE = 128
NSA_HEADS = 8
NSA_KV_HEADS = 2
NSA_HPG = NSA_HEADS // NSA_KV_HEADS
NSA_Q_W = NSA_HEADS * HEAD_DIM
NSA_KV_W = NSA_KV_HEADS * HEAD_DIM
CMP_BLOCK = 64
SEL_BLOCK = 64
SEL_TOPK = 16
WINDOW = 512
POOL_WIDTH = 512
POOL_WINDOWS = (2, 4, 8, 16)
POOL_GROUP = POOL_WIDTH // len(POOL_WINDOWS)
POOL_BUF = max(POOL_WINDOWS) - 1
POOL_HIST = 16
GLA_WIDTH = 512
GLA_HEADS = 4
GLA_DK = 64
GLA_DV = 128
GLA_RANK = 16
GLA_TAU = 16.0
GLA_CHUNK = 64
GLA_SUB = 16
GLA_QK = GLA_HEADS * GLA_DK
MEM_HEADS = 4
MEM_WIDTH = MEM_HEADS * HEAD_DIM
LANE = 128
PAD_SCORE = -3.0e38

_C_Q = 0
_C_KV = _C_Q + NSA_Q_W
_C_GT = _C_KV + 6 * NSA_KV_W
_C_U = _C_GT + 3 * NSA_HEADS
_C_GQ = _C_U + POOL_WIDTH
_C_GK = _C_GQ + GLA_QK
_C_GV = _C_GK + GLA_QK
_C_GA = _C_GV + GLA_WIDTH
_C_GR = _C_GA + GLA_RANK
_C_END = _C_GR + GLA_WIDTH

VMEM_LIMIT = 48 * 1024 * 1024


def _cparams(sem):
    return pltpu.CompilerParams(dimension_semantics=sem, vmem_limit_bytes=VMEM_LIMIT)


def _dot(a, b):
    return jnp.dot(a, b, preferred_element_type=F32)


def _dot_nt(a, b):
    return lax.dot_general(a, b, (((1,), (1,)), ((), ())), preferred_element_type=F32)


def _rms(x, g):
    return x * lax.rsqrt(jnp.mean(x * x, axis=-1, keepdims=True) + EPS) * g


def _sigmoid(x):
    return 1.0 / (1.0 + jnp.exp(-x))


def _silu(x):
    return x * _sigmoid(x)


def _log_sigmoid(x):
    return jnp.minimum(x, 0.0) - jnp.log1p(jnp.exp(-jnp.abs(x)))


def _rope(y, cos, sin):
    return y * cos + pltpu.roll(y, HEAD_DIM // 2, axis=1) * sin


def _pad_cols(w, width):
    return jnp.pad(w, ((0, 0), (0, width - w.shape[1])))


def _ffn_body(x_ref, g_ref, wg_ref, wu_ref, wd_ref, o_ref, xn_ref):
    j = pl.program_id(1)

    @pl.when(j == 0)
    def _():
        xn_ref[...] = _rms(x_ref[...], g_ref[...]).astype(BF16)
        o_ref[...] = jnp.zeros_like(o_ref)

    xn = xn_ref[...]
    a = _dot(xn, wg_ref[...])
    b = _dot(xn, wu_ref[...])
    h = (_silu(a) * b).astype(BF16)
    o_ref[...] += _dot(h, wd_ref[...])

    @pl.when(j == pl.num_programs(1) - 1)
    def _():
        o_ref[...] = x_ref[...] + 0.5 * o_ref[...]


def _ffn(x, g, w_gu, w_down, layer, *, tm, tf=512):
    m, d = x.shape
    f = w_down.shape[1]
    nf = f // tf
    return pl.pallas_call(
        _ffn_body,
        out_shape=jax.ShapeDtypeStruct((m, d), F32),
        grid=(m // tm, nf),
        in_specs=[
            pl.BlockSpec((tm, d), lambda i, j: (i, 0)),
            pl.BlockSpec((1, d), lambda i, j: (0, 0)),
            pl.BlockSpec((None, d, tf), lambda i, j: (layer, 0, j)),
            pl.BlockSpec((None, d, tf), lambda i, j: (layer, 0, j + nf)),
            pl.BlockSpec((None, tf, d), lambda i, j: (layer, j, 0)),
        ],
        out_specs=pl.BlockSpec((tm, d), lambda i, j: (i, 0)),
        scratch_shapes=[pltpu.VMEM((tm, d), BF16)],
        compiler_params=_cparams(("parallel", "arbitrary")),
        name="ffn_half_step",
    )(x, g, w_gu, w_gu, w_down)


def _qkv_body(x_ref, g_ref, w_ref, cos_ref, sin_ref, qg_ref, kg_ref, gb_ref,
              q_ref, cmp_ref, sel_ref, win_ref, gt_ref):
    xn = _rms(x_ref[...], g_ref[...]).astype(BF16)
    h = _dot(xn, w_ref[...])
    cos = cos_ref[...]
    sin = sin_ref[...]

    def head(c0, gain):
        return _rope(_rms(h[:, c0:c0 + HEAD_DIM], gain), cos, sin)

    for hh in range(NSA_HEADS):
        c0 = _C_Q + hh * HEAD_DIM
        q_ref[:, hh * HEAD_DIM:(hh + 1) * HEAD_DIM] = head(c0, qg_ref[...]).astype(BF16)
    for br, out in enumerate((cmp_ref, sel_ref, win_ref)):
        base = _C_KV + br * 2 * NSA_KV_W
        for g in range(NSA_KV_HEADS):
            out[:, g * HEAD_DIM:(g + 1) * HEAD_DIM] = head(base + g * HEAD_DIM, kg_ref[br:br + 1, :])
        out[:, NSA_KV_W:2 * NSA_KV_W] = h[:, base + NSA_KV_W:base + 2 * NSA_KV_W]
    gt_ref[...] = _sigmoid(h[:, _C_GT:_C_GT + 2 * LANE] + gb_ref[...])


def _qkv_proj(x, g, w, cos, sin, q_gain, k_gain, gate_b, *, tm, tiles_per_seq):
    m, d = x.shape
    n = w.shape[1]
    row = lambda i: (i, 0)
    fixed = lambda i: (0, 0)
    pos = lambda i: (i % tiles_per_seq, 0)
    return pl.pallas_call(
        _qkv_body,
        out_shape=(
            jax.ShapeDtypeStruct((m, NSA_Q_W), BF16),
            jax.ShapeDtypeStruct((m, 2 * NSA_KV_W), F32),
            jax.ShapeDtypeStruct((m, 2 * NSA_KV_W), F32),
            jax.ShapeDtypeStruct((m, 2 * NSA_KV_W), F32),
            jax.ShapeDtypeStruct((m, 2 * LANE), F32),
        ),
        grid=(m // tm,),
        in_specs=[
            pl.BlockSpec((tm, d), row),
            pl.BlockSpec((1, d), fixed),
            pl.BlockSpec((d, n), fixed),
            pl.BlockSpec((tm, HEAD_DIM), pos),
            pl.BlockSpec((tm, HEAD_DIM), pos),
            pl.BlockSpec((1, HEAD_DIM), fixed),
            pl.BlockSpec((3, HEAD_DIM), fixed),
            pl.BlockSpec((1, 2 * LANE), fixed),
        ],
        out_specs=(
            pl.BlockSpec((tm, NSA_Q_W), row),
            pl.BlockSpec((tm, 2 * NSA_KV_W), row),
            pl.BlockSpec((tm, 2 * NSA_KV_W), row),
            pl.BlockSpec((tm, 2 * NSA_KV_W), row),
            pl.BlockSpec((tm, 2 * LANE), row),
        ),
        compiler_params=_cparams(("parallel",)),
        name="nsa_projection",
    )(x, g, w, cos, sin, q_gain, k_gain, gate_b)


def _softmax_av(s, v):
    e = jnp.exp(s - jnp.max(s, axis=-1, keepdims=True))
    return _dot(e.astype(BF16), v) / jnp.sum(e, axis=-1, keepdims=True)


EXP2_SCALE = SCALE * float(np.log2(np.e))
NSA_PREFIXES = 4


def _nsa_prompt_body(q_ref, gt_ref, kc_ref, vc_ref, ks_ref, vs_ref, kw_ref, vw_ref, wk_ref, wv_ref, o_ref,
                     kcb_ref, vcb_ref, ksb_ref, vsb_ref, kwb_ref, vwb_ref, acc_ref):
    i = pl.program_id(2)
    tq = q_ref.shape[0]
    s_len = kc_ref.shape[0]
    nb = s_len // CMP_BLOCK

    @pl.when(i == 0)
    def _():
        kcb_ref[...] = jnp.sum(kc_ref[...].reshape(nb, CMP_BLOCK, HEAD_DIM) * wk_ref[...][None], axis=1).astype(BF16)
        vcb_ref[...] = jnp.sum(vc_ref[...].reshape(nb, CMP_BLOCK, HEAD_DIM) * wv_ref[...][None], axis=1).astype(BF16)
        ksb_ref[...] = ks_ref[...].astype(BF16)
        vsb_ref[...] = vs_ref[...].astype(BF16)
        kwb_ref[...] = kw_ref[...].astype(BF16)
        vwb_ref[...] = vw_ref[...].astype(BF16)

    qpos = i * tq + lax.broadcasted_iota(jnp.int32, (tq, 1), 0)
    qpos_t = i * tq + lax.broadcasted_iota(jnp.int32, (1, tq), 1)
    kc = kcb_ref[...]
    vc = vcb_ref[...]
    blk = lax.broadcasted_iota(jnp.int32, (1, nb), 1)
    blk_t = lax.broadcasted_iota(jnp.int32, (nb, 1), 0)
    valid_c = ((blk + 1) * CMP_BLOCK - 1) <= qpos
    valid_t = ((blk_t + 1) * CMP_BLOCK - 1) <= qpos_t
    imp_t = jnp.zeros((nb, tq), F32)
    o_c = []
    for hh in range(NSA_HPG):
        qh = q_ref[:, hh * HEAD_DIM:(hh + 1) * HEAD_DIM]
        s = jnp.where(valid_c, _dot_nt(qh, kc) * SCALE, NEG)
        e = jnp.where(valid_c, jnp.exp(s - jnp.max(s, axis=-1, keepdims=True)), 0.0)
        p = e / jnp.maximum(jnp.sum(e, axis=-1, keepdims=True), 1e-30)
        o_c.append(_dot(p.astype(BF16), vc))
        s_t = jnp.where(valid_t, _dot_nt(kc, qh) * SCALE, NEG)
        e_t = jnp.where(valid_t, jnp.exp(s_t - jnp.max(s_t, axis=0, keepdims=True)), 0.0)
        imp_t = imp_t + e_t / jnp.maximum(jnp.sum(e_t, axis=0, keepdims=True), 1e-30)

    cur_t = qpos_t // SEL_BLOCK
    forced = jnp.where(blk_t == cur_t, 1, jnp.where(blk_t == cur_t - 1, 1, jnp.where(blk_t == 0, 1, 0)))
    score = jnp.where(blk_t > cur_t, -FORCE, jnp.where(forced > 0, FORCE, imp_t))
    rank = jnp.zeros((nb, tq), jnp.int32)
    for b2 in range(nb):
        other = score[b2:b2 + 1, :]
        tie = jnp.where(other == score, jnp.where(blk_t > b2, 1, 0), 0)
        rank = rank + jnp.where(other > score, 1, tie)
    keep = jnp.where(rank < min(SEL_TOPK, nb), 1.0, 0.0).T.astype(BF16)

    nt = s_len // tq
    n_prefix = min(NSA_PREFIXES, nt)
    tiles_per_prefix = nt // n_prefix
    for k in range(n_prefix):
        @pl.when(i // tiles_per_prefix == k)
        def _(k=k):
            width = (k + 1) * tiles_per_prefix * tq
            expand = jnp.where(
                lax.broadcasted_iota(jnp.int32, (nb, width), 1) // SEL_BLOCK
                == lax.broadcasted_iota(jnp.int32, (nb, width), 0), 1.0, 0.0).astype(BF16)
            keep_keys = _dot(keep, expand)
            kpos = lax.broadcasted_iota(jnp.int32, (1, width), 1)
            bias = jnp.where(keep_keys > 0.5, jnp.where(kpos <= qpos, 0.0, NEG), NEG)
            ks = ksb_ref[0:width, :]
            vs = vsb_ref[0:width, :]
            for hh in range(NSA_HPG):
                s = _dot_nt(q_ref[:, hh * HEAD_DIM:(hh + 1) * HEAD_DIM], ks) + bias
                e = jnp.exp2((s - jnp.max(s, axis=-1, keepdims=True)) * EXP2_SCALE)
                acc_ref[hh] = _dot(e.astype(BF16), vs) / jnp.sum(e, axis=-1, keepdims=True)

    span = WINDOW + tq
    kstart = pl.multiple_of(jnp.maximum(i * tq - WINDOW, 0), tq)
    kw = kwb_ref[pl.ds(kstart, span), :]
    vw = vwb_ref[pl.ds(kstart, span), :]
    rel = qpos - (kstart + lax.broadcasted_iota(jnp.int32, (1, span), 1))
    bias_w = jnp.where(rel >= 0, jnp.where(rel <= WINDOW, 0.0, NEG), NEG)

    gt = gt_ref[...]
    for hh in range(NSA_HPG):
        s = _dot_nt(q_ref[:, hh * HEAD_DIM:(hh + 1) * HEAD_DIM], kw) + bias_w
        e = jnp.exp2((s - jnp.max(s, axis=-1, keepdims=True)) * EXP2_SCALE)
        o_w = _dot(e.astype(BF16), vw) / jnp.sum(e, axis=-1, keepdims=True)
        o = (gt[:, 3 * hh:3 * hh + 1] * o_c[hh] + gt[:, 3 * hh + 1:3 * hh + 2] * acc_ref[hh]
             + gt[:, 3 * hh + 2:3 * hh + 3] * o_w)
        o_ref[:, hh * HEAD_DIM:(hh + 1) * HEAD_DIM] = o.astype(BF16)


def _nsa_prompt(q, gates, cmp_kv, sel_kv, win_kv, wk, wv, *, n_seq, s_len, tq):
    m = q.shape[0]
    nt = s_len // tq
    nb = s_len // CMP_BLOCK
    assert s_len >= WINDOW + tq and s_len % tq == 0 and tq % SEL_BLOCK == 0 and nt % min(NSA_PREFIXES, nt) == 0
    gw = NSA_HPG * HEAD_DIM
    row = lambda n, g, i: (n * nt + i, g)
    kspec = pl.BlockSpec((s_len, HEAD_DIM), lambda n, g, i: (n, g))
    vspec = pl.BlockSpec((s_len, HEAD_DIM), lambda n, g, i: (n, NSA_KV_HEADS + g))
    wspec = pl.BlockSpec((None, CMP_BLOCK, HEAD_DIM), lambda n, g, i: (g, 0, 0))
    return pl.pallas_call(
        _nsa_prompt_body,
        out_shape=jax.ShapeDtypeStruct((m, NSA_Q_W), BF16),
        grid=(n_seq, NSA_KV_HEADS, nt),
        in_specs=[
            pl.BlockSpec((tq, gw), row),
            pl.BlockSpec((tq, LANE), row),
            kspec, vspec, kspec, vspec, kspec, vspec, wspec, wspec,
        ],
        out_specs=pl.BlockSpec((tq, gw), row),
        scratch_shapes=[pltpu.VMEM((nb, HEAD_DIM), BF16), pltpu.VMEM((nb, HEAD_DIM), BF16)]
        + [pltpu.VMEM((s_len, HEAD_DIM), BF16)] * 4
        + [pltpu.VMEM((NSA_HPG, tq, HEAD_DIM), F32)],
        compiler_params=_cparams(("parallel", "parallel", "arbitrary")),
        name="nsa_prompt",
    )(q, gates, cmp_kv, cmp_kv, sel_kv, sel_kv, win_kv, win_kv, wk, wv)


def _pool_prompt_body(x_ref, g_ref, w_ref, pw_ref, ps_ref, u_ref, y_ref, hist_ref):
    j = pl.program_id(1)
    tm = x_ref.shape[0]

    @pl.when(j == 0)
    def _():
        hist_ref[...] = jnp.zeros_like(hist_ref)

    u = _dot(_rms(x_ref[...], g_ref[...]).astype(BF16), w_ref[...])
    u_ref[...] = u
    ext = jnp.concatenate([hist_ref[...], u], axis=0)
    t = j * tm + lax.broadcasted_iota(jnp.int32, (tm, 1), 0)
    for gi, win in enumerate(POOL_WINDOWS):
        c0, c1 = gi * POOL_GROUP, (gi + 1) * POOL_GROUP
        acc = ext[:, c0:c1]
        span = 1
        while span < win:
            acc = acc + pltpu.roll(acc, span, axis=0)
            span *= 2
        mean = acc[POOL_HIST:] / jnp.minimum(t + 1, win).astype(F32)
        y = _dot((mean - u[:, c0:c1]).astype(BF16), pw_ref[gi].astype(BF16))
        y_ref[:, c0:c1] = (y * ps_ref[:, c0:c1]).astype(BF16)
    hist_ref[...] = u[tm - POOL_HIST:]


def _pool_prompt(x, g, w_u, pool_w, pool_scale, *, n_seq, s_len, tm):
    d = x.shape[1]
    nt = s_len // tm
    row = lambda n, j: (n * nt + j, 0)
    fixed = lambda n, j: (0, 0)
    return pl.pallas_call(
        _pool_prompt_body,
        out_shape=(jax.ShapeDtypeStruct((n_seq * s_len, POOL_WIDTH), F32),
                   jax.ShapeDtypeStruct((n_seq * s_len, POOL_WIDTH), BF16)),
        grid=(n_seq, nt),
        in_specs=[
            pl.BlockSpec((tm, d), row),
            pl.BlockSpec((1, d), fixed),
            pl.BlockSpec((d, POOL_WIDTH), fixed),
            pl.BlockSpec((len(POOL_WINDOWS), POOL_GROUP, POOL_GROUP), lambda n, j: (0, 0, 0)),
            pl.BlockSpec((1, POOL_WIDTH), fixed),
        ],
        out_specs=(pl.BlockSpec((tm, POOL_WIDTH), row), pl.BlockSpec((tm, POOL_WIDTH), row)),
        scratch_shapes=[pltpu.VMEM((POOL_HIST, POOL_WIDTH), F32)],
        compiler_params=_cparams(("parallel", "arbitrary")),
        name="pool_prompt",
    )(x, g, w_u, pool_w, pool_scale)


def _gla_prompt_body(x_ref, g_ref, w_ref, wa2_ref, ba_ref, gain_ref, og_ref, st_ref, state_ref):
    j = pl.program_id(1)
    tm = x_ref.shape[0]

    @pl.when(j == 0)
    def _():
        state_ref[...] = jnp.zeros_like(state_ref)

    h = _dot(_rms(x_ref[...], g_ref[...]).astype(BF16), w_ref[...])
    c_k, c_v, c_r, c_a = GLA_QK, 2 * GLA_QK, 2 * GLA_QK + GLA_WIDTH, 2 * GLA_QK + 2 * GLA_WIDTH
    z = _dot(h[:, c_a:c_a + LANE].astype(BF16), wa2_ref[...]) + ba_ref[...]
    la_all = _log_sigmoid(z) / GLA_TAU

    lane = lax.broadcasted_iota(jnp.int32, (1, GLA_QK), 1)
    head_mask = [jnp.where(lane // GLA_DK == hd, 1.0, 0.0) for hd in range(GLA_HEADS)]
    rows = lax.broadcasted_iota(jnp.int32, (GLA_CHUNK, 1), 0)
    causal = rows >= lax.broadcasted_iota(jnp.int32, (1, GLA_CHUNK), 1)
    n_sub = GLA_CHUNK // GLA_SUB

    for c in range(tm // GLA_CHUNK):
        r0, r1 = c * GLA_CHUNK, (c + 1) * GLA_CHUNK
        q = h[r0:r1, 0:GLA_QK] * (GLA_DK ** -0.5)
        k = h[r0:r1, c_k:c_k + GLA_QK]
        b = la_all[r0:r1]
        sh = 1
        while sh < GLA_CHUNK:
            b = b + jnp.where(rows >= sh, pltpu.roll(b, sh, axis=0), 0.0)
            sh *= 2
        state = state_ref[...]
        state_b = state.astype(BF16)
        q_in = q * jnp.exp(b)
        a_parts = [[] for _ in range(GLA_HEADS)]
        for sb in range(n_sub):
            s0, s1 = sb * GLA_SUB, (sb + 1) * GLA_SUB
            ref_row = b[s0 - 1:s0] if sb > 0 else jnp.zeros((1, GLA_QK), F32)
            q_t = q[s0:s1] * jnp.exp(b[s0:s1] - ref_row)
            k_t = (k * jnp.exp(jnp.where(rows < s1, ref_row - b, 0.0))).astype(BF16)
            for hd in range(GLA_HEADS):
                a_parts[hd].append(_dot_nt((q_t * head_mask[hd]).astype(BF16), k_t))
        b_last = b[GLA_CHUNK - 1:GLA_CHUNK]
        k_out = k * jnp.exp(b_last - b)
        new_state = state * jnp.exp(b_last)
        for hd in range(GLA_HEADS):
            v = h[r0:r1, c_v + hd * GLA_DV:c_v + (hd + 1) * GLA_DV]
            vb = v.astype(BF16)
            a = jnp.where(causal, jnp.concatenate(a_parts[hd], axis=0), 0.0)
            o = _dot(a.astype(BF16), vb) + _dot_nt((q_in * head_mask[hd]).astype(BF16), state_b)
            gr = h[r0:r1, c_r + hd * GLA_DV:c_r + (hd + 1) * GLA_DV]
            og = _rms(o, gain_ref[:, hd * GLA_DV:(hd + 1) * GLA_DV]) * _silu(gr)
            og_ref[r0:r1, hd * GLA_DV:(hd + 1) * GLA_DV] = og.astype(BF16)
            new_state = new_state + _dot(v.T.astype(BF16), (k_out * head_mask[hd]).astype(BF16))
        state_ref[...] = new_state

    @pl.when(j == pl.num_programs(1) - 1)
    def _():
        st_ref[...] = state_ref[...]


def _gla_prompt(x, g, w_gla, wa2, ba, gain, *, n_seq, s_len, tm):
    d = x.shape[1]
    nt = s_len // tm
    row = lambda n, j: (n * nt + j, 0)
    fixed = lambda n, j: (0, 0)
    return pl.pallas_call(
        _gla_prompt_body,
        out_shape=(jax.ShapeDtypeStruct((n_seq * s_len, GLA_WIDTH), BF16),
                   jax.ShapeDtypeStruct((n_seq, GLA_DV, GLA_QK), F32)),
        grid=(n_seq, nt),
        in_specs=[
            pl.BlockSpec((tm, d), row),
            pl.BlockSpec((1, d), fixed),
            pl.BlockSpec((d, w_gla.shape[1]), fixed),
            pl.BlockSpec((LANE, GLA_QK), fixed),
            pl.BlockSpec((1, GLA_QK), fixed),
            pl.BlockSpec((1, GLA_WIDTH), fixed),
        ],
        out_specs=(pl.BlockSpec((tm, GLA_WIDTH), row),
                   pl.BlockSpec((None, GLA_DV, GLA_QK), lambda n, j: (n, 0, 0))),
        scratch_shapes=[pltpu.VMEM((GLA_DV, GLA_QK), F32)],
        compiler_params=_cparams(("parallel", "arbitrary")),
        name="gla_prompt",
    )(x, g, w_gla, wa2, ba, gain)


def _norm_matmul_body(x_ref, g_ref, w_ref, hg_ref, o_ref, *, n_norm_heads):
    h = _dot(_rms(x_ref[...], g_ref[...]).astype(BF16), w_ref[...])
    if n_norm_heads:
        for hd in range(n_norm_heads):
            c0 = hd * HEAD_DIM
            o_ref[:, c0:c0 + HEAD_DIM] = _rms(h[:, c0:c0 + HEAD_DIM], hg_ref[...])
        c0 = n_norm_heads * HEAD_DIM
        o_ref[:, c0:] = h[:, c0:]
    else:
        o_ref[...] = h


def _norm_matmul(x, g, w, head_gain, *, tm, n_norm_heads=0):
    m, d = x.shape
    n = w.shape[1]
    return pl.pallas_call(
        functools.partial(_norm_matmul_body, n_norm_heads=n_norm_heads),
        out_shape=jax.ShapeDtypeStruct((m, n), F32),
        grid=(m // tm,),
        in_specs=[
            pl.BlockSpec((tm, d), lambda i: (i, 0)),
            pl.BlockSpec((1, d), lambda i: (0, 0)),
            pl.BlockSpec((d, n), lambda i: (0, 0)),
            pl.BlockSpec((1, HEAD_DIM), lambda i: (0, 0)),
        ],
        out_specs=pl.BlockSpec((tm, n), lambda i: (i, 0)),
        compiler_params=_cparams(("parallel",)),
        name="norm_matmul",
    )(x, g, w, head_gain)


def _post_body(x_ref, on_ref, yp_ref, og_ref, wout_ref, mk_ref, mv_ref, qn_ref, wq_ref, qg_ref, wo_ref, o_ref):
    c1, c2 = NSA_Q_W, NSA_Q_W + POOL_WIDTH
    x1 = (x_ref[...] + _dot(on_ref[...], wout_ref[0:c1, :]) + _dot(yp_ref[...], wout_ref[c1:c2, :])
          + _dot(og_ref[...], wout_ref[c2:, :]))
    q = _dot(_rms(x1, qn_ref[...]).astype(BF16), wq_ref[...])
    outs = []
    for hd in range(MEM_HEADS):
        c0 = hd * HEAD_DIM
        qh = _rms(q[:, c0:c0 + HEAD_DIM], qg_ref[...]).astype(BF16)
        s = _dot_nt(qh, mk_ref[:, c0:c0 + HEAD_DIM].astype(BF16)) * SCALE
        outs.append(_softmax_av(s, mv_ref[:, c0:c0 + HEAD_DIM].astype(BF16)).astype(BF16))
    o_ref[...] = x1 + _dot(jnp.concatenate(outs, axis=1), wo_ref[...])


def _post_mix(x, o_nsa, y_pool, o_gla, w_out, mem_kv, q_norm, wq, q_gain, wo, *, n_seq, t_len, tm):
    d = x.shape[1]
    nt = t_len // tm
    mt = mem_kv.shape[1]
    row = lambda n, j: (n * nt + j, 0)
    fixed = lambda n, j: (0, 0)
    return pl.pallas_call(
        _post_body,
        out_shape=jax.ShapeDtypeStruct(x.shape, F32),
        grid=(n_seq, nt),
        in_specs=[
            pl.BlockSpec((tm, d), row),
            pl.BlockSpec((tm, NSA_Q_W), row),
            pl.BlockSpec((tm, POOL_WIDTH), row),
            pl.BlockSpec((tm, GLA_WIDTH), row),
            pl.BlockSpec(w_out.shape, fixed),
            pl.BlockSpec((None, mt, MEM_WIDTH), lambda n, j: (n, 0, 0)),
            pl.BlockSpec((None, mt, MEM_WIDTH), lambda n, j: (n, 0, 1)),
            pl.BlockSpec((1, d), fixed),
            pl.BlockSpec((d, MEM_WIDTH), fixed),
            pl.BlockSpec((1, HEAD_DIM), fixed),
            pl.BlockSpec((MEM_WIDTH, d), fixed),
        ],
        out_specs=pl.BlockSpec((tm, d), row),
        compiler_params=_cparams(("parallel", "parallel")),
        name="mix_out_mem_attn",
    )(x, o_nsa, y_pool, o_gla, w_out, mem_kv, mem_kv, q_norm, wq, q_gain, wo)


PAGES_PER_STEP = 16


KV_ROWS = 2 * NSA_KV_HEADS
SUBLANES = 8


def _cmp_pages_body(pt_ref, *refs):
    del pt_ref
    pages, w_ref, o_ref = refs[:PAGES_PER_STEP], refs[PAGES_PER_STEP], refs[PAGES_PER_STEP + 1]
    w = w_ref[...]
    blk_rows = CMP_BLOCK * KV_ROWS
    sums = []
    for page in pages:
        for half in range(PAGE_SIZE // CMP_BLOCK):
            prod = page[half * blk_rows:(half + 1) * blk_rows, :] * w
            part = jnp.sum(prod.reshape(blk_rows // SUBLANES, SUBLANES, HEAD_DIM), axis=0)
            sums.append(part[0:KV_ROWS] + part[KV_ROWS:2 * KV_ROWS])
    for r in range(KV_ROWS):
        o_ref[r] = jnp.concatenate([s[r:r + 1] for s in sums], axis=0)


def _cmp_pages(cache, page_table, w_cmp, layer):
    n_seq, n_pages = page_table.shape
    page_rows = cache.shape[2]
    bpp = PAGE_SIZE // CMP_BLOCK
    assert n_pages % PAGES_PER_STEP == 0 and 2 * KV_ROWS == SUBLANES

    def page_spec(k):
        return pl.BlockSpec((None, None, page_rows, HEAD_DIM),
                            lambda n, j, pt: (layer, pt[n, j * PAGES_PER_STEP + k], 0, 0))

    return pl.pallas_call(
        _cmp_pages_body,
        out_shape=jax.ShapeDtypeStruct((n_seq, KV_ROWS, n_pages * bpp, HEAD_DIM), F32),
        grid_spec=pltpu.PrefetchScalarGridSpec(
            num_scalar_prefetch=1,
            grid=(n_seq, n_pages // PAGES_PER_STEP),
            in_specs=[page_spec(k) for k in range(PAGES_PER_STEP)]
            + [pl.BlockSpec((CMP_BLOCK * KV_ROWS, HEAD_DIM), lambda n, j, pt: (0, 0))],
            out_specs=pl.BlockSpec((None, KV_ROWS, PAGES_PER_STEP * bpp, HEAD_DIM), lambda n, j, pt: (n, 0, j, 0)),
        ),
        compiler_params=_cparams(("parallel", "arbitrary")),
        name="cmp_pages",
    )(page_table, *([cache] * PAGES_PER_STEP), w_cmp)


def _cmp_select_body(q_ref, c_ref, oc_ref, idx_ref, *, n_past_blocks):
    n_seq = q_ref.shape[0]
    nbp = n_past_blocks
    width = idx_ref.shape[-1]
    cur = nbp
    q = q_ref[...]
    lane = lax.broadcasted_iota(jnp.int32, (n_seq, width), 1)
    lane_f = lane.astype(F32)
    row_head = lax.broadcasted_iota(jnp.int32, (1, NSA_HEADS, 1), 1)
    o_c = jnp.zeros((n_seq, NSA_HEADS, HEAD_DIM), F32)
    for g in range(NSA_KV_HEADS):
        kc = c_ref[:, g].astype(BF16)
        vc = c_ref[:, NSA_KV_HEADS + g].astype(BF16)
        s = lax.dot_general(q, kc, (((2,), (2,)), ((0,), (0,))), preferred_element_type=F32) * SCALE
        e = jnp.exp(s - jnp.max(s, axis=-1, keepdims=True))
        p = e / jnp.maximum(jnp.sum(e, axis=-1, keepdims=True), 1e-30)
        o_g = lax.dot_general(p.astype(BF16), vc, (((2,), (1,)), ((0,), (0,))), preferred_element_type=F32)
        in_group = row_head // NSA_HPG == g
        o_c = jnp.where(in_group, o_g, o_c)
        imp = jnp.sum(jnp.where(in_group, p, 0.0), axis=1)
        imp = jnp.concatenate([imp, jnp.zeros((n_seq, width - nbp), F32)], axis=1)
        forced = jnp.where(lane == cur, 1, jnp.where(lane == cur - 1, 1, jnp.where(lane == 0, 1, 0)))
        score = jnp.where(lane > cur, PAD_SCORE, jnp.where(forced > 0, FORCE, imp))
        picks = jnp.zeros((n_seq, width), F32)
        for it in range(SEL_TOPK):
            top = jnp.max(score, axis=-1, keepdims=True)
            first = jnp.min(jnp.where(score == top, lane_f, float(width)), axis=-1, keepdims=True)
            picks = jnp.where(lane == it, first, picks)
            score = jnp.where(lane_f == first, PAD_SCORE, score)
        idx_ref[g] = picks.astype(jnp.int32)
    oc_ref[...] = o_c


def _cmp_select(q3, cmpc):
    n_seq, _, nbp, _ = cmpc.shape
    assert nbp + 1 >= SEL_TOPK and nbp % LANE == 0
    idx_w = nbp + LANE
    return pl.pallas_call(
        functools.partial(_cmp_select_body, n_past_blocks=nbp),
        out_shape=(jax.ShapeDtypeStruct((n_seq, NSA_HEADS, HEAD_DIM), F32),
                   jax.ShapeDtypeStruct((NSA_KV_HEADS, n_seq, idx_w), jnp.int32)),
        grid=(1,),
        in_specs=[
            pl.BlockSpec((n_seq, NSA_HEADS, HEAD_DIM), lambda i: (0, 0, 0)),
            pl.BlockSpec(cmpc.shape, lambda i: (0, 0, 0, 0)),
        ],
        out_specs=(pl.BlockSpec((n_seq, NSA_HEADS, HEAD_DIM), lambda i: (0, 0, 0)),
                   pl.BlockSpec((NSA_KV_HEADS, n_seq, idx_w), lambda i: (0, 0, 0))),
        compiler_params=_cparams(("arbitrary",)),
        name="cmp_select",
    )(q3, cmpc)


def _one_token_attn(q, rows, g, k_new, v_new, extra_bias):
    n_rows = rows.shape[0]
    kind = lax.broadcasted_iota(jnp.int32, (1, n_rows), 1) % KV_ROWS
    s = _dot_nt(q, rows) * SCALE + jnp.where(kind == g, 0.0, NEG)
    if extra_bias is not None:
        s = s + extra_bias
    s_new = jnp.sum(q.astype(F32) * k_new, axis=-1, keepdims=True) * SCALE
    top = jnp.maximum(jnp.max(s, axis=-1, keepdims=True), s_new)
    e = jnp.exp(s - top)
    e_new = jnp.exp(s_new - top)
    e_val = pltpu.roll(e, NSA_KV_HEADS, axis=1)
    return (_dot(e_val.astype(BF16), rows) + e_new * v_new) / (jnp.sum(e, axis=-1, keepdims=True) + e_new)


def _sel_win_body(blk_ref, new_ref, *refs):
    del blk_ref
    n = pl.program_id(0)
    nk = SEL_TOPK
    n_sel = NSA_KV_HEADS * nk
    sel_refs = refs[:n_sel]
    win_ref, q_ref, sn_ref, wn_ref, oc_ref, gt_ref, o_ref = refs[n_sel:]
    q = q_ref[...]
    sel_new = sn_ref[...]
    win_new = wn_ref[...]
    gt = gt_ref[...]
    row_head = lax.broadcasted_iota(jnp.int32, (NSA_HEADS, 1), 0)
    blk_rows = SEL_BLOCK * KV_ROWS
    key_blk = lax.broadcasted_iota(jnp.int32, (1, nk * blk_rows), 1) // blk_rows
    win_rows = win_ref[...].astype(BF16)
    o_s = jnp.zeros((NSA_HEADS, HEAD_DIM), F32)
    o_w = jnp.zeros((NSA_HEADS, HEAD_DIM), F32)
    for g in range(NSA_KV_HEADS):
        k0, v0 = g * HEAD_DIM, NSA_KV_W + g * HEAD_DIM
        rows = jnp.concatenate([r[...] for r in sel_refs[g * nk:(g + 1) * nk]], axis=0).astype(BF16)
        bias = jnp.where(key_blk == new_ref[n * NSA_KV_HEADS + g], NEG, 0.0)
        o_sg = _one_token_attn(q, rows, g, sel_new[:, k0:k0 + HEAD_DIM], sel_new[:, v0:v0 + HEAD_DIM], bias)
        o_wg = _one_token_attn(q, win_rows, g, win_new[:, k0:k0 + HEAD_DIM], win_new[:, v0:v0 + HEAD_DIM], None)
        in_group = row_head // NSA_HPG == g
        o_s = jnp.where(in_group, o_sg, o_s)
        o_w = jnp.where(in_group, o_wg, o_w)
    o_ref[...] = (gt[:, 0:1] * oc_ref[...] + gt[:, 1:2] * o_s + gt[:, 2:3] * o_w).astype(BF16)


def _sel_win(blk_rows, new_slot, cache_sel, cache_win, q3, sel_new, win_new, o_c, gates3, layer):
    n_seq = q3.shape[0]
    nk = SEL_TOPK

    def sel_spec(g, k):
        return pl.BlockSpec((None, None, SEL_BLOCK * KV_ROWS, HEAD_DIM),
                            lambda n, br, ns: (layer, br[(n * NSA_KV_HEADS + g) * nk + k], 0, 0))

    sel_specs = [sel_spec(g, k) for g in range(NSA_KV_HEADS) for k in range(nk)]
    win_specs = [pl.BlockSpec((None, None, cache_win.shape[2], HEAD_DIM), lambda n, br, ns: (layer, n, 0, 0))]
    tok = lambda n, br, ns: (n, 0, 0)
    return pl.pallas_call(
        _sel_win_body,
        out_shape=jax.ShapeDtypeStruct((n_seq, NSA_HEADS, HEAD_DIM), BF16),
        grid_spec=pltpu.PrefetchScalarGridSpec(
            num_scalar_prefetch=2,
            grid=(n_seq,),
            in_specs=sel_specs + win_specs + [
                pl.BlockSpec((None, NSA_HEADS, HEAD_DIM), tok),
                pl.BlockSpec((None, 1, 2 * NSA_KV_W), tok),
                pl.BlockSpec((None, 1, 2 * NSA_KV_W), tok),
                pl.BlockSpec((None, NSA_HEADS, HEAD_DIM), tok),
                pl.BlockSpec((None, NSA_HEADS, 3), tok),
            ],
            out_specs=pl.BlockSpec((None, NSA_HEADS, HEAD_DIM), tok),
        ),
        compiler_params=_cparams(("parallel",)),
        name="sel_win_sample",
    )(blk_rows, new_slot, *([cache_sel] * len(sel_specs)), *([cache_win] * len(win_specs)),
      q3, sel_new, win_new, o_c, gates3)


def _pool_sample_body(uc_ref, pw_ref, ps_ref, y_ref):
    rows = uc_ref.shape[0]
    for gi, win in enumerate(POOL_WINDOWS):
        c0, c1 = gi * POOL_GROUP, (gi + 1) * POOL_GROUP
        acc = uc_ref[rows - 1, :, c0:c1]
        for r in range(rows - 2, rows - 1 - win, -1):
            acc = acc + uc_ref[r, :, c0:c1]
        d = acc / float(win) - uc_ref[rows - 1, :, c0:c1]
        y_ref[:, c0:c1] = (_dot(d.astype(BF16), pw_ref[gi].astype(BF16)) * ps_ref[:, c0:c1]).astype(BF16)


def _pool_sample(u_ctx_t, pool_w, pool_scale):
    rows, n_seq, width = u_ctx_t.shape
    assert rows >= max(POOL_WINDOWS)
    return pl.pallas_call(
        _pool_sample_body,
        out_shape=jax.ShapeDtypeStruct((n_seq, width), BF16),
        name="pool_sample",
    )(u_ctx_t, pool_w, pool_scale)


def _gla_sample_body(q_ref, k_ref, v_ref, ga_ref, gr_ref, s_ref, wa2t_ref, ba_ref, gain_ref, og_ref, so_ref):
    z = jnp.sum(wa2t_ref[...] * ga_ref[...], axis=-1, keepdims=True) + ba_ref[...]
    decay = jnp.exp(_log_sigmoid(z) / GLA_TAU)
    v = v_ref[...]
    v_rows = jnp.concatenate(
        [jnp.broadcast_to(v[:, hd * GLA_DV:(hd + 1) * GLA_DV], (GLA_DK, GLA_DV)) for hd in range(GLA_HEADS)], axis=0)
    state = decay * s_ref[...] + k_ref[...] * v_rows
    so_ref[...] = state
    qs = (q_ref[...] * (GLA_DK ** -0.5)) * state
    gr = gr_ref[...]
    for hd in range(GLA_HEADS):
        c0, c1 = hd * GLA_DV, (hd + 1) * GLA_DV
        o = jnp.sum(qs[hd * GLA_DK:(hd + 1) * GLA_DK], axis=0, keepdims=True)
        og_ref[:, c0:c1] = (_rms(o, gain_ref[:, c0:c1]) * _silu(gr[:, c0:c1])).astype(BF16)


def _gla_sample(q_col, k_col, v, ga, gr, state, wa2_t, ba_col, gain):
    n_seq = v.shape[0]
    tok = lambda n: (n, 0, 0)
    fixed = lambda n: (0, 0)
    return pl.pallas_call(
        _gla_sample_body,
        out_shape=(jax.ShapeDtypeStruct((n_seq, 1, GLA_WIDTH), BF16),
                   jax.ShapeDtypeStruct((n_seq, GLA_QK, GLA_DV), F32)),
        grid=(n_seq,),
        in_specs=[
            pl.BlockSpec((None, GLA_QK, 1), tok),
            pl.BlockSpec((None, GLA_QK, 1), tok),
            pl.BlockSpec((None, 1, GLA_WIDTH), tok),
            pl.BlockSpec((None, 1, LANE), tok),
            pl.BlockSpec((None, 1, GLA_WIDTH), tok),
            pl.BlockSpec((None, GLA_QK, GLA_DV), tok),
            pl.BlockSpec((GLA_QK, LANE), fixed),
            pl.BlockSpec((GLA_QK, 1), fixed),
            pl.BlockSpec((1, GLA_WIDTH), fixed),
        ],
        out_specs=(pl.BlockSpec((None, 1, GLA_WIDTH), tok), pl.BlockSpec((None, GLA_QK, GLA_DV), tok)),
        compiler_params=_cparams(("parallel",)),
        name="gla_sample",
    )(q_col, k_col, v, ga, gr, state, wa2_t, ba_col, gain)


def _rope_tables(pos):
    half = HEAD_DIM // 2
    inv = ROPE_THETA ** (-jnp.arange(half, dtype=F32) / half)
    ang = pos.astype(F32)[:, None] * inv[None, :]
    cos, sin = jnp.cos(ang), jnp.sin(ang)
    return jnp.concatenate([cos, cos], axis=1), jnp.concatenate([-sin, sin], axis=1)


def _layer_weights(p, l):
    w_in = p['w_in'][l]
    n_gate = 3 * NSA_HPG
    gate_cols = [_pad_cols(w_in[:, _C_GT + g * n_gate:_C_GT + (g + 1) * n_gate], LANE) for g in range(NSA_KV_HEADS)]
    gate_b = p['nsa_gate_bias'][l]
    ga_cols = _pad_cols(w_in[:, _C_GA:_C_GR], LANE)
    gla_cols = [w_in[:, _C_GQ:_C_GA], w_in[:, _C_GR:_C_END], ga_cols]
    row = lambda a: a.reshape(1, -1)
    return dict(
        layer=l,
        ffn1_norm=row(p['ffn1_norm'][l]), ffn1_gu=p['ffn1_w_gu_b'], ffn1_down=p['ffn1_w_down_b'],
        ffn2_norm=row(p['ffn2_norm'][l]), ffn2_gu=p['ffn2_w_gu_b'], ffn2_down=p['ffn2_w_down_b'],
        mix_norm=row(p['mix_norm'][l]),
        w_qkv=jnp.concatenate([w_in[:, _C_Q:_C_GT]] + gate_cols, axis=1).astype(BF16),
        gate_b=jnp.concatenate([_pad_cols(row(gate_b[g * n_gate:(g + 1) * n_gate]), LANE) for g in range(NSA_KV_HEADS)], axis=1),
        q_gain=row(p['nsa_q_gain'][l]), k_gain=p['nsa_k_gain'][l],
        wk=p['nsa_cmp_wk'][l], wv=p['nsa_cmp_wv'][l],
        w_cmp=jnp.stack([p['nsa_cmp_wk'][l], p['nsa_cmp_wv'][l]]).transpose(2, 0, 1, 3).reshape(-1, HEAD_DIM),
        w_u=w_in[:, _C_U:_C_GQ].astype(BF16),
        w_gla=jnp.concatenate(gla_cols, axis=1).astype(BF16),
        w_sample=jnp.concatenate([w_in[:, _C_U:_C_GQ]] + gla_cols, axis=1).astype(BF16),
        pool_w=p['pool_w'][l], pool_scale=row(p['pool_scale'][l]),
        wa2=jnp.pad(p['gla_wa2'][l], ((0, LANE - GLA_RANK), (0, 0))).astype(BF16),
        wa2_t=_pad_cols(p['gla_wa2'][l].T, LANE),
        ba=row(p['gla_ba'][l]), ba_col=p['gla_ba'][l].reshape(-1, 1),
        gla_gain=row(p['gla_o_gain'][l]),
        w_out=p['w_out'][l].astype(BF16),
        mem_q_norm=row(p['mem_q_norm'][l]), mem_kv_norm=row(p['mem_kv_norm'][l]),
        mem_wq=p['mem_wq'][l].astype(BF16), mem_wkv=p['mem_wkv'][l].astype(BF16),
        mem_q_gain=row(p['mem_q_gain'][l]), mem_k_gain=row(p['mem_k_gain'][l]),
        mem_wo=p['mem_wo'][l].astype(BF16),
    )


def _prompt_layer(x, mem, w, tables, *, n_seq, s_len, tm):
    cos, sin = tables
    x = _ffn(x, w['ffn1_norm'], w['ffn1_gu'], w['ffn1_down'], w['layer'], tm=min(512, n_seq * s_len))
    q, cmp_kv, sel_kv, win_kv, gates = _qkv_proj(
        x, w['mix_norm'], w['w_qkv'], cos, sin, w['q_gain'], w['k_gain'], w['gate_b'], tm=tm, tiles_per_seq=s_len // tm)
    o_nsa = _nsa_prompt(q, gates, cmp_kv, sel_kv, win_kv, w['wk'], w['wv'], n_seq=n_seq, s_len=s_len, tq=tm)
    u, y_pool = _pool_prompt(x, w['mix_norm'], w['w_u'], w['pool_w'], w['pool_scale'], n_seq=n_seq, s_len=s_len, tm=tm)
    o_gla, state_t = _gla_prompt(x, w['mix_norm'], w['w_gla'], w['wa2'], w['ba'], w['gla_gain'],
                                 n_seq=n_seq, s_len=s_len, tm=tm)
    mkv = _norm_matmul(mem, w['mem_kv_norm'], w['mem_wkv'], w['mem_k_gain'], tm=tm, n_norm_heads=MEM_HEADS)
    mt = mem.shape[0] // n_seq
    x = _post_mix(x, o_nsa, y_pool, o_gla, w['w_out'], mkv.reshape(n_seq, mt, 2 * MEM_WIDTH), w['mem_q_norm'],
                  w['mem_wq'], w['mem_q_gain'], w['mem_wo'], n_seq=n_seq, t_len=s_len, tm=tm)
    x = _ffn(x, w['ffn2_norm'], w['ffn2_gu'], w['ffn2_down'], w['layer'], tm=min(512, n_seq * s_len))
    w_keep = min(WINDOW, s_len)
    kv_shape = (n_seq, s_len, 2, NSA_KV_HEADS, HEAD_DIM)
    state = state_t.reshape(n_seq, GLA_DV, GLA_HEADS, GLA_DK).transpose(0, 2, 3, 1)
    outs = (cmp_kv.reshape(kv_shape), sel_kv.reshape(kv_shape), win_kv.reshape(kv_shape)[:, s_len - w_keep:],
            u.reshape(n_seq, s_len, POOL_WIDTH)[:, s_len - POOL_BUF:], state,
            mkv.reshape(n_seq, mt, 2, MEM_HEADS, HEAD_DIM))
    return x, outs


SAMPLE_PAD = 8


def _sample_layer(x, w, tables, caches, page_table, layer):
    cache_cmp, cache_sel, cache_win, cache_pool, state_gla, cache_mem = caches
    cos, sin = tables
    n_seq = x.shape[0]
    n_pages = page_table.shape[1]
    bpp = PAGE_SIZE // SEL_BLOCK
    nbp = n_pages * bpp
    kvw = 2 * NSA_KV_W
    x = _ffn(x, w['ffn1_norm'], w['ffn1_gu'], w['ffn1_down'], w['layer'], tm=n_seq)
    q, cmp_new, sel_new, win_new, gates = _qkv_proj(
        x, w['mix_norm'], w['w_qkv'], cos, sin, w['q_gain'], w['k_gain'], w['gate_b'], tm=n_seq, tiles_per_seq=1)
    raw = _norm_matmul(x, w['mix_norm'], w['w_sample'], w['q_gain'], tm=n_seq)
    u = raw[:, :POOL_WIDTH]
    gq, gk = raw[:, POOL_WIDTH:POOL_WIDTH + GLA_QK], raw[:, POOL_WIDTH + GLA_QK:POOL_WIDTH + 2 * GLA_QK]
    c0 = POOL_WIDTH + 2 * GLA_QK
    gv, gr, ga = raw[:, c0:c0 + GLA_WIDTH], raw[:, c0 + GLA_WIDTH:c0 + 2 * GLA_WIDTH], raw[:, c0 + 2 * GLA_WIDTH:]

    q3 = q.reshape(n_seq, NSA_HEADS, HEAD_DIM)
    n_pool = cache_cmp.shape[1]
    cmpc = _cmp_pages(cache_cmp.reshape(-1, n_pool, PAGE_SIZE * KV_ROWS, HEAD_DIM), page_table, w['w_cmp'], layer)
    o_c, idx = _cmp_select(q3, cmpc)
    idx = idx[:, :, :SEL_TOPK].transpose(1, 0, 2)
    pidx = jnp.minimum(idx, nbp - 1)
    phys = jnp.take_along_axis(page_table[:, None, :], pidx // bpp, axis=2)
    blk_rows = (phys * bpp + pidx % bpp).reshape(-1).astype(jnp.int32)
    new_slot = jnp.argmax(idx >= nbp, axis=-1).reshape(-1).astype(jnp.int32)
    gates3 = jnp.concatenate([gates[:, :3 * NSA_HPG], gates[:, LANE:LANE + 3 * NSA_HPG]], axis=1)
    w_buf = cache_win.shape[2]
    o_nsa = _sel_win(blk_rows, new_slot, cache_sel.reshape(-1, n_pool * bpp, SEL_BLOCK * KV_ROWS, HEAD_DIM),
                     cache_win.reshape(-1, n_seq, w_buf * KV_ROWS, HEAD_DIM), q3, sel_new.reshape(n_seq, 1, kvw),
                     win_new.reshape(n_seq, 1, kvw), o_c, gates3.reshape(n_seq, NSA_HEADS, 3), layer)

    u_ctx = jnp.concatenate([cache_pool[layer], u[:, None, :]], axis=1)
    y_pool = _pool_sample(u_ctx.transpose(1, 0, 2), w['pool_w'], w['pool_scale'])
    o_gla, new_state = _gla_sample(
        gq.reshape(n_seq, GLA_QK, 1), gk.reshape(n_seq, GLA_QK, 1), gv.reshape(n_seq, 1, GLA_WIDTH),
        ga.reshape(n_seq, 1, LANE), gr.reshape(n_seq, 1, GLA_WIDTH), state_gla[layer].reshape(n_seq, GLA_QK, GLA_DV),
        w['wa2_t'], w['ba_col'], w['gla_gain'])

    def pad_rows(a):
        a = a.reshape(n_seq, 1, -1)
        return jnp.pad(a, ((0, 0), (0, SAMPLE_PAD - 1), (0, 0))).reshape(n_seq * SAMPLE_PAD, -1)

    mt = cache_mem.shape[2]
    xo = _post_mix(pad_rows(x), pad_rows(o_nsa), pad_rows(y_pool), pad_rows(o_gla), w['w_out'],
                   cache_mem[layer].reshape(n_seq, mt, 2 * MEM_WIDTH), w['mem_q_norm'], w['mem_wq'], w['mem_q_gain'],
                   w['mem_wo'], n_seq=n_seq, t_len=SAMPLE_PAD, tm=SAMPLE_PAD)
    x = xo.reshape(n_seq, SAMPLE_PAD, -1)[:, 0]
    x = _ffn(x, w['ffn2_norm'], w['ffn2_gu'], w['ffn2_down'], w['layer'], tm=n_seq)
    kv_shape = (n_seq, 1, 2, NSA_KV_HEADS, HEAD_DIM)
    win_buf = jnp.concatenate([cache_win[layer], win_new.reshape(kv_shape)], axis=1)[:, 1:]
    outs = (cmp_new.reshape(kv_shape), sel_new.reshape(kv_shape), win_buf, u_ctx[:, 1:],
            new_state.reshape(n_seq, GLA_HEADS, GLA_DK, GLA_DV))
    return x, outs


def kernel(x_prompt, x_sample, cache_cmp_kv, cache_sel_kv, cache_win_kv, cache_pool, state_gla, cache_mem_kv, page_table, mem_prompt, ffn1_norm, ffn1_w_gu, ffn1_w_down, mix_norm, w_in, nsa_q_gain, nsa_k_gain, nsa_gate_bias, nsa_cmp_wk, nsa_cmp_wv, pool_w, pool_scale, gla_wa2, gla_ba, gla_o_gain, w_out, mem_q_norm, mem_kv_norm, mem_wq, mem_wkv, mem_q_gain, mem_k_gain, mem_wo, ffn2_norm, ffn2_w_gu, ffn2_w_down):
    params = dict(ffn1_norm=ffn1_norm, ffn1_w_gu=ffn1_w_gu, ffn1_w_down=ffn1_w_down, mix_norm=mix_norm, w_in=w_in,
                  nsa_q_gain=nsa_q_gain, nsa_k_gain=nsa_k_gain, nsa_gate_bias=nsa_gate_bias, nsa_cmp_wk=nsa_cmp_wk,
                  nsa_cmp_wv=nsa_cmp_wv, pool_w=pool_w, pool_scale=pool_scale, gla_wa2=gla_wa2, gla_ba=gla_ba,
                  gla_o_gain=gla_o_gain, w_out=w_out, mem_q_norm=mem_q_norm, mem_kv_norm=mem_kv_norm, mem_wq=mem_wq,
                  mem_wkv=mem_wkv, mem_q_gain=mem_q_gain, mem_k_gain=mem_k_gain, mem_wo=mem_wo, ffn2_norm=ffn2_norm,
                  ffn2_w_gu=ffn2_w_gu, ffn2_w_down=ffn2_w_down)
    depth = w_in.shape[0]
    n_seq, s_len, d = x_prompt.shape
    n_dec, dec_len, _ = x_sample.shape
    assert dec_len == 1
    past = page_table.shape[1] * PAGE_SIZE
    tables_p = _rope_tables(jnp.arange(s_len))
    tables_s = _rope_tables(jnp.full((n_dec,), past))
    caches = (cache_cmp_kv, cache_sel_kv, cache_win_kv, cache_pool, state_gla, cache_mem_kv)
    xp = x_prompt.reshape(n_seq * s_len, d)
    xs = x_sample.reshape(n_dec, d)
    mem = mem_prompt.reshape(-1, d)
    for name in ('ffn1_w_gu', 'ffn1_w_down', 'ffn2_w_gu', 'ffn2_w_down'):
        params[name + '_b'] = params[name].astype(BF16)
    p_outs, s_outs = [], []
    for l in range(depth):
        w = _layer_weights(params, l)
        xp, po = _prompt_layer(xp, mem, w, tables_p, n_seq=n_seq, s_len=s_len, tm=256)
        xs, so = _sample_layer(xs, w, tables_s, caches, page_table, l)
        p_outs.append(po)
        s_outs.append(so)
    stack = lambda outs, k: jnp.stack([o[k] for o in outs])
    return ((xp.reshape(n_seq, s_len, d), xs.reshape(n_dec, 1, d))
            + tuple(stack(p_outs, k) for k in range(6)) + tuple(stack(s_outs, k) for k in range(5)))
```

```python
import functools

import jax
import jax.numpy as jnp
import numpy as np
from jax import lax
from jax.experimental import pallas as pl
from jax.experimental.pallas import tpu as pltpu

F32 = jnp.float32
BF16 = jnp.bfloat16

D_MODEL = 2048
HEAD_DIM = 128
ROPE_THETA = 10000.0
EPS = 1e-6
NEG = -1e30
FORCE = 1e4
SCALE = HEAD_DIM ** -0.5
PAGE_SIZE = 128
NSA_HEADS = 8
NSA_KV_HEADS = 2
NSA_HPG = NSA_HEADS // NSA_KV_HEADS
NSA_Q_W = NSA_HEADS * HEAD_DIM
NSA_KV_W = NSA_KV_HEADS * HEAD_DIM
CMP_BLOCK = 64
SEL_BLOCK = 64
SEL_TOPK = 16
WINDOW = 512
POOL_WIDTH = 512
POOL_WINDOWS = (2, 4, 8, 16)
POOL_GROUP = POOL_WIDTH // len(POOL_WINDOWS)
POOL_BUF = max(POOL_WINDOWS) - 1
POOL_HIST = 16
GLA_WIDTH = 512
GLA_HEADS = 4
GLA_DK = 64
GLA_DV = 128
GLA_RANK = 16
GLA_TAU = 16.0
GLA_CHUNK = 64
GLA_SUB = 16
GLA_QK = GLA_HEADS * GLA_DK
MEM_HEADS = 4
MEM_WIDTH = MEM_HEADS * HEAD_DIM
LANE = 128
PAD_SCORE = -3.0e38

_C_Q = 0
_C_KV = _C_Q + NSA_Q_W
_C_GT = _C_KV + 6 * NSA_KV_W
_C_U = _C_GT + 3 * NSA_HEADS
_C_GQ = _C_U + POOL_WIDTH
_C_GK = _C_GQ + GLA_QK
_C_GV = _C_GK + GLA_QK
_C_GA = _C_GV + GLA_WIDTH
_C_GR = _C_GA + GLA_RANK
_C_END = _C_GR + GLA_WIDTH

VMEM_LIMIT = 48 * 1024 * 1024


def _cparams(sem):
    return pltpu.CompilerParams(dimension_semantics=sem, vmem_limit_bytes=VMEM_LIMIT)


def _dot(a, b):
    return jnp.dot(a, b, preferred_element_type=F32)


def _dot_nt(a, b):
    return lax.dot_general(a, b, (((1,), (1,)), ((), ())), preferred_element_type=F32)


def _rms(x, g):
    return x * lax.rsqrt(jnp.mean(x * x, axis=-1, keepdims=True) + EPS) * g


def _sigmoid(x):
    return 1.0 / (1.0 + jnp.exp(-x))


def _silu(x):
    return x * _sigmoid(x)


def _log_sigmoid(x):
    return jnp.minimum(x, 0.0) - jnp.log1p(jnp.exp(-jnp.abs(x)))


def _rope(y, cos, sin):
    return y * cos + pltpu.roll(y, HEAD_DIM // 2, axis=1) * sin


def _pad_cols(w, width):
    return jnp.pad(w, ((0, 0), (0, width - w.shape[1])))


def _ffn_body(x_ref, g_ref, wg_ref, wu_ref, wd_ref, o_ref, xn_ref):
    j = pl.program_id(1)

    @pl.when(j == 0)
    def _():
        xn_ref[...] = _rms(x_ref[...], g_ref[...]).astype(BF16)
        o_ref[...] = jnp.zeros_like(o_ref)

    xn = xn_ref[...]
    a = _dot(xn, wg_ref[...])
    b = _dot(xn, wu_ref[...])
    h = (_silu(a) * b).astype(BF16)
    o_ref[...] += _dot(h, wd_ref[...])

    @pl.when(j == pl.num_programs(1) - 1)
    def _():
        o_ref[...] = x_ref[...] + 0.5 * o_ref[...]


def _ffn(x, g, w_gu, w_down, layer, *, tm, tf=512):
    m, d = x.shape
    f = w_down.shape[1]
    nf = f // tf
    return pl.pallas_call(
        _ffn_body,
        out_shape=jax.ShapeDtypeStruct((m, d), F32),
        grid=(m // tm, nf),
        in_specs=[
            pl.BlockSpec((tm, d), lambda i, j: (i, 0)),
            pl.BlockSpec((1, d), lambda i, j: (0, 0)),
            pl.BlockSpec((None, d, tf), lambda i, j: (layer, 0, j)),
            pl.BlockSpec((None, d, tf), lambda i, j: (layer, 0, j + nf)),
            pl.BlockSpec((None, tf, d), lambda i, j: (layer, j, 0)),
        ],
        out_specs=pl.BlockSpec((tm, d), lambda i, j: (i, 0)),
        scratch_shapes=[pltpu.VMEM((tm, d), BF16)],
        compiler_params=_cparams(("parallel", "arbitrary")),
        name="ffn_half_step",
    )(x, g, w_gu, w_gu, w_down)


KV_ROWS = 2 * NSA_KV_HEADS


def _qkv_body(x_ref, g_ref, w_ref, cos_ref, sin_ref, qg_ref, kg_ref, gb_ref,
              q_ref, kvb_ref, cmp_ref, sel_ref, win_ref, gt_ref):
    tm = x_ref.shape[0]
    xn = _rms(x_ref[...], g_ref[...]).astype(BF16)
    h = _dot(xn, w_ref[...])
    cos = cos_ref[...]
    sin = sin_ref[...]

    def head(c0, gain):
        return _rope(_rms(h[:, c0:c0 + HEAD_DIM], gain), cos, sin)

    for hh in range(NSA_HEADS):
        c0 = _C_Q + hh * HEAD_DIM
        q_ref[:, hh * HEAD_DIM:(hh + 1) * HEAD_DIM] = head(c0, qg_ref[...]).astype(BF16)
    for br, out in enumerate((cmp_ref, sel_ref, win_ref)):
        base = _C_KV + br * 2 * NSA_KV_W
        for r in range(KV_ROWS):
            c0 = base + r * HEAD_DIM
            slab = head(c0, kg_ref[br:br + 1, :]) if r < NSA_KV_HEADS else h[:, c0:c0 + HEAD_DIM]
            out[pl.ds(r, tm, stride=KV_ROWS), :] = slab
            kvb_ref[:, c0 - _C_KV:c0 - _C_KV + HEAD_DIM] = slab.astype(BF16)
    gt_ref[...] = _sigmoid(h[:, _C_GT:_C_GT + 2 * LANE] + gb_ref[...])


def _qkv_proj(x, g, w, cos, sin, q_gain, k_gain, gate_b, *, tm, tiles_per_seq):
    m, d = x.shape
    n = w.shape[1]
    row = lambda i: (i, 0)
    fixed = lambda i: (0, 0)
    pos = lambda i: (i % tiles_per_seq, 0)
    return pl.pallas_call(
        _qkv_body,
        out_shape=(
            jax.ShapeDtypeStruct((m, NSA_Q_W), BF16),
            jax.ShapeDtypeStruct((m, 6 * NSA_KV_W), BF16),
            jax.ShapeDtypeStruct((m * KV_ROWS, HEAD_DIM), F32),
            jax.ShapeDtypeStruct((m * KV_ROWS, HEAD_DIM), F32),
            jax.ShapeDtypeStruct((m * KV_ROWS, HEAD_DIM), F32),
            jax.ShapeDtypeStruct((m, 2 * LANE), F32),
        ),
        grid=(m // tm,),
        in_specs=[
            pl.BlockSpec((tm, d), row),
            pl.BlockSpec((1, d), fixed),
            pl.BlockSpec((d, n), fixed),
            pl.BlockSpec((tm, HEAD_DIM), pos),
            pl.BlockSpec((tm, HEAD_DIM), pos),
            pl.BlockSpec((1, HEAD_DIM), fixed),
            pl.BlockSpec((3, HEAD_DIM), fixed),
            pl.BlockSpec((1, 2 * LANE), fixed),
        ],
        out_specs=(
            pl.BlockSpec((tm, NSA_Q_W), row),
            pl.BlockSpec((tm, 6 * NSA_KV_W), row),
            pl.BlockSpec((tm * KV_ROWS, HEAD_DIM), row),
            pl.BlockSpec((tm * KV_ROWS, HEAD_DIM), row),
            pl.BlockSpec((tm * KV_ROWS, HEAD_DIM), row),
            pl.BlockSpec((tm, 2 * LANE), row),
        ),
        compiler_params=_cparams(("parallel",)),
        name="nsa_projection",
    )(x, g, w, cos, sin, q_gain, k_gain, gate_b)


def _softmax_av(s, v):
    e = jnp.exp(s - jnp.max(s, axis=-1, keepdims=True))
    return _dot(e.astype(BF16), v) / jnp.sum(e, axis=-1, keepdims=True)


EXP2_SCALE = SCALE * float(np.log2(np.e))
NSA_PREFIXES = 8


def _nsa_prompt_body(q_ref, gt_ref, kc_ref, vc_ref, ksb_ref, vsb_ref, kwb_ref, vwb_ref, wk_ref, wv_ref, o_ref,
                     kcb_ref, vcb_ref, acc_ref):
    i = pl.program_id(2)
    tq = q_ref.shape[0]
    s_len = kc_ref.shape[0]
    nb = s_len // CMP_BLOCK

    @pl.when(i == 0)
    def _():
        kcf = kc_ref[...].astype(F32).reshape(nb, CMP_BLOCK, HEAD_DIM)
        vcf = vc_ref[...].astype(F32).reshape(nb, CMP_BLOCK, HEAD_DIM)
        kcb_ref[...] = jnp.sum(kcf * wk_ref[...][None], axis=1).astype(BF16)
        vcb_ref[...] = jnp.sum(vcf * wv_ref[...][None], axis=1).astype(BF16)

    qpos = i * tq + lax.broadcasted_iota(jnp.int32, (tq, 1), 0)
    qpos_t = i * tq + lax.broadcasted_iota(jnp.int32, (1, tq), 1)
    kc = kcb_ref[...]
    vc = vcb_ref[...]
    blk = lax.broadcasted_iota(jnp.int32, (1, nb), 1)
    blk_t = lax.broadcasted_iota(jnp.int32, (nb, 1), 0)
    valid_c = ((blk + 1) * CMP_BLOCK - 1) <= qpos
    valid_t = ((blk_t + 1) * CMP_BLOCK - 1) <= qpos_t
    imp_t = jnp.zeros((nb, tq), F32)
    o_c = []
    for hh in range(NSA_HPG):
        qh = q_ref[:, hh * HEAD_DIM:(hh + 1) * HEAD_DIM]
        s = jnp.where(valid_c, _dot_nt(qh, kc) * SCALE, NEG)
        e = jnp.where(valid_c, jnp.exp(s - jnp.max(s, axis=-1, keepdims=True)), 0.0)
        p = e / jnp.maximum(jnp.sum(e, axis=-1, keepdims=True), 1e-30)
        o_c.append(_dot(p.astype(BF16), vc))
        s_t = jnp.where(valid_t, _dot_nt(kc, qh) * SCALE, NEG)
        e_t = jnp.where(valid_t, jnp.exp(s_t - jnp.max(s_t, axis=0, keepdims=True)), 0.0)
        imp_t = imp_t + e_t / jnp.maximum(jnp.sum(e_t, axis=0, keepdims=True), 1e-30)

    cur_t = qpos_t // SEL_BLOCK
    forced = jnp.where(blk_t == cur_t, 1, jnp.where(blk_t == cur_t - 1, 1, jnp.where(blk_t == 0, 1, 0)))
    score = jnp.where(blk_t > cur_t, -FORCE, jnp.where(forced > 0, FORCE, imp_t))
    rank = jnp.zeros((nb, tq), jnp.int32)
    for b2 in range(nb):
        other = score[b2:b2 + 1, :]
        tie = jnp.where(other == score, jnp.where(blk_t > b2, 1, 0), 0)
        rank = rank + jnp.where(other > score, 1, tie)
    keep = jnp.where(rank < min(SEL_TOPK, nb), 1.0, 0.0).T.astype(BF16)

    nt = s_len // tq
    n_prefix = min(NSA_PREFIXES, nt)
    tiles_per_prefix = nt // n_prefix
    for k in range(n_prefix):
        @pl.when(i // tiles_per_prefix == k)
        def _(k=k):
            width = (k + 1) * tiles_per_prefix * tq
            expand = jnp.where(
                lax.broadcasted_iota(jnp.int32, (nb, width), 1) // SEL_BLOCK
                == lax.broadcasted_iota(jnp.int32, (nb, width), 0), 1.0, 0.0).astype(BF16)
            keep_keys = _dot(keep, expand)
            kpos = lax.broadcasted_iota(jnp.int32, (1, width), 1)
            bias = jnp.where(keep_keys > 0.5, jnp.where(kpos <= qpos, 0.0, NEG), NEG)
            ks = ksb_ref[0:width, :]
            vs = vsb_ref[0:width, :]
            for hh in range(NSA_HPG):
                s = _dot_nt(q_ref[:, hh * HEAD_DIM:(hh + 1) * HEAD_DIM], ks) + bias
                e = jnp.exp2((s - jnp.max(s, axis=-1, keepdims=True)) * EXP2_SCALE)
                acc_ref[hh] = _dot(e.astype(BF16), vs) / jnp.sum(e, axis=-1, keepdims=True)

    span = WINDOW + tq
    kstart = pl.multiple_of(jnp.maximum(i * tq - WINDOW, 0), tq)
    kw = kwb_ref[pl.ds(kstart, span), :]
    vw = vwb_ref[pl.ds(kstart, span), :]
    rel = qpos - (kstart + lax.broadcasted_iota(jnp.int32, (1, span), 1))
    bias_w = jnp.where(rel >= 0, jnp.where(rel <= WINDOW, 0.0, NEG), NEG)

    gt = gt_ref[...]
    for hh in range(NSA_HPG):
        s = _dot_nt(q_ref[:, hh * HEAD_DIM:(hh + 1) * HEAD_DIM], kw) + bias_w
        e = jnp.exp2((s - jnp.max(s, axis=-1, keepdims=True)) * EXP2_SCALE)
        o_w = _dot(e.astype(BF16), vw) / jnp.sum(e, axis=-1, keepdims=True)
        o = (gt[:, 3 * hh:3 * hh + 1] * o_c[hh] + gt[:, 3 * hh + 1:3 * hh + 2] * acc_ref[hh]
             + gt[:, 3 * hh + 2:3 * hh + 3] * o_w)
        o_ref[:, hh * HEAD_DIM:(hh + 1) * HEAD_DIM] = o.astype(BF16)


def _nsa_prompt(q, gates, kv, wk, wv, *, n_seq, s_len, tq):
    m = q.shape[0]
    nt = s_len // tq
    nb = s_len // CMP_BLOCK
    assert s_len >= WINDOW + tq and s_len % tq == 0 and tq % SEL_BLOCK == 0 and nt % min(NSA_PREFIXES, nt) == 0
    gw = NSA_HPG * HEAD_DIM
    row = lambda n, g, i: (n * nt + i, g)

    def kv_spec(branch, is_value):
        col = branch * KV_ROWS + is_value * NSA_KV_HEADS
        return pl.BlockSpec((s_len, HEAD_DIM), lambda n, g, i: (n, col + g))

    kv_specs = [kv_spec(br, v) for br in range(3) for v in range(2)]
    wspec = pl.BlockSpec((None, CMP_BLOCK, HEAD_DIM), lambda n, g, i: (g, 0, 0))
    return pl.pallas_call(
        _nsa_prompt_body,
        out_shape=jax.ShapeDtypeStruct((m, NSA_Q_W), BF16),
        grid=(n_seq, NSA_KV_HEADS, nt),
        in_specs=[
            pl.BlockSpec((tq, gw), row),
            pl.BlockSpec((tq, LANE), row),
            *kv_specs, wspec, wspec,
        ],
        out_specs=pl.BlockSpec((tq, gw), row),
        scratch_shapes=[pltpu.VMEM((nb, HEAD_DIM), BF16), pltpu.VMEM((nb, HEAD_DIM), BF16),
                        pltpu.VMEM((NSA_HPG, tq, HEAD_DIM), F32)],
        compiler_params=_cparams(("parallel", "parallel", "arbitrary")),
        name="nsa_prompt",
    )(q, gates, *([kv] * len(kv_specs)), wk, wv)


def _pool_prompt_body(x_ref, g_ref, w_ref, pw_ref, ps_ref, u_ref, y_ref, hist_ref):
    j = pl.program_id(1)
    tm = x_ref.shape[0]

    @pl.when(j == 0)
    def _():
        hist_ref[...] = jnp.zeros_like(hist_ref)

    u = _dot(_rms(x_ref[...], g_ref[...]).astype(BF16), w_ref[...])
    u_ref[...] = u
    ext = jnp.concatenate([hist_ref[...], u], axis=0)
    t = j * tm + lax.broadcasted_iota(jnp.int32, (tm, 1), 0)
    for gi, win in enumerate(POOL_WINDOWS):
        c0, c1 = gi * POOL_GROUP, (gi + 1) * POOL_GROUP
        acc = ext[:, c0:c1]
        span = 1
        while span < win:
            acc = acc + pltpu.roll(acc, span, axis=0)
            span *= 2
        mean = acc[POOL_HIST:] / jnp.minimum(t + 1, win).astype(F32)
        y = _dot((mean - u[:, c0:c1]).astype(BF16), pw_ref[gi].astype(BF16))
        y_ref[:, c0:c1] = (y * ps_ref[:, c0:c1]).astype(BF16)
    hist_ref[...] = u[tm - POOL_HIST:]


def _pool_prompt(x, g, w_u, pool_w, pool_scale, *, n_seq, s_len, tm):
    d = x.shape[1]
    nt = s_len // tm
    row = lambda n, j: (n * nt + j, 0)
    fixed = lambda n, j: (0, 0)
    return pl.pallas_call(
        _pool_prompt_body,
        out_shape=(jax.ShapeDtypeStruct((n_seq * s_len, POOL_WIDTH), F32),
                   jax.ShapeDtypeStruct((n_seq * s_len, POOL_WIDTH), BF16)),
        grid=(n_seq, nt),
        in_specs=[
            pl.BlockSpec((tm, d), row),
            pl.BlockSpec((1, d), fixed),
            pl.BlockSpec((d, POOL_WIDTH), fixed),
            pl.BlockSpec((len(POOL_WINDOWS), POOL_GROUP, POOL_GROUP), lambda n, j: (0, 0, 0)),
            pl.BlockSpec((1, POOL_WIDTH), fixed),
        ],
        out_specs=(pl.BlockSpec((tm, POOL_WIDTH), row), pl.BlockSpec((tm, POOL_WIDTH), row)),
        scratch_shapes=[pltpu.VMEM((POOL_HIST, POOL_WIDTH), F32)],
        compiler_params=_cparams(("parallel", "arbitrary")),
        name="pool_prompt",
    )(x, g, w_u, pool_w, pool_scale)


def _gla_prompt_body(x_ref, g_ref, w_ref, wa2_ref, ba_ref, gain_ref, og_ref, st_ref, state_ref):
    j = pl.program_id(1)
    tm = x_ref.shape[0]

    @pl.when(j == 0)
    def _():
        state_ref[...] = jnp.zeros_like(state_ref)

    h = _dot(_rms(x_ref[...], g_ref[...]).astype(BF16), w_ref[...])
    c_k, c_v, c_r, c_a = GLA_QK, 2 * GLA_QK, 2 * GLA_QK + GLA_WIDTH, 2 * GLA_QK + 2 * GLA_WIDTH
    z = _dot(h[:, c_a:c_a + LANE].astype(BF16), wa2_ref[...]) + ba_ref[...]
    la_all = _log_sigmoid(z) / GLA_TAU

    lane = lax.broadcasted_iota(jnp.int32, (1, GLA_QK), 1)
    head_mask = [jnp.where(lane // GLA_DK == hd, 1.0, 0.0) for hd in range(GLA_HEADS)]
    rows = lax.broadcasted_iota(jnp.int32, (GLA_CHUNK, 1), 0)
    causal = rows >= lax.broadcasted_iota(jnp.int32, (1, GLA_CHUNK), 1)
    n_sub = GLA_CHUNK // GLA_SUB

    for c in range(tm // GLA_CHUNK):
        r0, r1 = c * GLA_CHUNK, (c + 1) * GLA_CHUNK
        q = h[r0:r1, 0:GLA_QK] * (GLA_DK ** -0.5)
        k = h[r0:r1, c_k:c_k + GLA_QK]
        b = la_all[r0:r1]
        sh = 1
        while sh < GLA_CHUNK:
            b = b + jnp.where(rows >= sh, pltpu.roll(b, sh, axis=0), 0.0)
            sh *= 2
        state = state_ref[...]
        state_b = state.astype(BF16)
        q_in = q * jnp.exp(b)
        a_parts = [[] for _ in range(GLA_HEADS)]
        for sb in range(n_sub):
            s0, s1 = sb * GLA_SUB, (sb + 1) * GLA_SUB
            ref_row = b[s0 - 1:s0] if sb > 0 else jnp.zeros((1, GLA_QK), F32)
            q_t = q[s0:s1] * jnp.exp(b[s0:s1] - ref_row)
            k_t = (k * jnp.exp(jnp.where(rows < s1, ref_row - b, 0.0))).astype(BF16)
            for hd in range(GLA_HEADS):
                a_parts[hd].append(_dot_nt((q_t * head_mask[hd]).astype(BF16), k_t))
        b_last = b[GLA_CHUNK - 1:GLA_CHUNK]
        k_out = k * jnp.exp(b_last - b)
        new_state = state * jnp.exp(b_last)
        for hd in range(GLA_HEADS):
            v = h[r0:r1, c_v + hd * GLA_DV:c_v + (hd + 1) * GLA_DV]
            vb = v.astype(BF16)
            a = jnp.where(causal, jnp.concatenate(a_parts[hd], axis=0), 0.0)
            o = _dot(a.astype(BF16), vb) + _dot_nt((q_in * head_mask[hd]).astype(BF16), state_b)
            gr = h[r0:r1, c_r + hd * GLA_DV:c_r + (hd + 1) * GLA_DV]
            og = _rms(o, gain_ref[:, hd * GLA_DV:(hd + 1) * GLA_DV]) * _silu(gr)
            og_ref[r0:r1, hd * GLA_DV:(hd + 1) * GLA_DV] = og.astype(BF16)
            new_state = new_state + _dot(v.T.astype(BF16), (k_out * head_mask[hd]).astype(BF16))
        state_ref[...] = new_state

    @pl.when(j == pl.num_programs(1) - 1)
    def _():
        st_ref[...] = state_ref[...]


def _gla_prompt(x, g, w_gla, wa2, ba, gain, *, n_seq, s_len, tm):
    d = x.shape[1]
    nt = s_len // tm
    row = lambda n, j: (n * nt + j, 0)
    fixed = lambda n, j: (0, 0)
    return pl.pallas_call(
        _gla_prompt_body,
        out_shape=(jax.ShapeDtypeStruct((n_seq * s_len, GLA_WIDTH), BF16),
                   jax.ShapeDtypeStruct((n_seq, GLA_DV, GLA_QK), F32)),
        grid=(n_seq, nt),
        in_specs=[
            pl.BlockSpec((tm, d), row),
            pl.BlockSpec((1, d), fixed),
            pl.BlockSpec((d, w_gla.shape[1]), fixed),
            pl.BlockSpec((LANE, GLA_QK), fixed),
            pl.BlockSpec((1, GLA_QK), fixed),
            pl.BlockSpec((1, GLA_WIDTH), fixed),
        ],
        out_specs=(pl.BlockSpec((tm, GLA_WIDTH), row),
                   pl.BlockSpec((None, GLA_DV, GLA_QK), lambda n, j: (n, 0, 0))),
        scratch_shapes=[pltpu.VMEM((GLA_DV, GLA_QK), F32)],
        compiler_params=_cparams(("parallel", "arbitrary")),
        name="gla_prompt",
    )(x, g, w_gla, wa2, ba, gain)


def _norm_matmul_body(x_ref, g_ref, w_ref, hg_ref, o_ref, *, n_norm_heads):
    h = _dot(_rms(x_ref[...], g_ref[...]).astype(BF16), w_ref[...])
    if n_norm_heads:
        for hd in range(n_norm_heads):
            c0 = hd * HEAD_DIM
            o_ref[:, c0:c0 + HEAD_DIM] = _rms(h[:, c0:c0 + HEAD_DIM], hg_ref[...])
        c0 = n_norm_heads * HEAD_DIM
        o_ref[:, c0:] = h[:, c0:]
    else:
        o_ref[...] = h


def _norm_matmul(x, g, w, head_gain, *, tm, n_norm_heads=0):
    m, d = x.shape
    n = w.shape[1]
    return pl.pallas_call(
        functools.partial(_norm_matmul_body, n_norm_heads=n_norm_heads),
        out_shape=jax.ShapeDtypeStruct((m, n), F32),
        grid=(m // tm,),
        in_specs=[
            pl.BlockSpec((tm, d), lambda i: (i, 0)),
            pl.BlockSpec((1, d), lambda i: (0, 0)),
            pl.BlockSpec((d, n), lambda i: (0, 0)),
            pl.BlockSpec((1, HEAD_DIM), lambda i: (0, 0)),
        ],
        out_specs=pl.BlockSpec((tm, n), lambda i: (i, 0)),
        compiler_params=_cparams(("parallel",)),
        name="norm_matmul",
    )(x, g, w, head_gain)


def _post_body(x_ref, on_ref, yp_ref, og_ref, wout_ref, mk_ref, mv_ref, qn_ref, wq_ref, qg_ref, wo_ref, o_ref):
    c1, c2 = NSA_Q_W, NSA_Q_W + POOL_WIDTH
    x1 = (x_ref[...] + _dot(on_ref[...], wout_ref[0:c1, :]) + _dot(yp_ref[...], wout_ref[c1:c2, :])
          + _dot(og_ref[...], wout_ref[c2:, :]))
    q = _dot(_rms(x1, qn_ref[...]).astype(BF16), wq_ref[...])
    outs = []
    for hd in range(MEM_HEADS):
        c0 = hd * HEAD_DIM
        qh = _rms(q[:, c0:c0 + HEAD_DIM], qg_ref[...]).astype(BF16)
        s = _dot_nt(qh, mk_ref[:, c0:c0 + HEAD_DIM].astype(BF16)) * SCALE
        outs.append(_softmax_av(s, mv_ref[:, c0:c0 + HEAD_DIM].astype(BF16)).astype(BF16))
    o_ref[...] = x1 + _dot(jnp.concatenate(outs, axis=1), wo_ref[...])


def _post_mix(x, o_nsa, y_pool, o_gla, w_out, mem_kv, q_norm, wq, q_gain, wo, *, n_seq, t_len, tm):
    d = x.shape[1]
    nt = t_len // tm
    mt = mem_kv.shape[1]
    row = lambda n, j: (n * nt + j, 0)
    fixed = lambda n, j: (0, 0)
    return pl.pallas_call(
        _post_body,
        out_shape=jax.ShapeDtypeStruct(x.shape, F32),
        grid=(n_seq, nt),
        in_specs=[
            pl.BlockSpec((tm, d), row),
            pl.BlockSpec((tm, NSA_Q_W), row),
            pl.BlockSpec((tm, POOL_WIDTH), row),
            pl.BlockSpec((tm, GLA_WIDTH), row),
            pl.BlockSpec(w_out.shape, fixed),
            pl.BlockSpec((None, mt, MEM_WIDTH), lambda n, j: (n, 0, 0)),
            pl.BlockSpec((None, mt, MEM_WIDTH), lambda n, j: (n, 0, 1)),
            pl.BlockSpec((1, d), fixed),
            pl.BlockSpec((d, MEM_WIDTH), fixed),
            pl.BlockSpec((1, HEAD_DIM), fixed),
            pl.BlockSpec((MEM_WIDTH, d), fixed),
        ],
        out_specs=pl.BlockSpec((tm, d), row),
        compiler_params=_cparams(("parallel", "parallel")),
        name="mix_out_mem_attn",
    )(x, o_nsa, y_pool, o_gla, w_out, mem_kv, mem_kv, q_norm, wq, q_gain, wo)


PAGES_PER_STEP = 16


SUBLANES = 8


def _cmp_pages_body(pt_ref, *refs):
    del pt_ref
    pages, w_ref, o_ref = refs[:PAGES_PER_STEP], refs[PAGES_PER_STEP], refs[PAGES_PER_STEP + 1]
    w = w_ref[...]
    blk_rows = CMP_BLOCK * KV_ROWS
    sums = []
    for page in pages:
        for half in range(PAGE_SIZE // CMP_BLOCK):
            prod = page[half * blk_rows:(half + 1) * blk_rows, :] * w
            part = jnp.sum(prod.reshape(blk_rows // SUBLANES, SUBLANES, HEAD_DIM), axis=0)
            sums.append(part[0:KV_ROWS] + part[KV_ROWS:2 * KV_ROWS])
    for r in range(KV_ROWS):
        o_ref[r] = jnp.concatenate([s[r:r + 1] for s in sums], axis=0)


def _cmp_pages(cache, page_table, w_cmp, layer):
    n_seq, n_pages = page_table.shape
    page_rows = cache.shape[2]
    bpp = PAGE_SIZE // CMP_BLOCK
    assert n_pages % PAGES_PER_STEP == 0 and 2 * KV_ROWS == SUBLANES

    def page_spec(k):
        return pl.BlockSpec((None, None, page_rows, HEAD_DIM),
                            lambda n, j, pt: (layer, pt[n, j * PAGES_PER_STEP + k], 0, 0))

    return pl.pallas_call(
        _cmp_pages_body,
        out_shape=jax.ShapeDtypeStruct((n_seq, KV_ROWS, n_pages * bpp, HEAD_DIM), F32),
        grid_spec=pltpu.PrefetchScalarGridSpec(
            num_scalar_prefetch=1,
            grid=(n_seq, n_pages // PAGES_PER_STEP),
            in_specs=[page_spec(k) for k in range(PAGES_PER_STEP)]
            + [pl.BlockSpec((CMP_BLOCK * KV_ROWS, HEAD_DIM), lambda n, j, pt: (0, 0))],
            out_specs=pl.BlockSpec((None, KV_ROWS, PAGES_PER_STEP * bpp, HEAD_DIM), lambda n, j, pt: (n, 0, j, 0)),
        ),
        compiler_params=_cparams(("parallel", "arbitrary")),
        name="cmp_pages",
    )(page_table, *([cache] * PAGES_PER_STEP), w_cmp)


def _cmp_select_body(q_ref, c_ref, oc_ref, idx_ref, *, n_past_blocks):
    n_seq = q_ref.shape[0]
    nbp = n_past_blocks
    width = idx_ref.shape[-1]
    cur = nbp
    q = q_ref[...]
    lane = lax.broadcasted_iota(jnp.int32, (n_seq, width), 1)
    lane_f = lane.astype(F32)
    row_head = lax.broadcasted_iota(jnp.int32, (1, NSA_HEADS, 1), 1)
    o_c = jnp.zeros((n_seq, NSA_HEADS, HEAD_DIM), F32)
    for g in range(NSA_KV_HEADS):
        kc = c_ref[:, g].astype(BF16)
        vc = c_ref[:, NSA_KV_HEADS + g].astype(BF16)
        s = lax.dot_general(q, kc, (((2,), (2,)), ((0,), (0,))), preferred_element_type=F32) * SCALE
        e = jnp.exp(s - jnp.max(s, axis=-1, keepdims=True))
        p = e / jnp.maximum(jnp.sum(e, axis=-1, keepdims=True), 1e-30)
        o_g = lax.dot_general(p.astype(BF16), vc, (((2,), (1,)), ((0,), (0,))), preferred_element_type=F32)
        in_group = row_head // NSA_HPG == g
        o_c = jnp.where(in_group, o_g, o_c)
        imp = jnp.sum(jnp.where(in_group, p, 0.0), axis=1)
        imp = jnp.concatenate([imp, jnp.zeros((n_seq, width - nbp), F32)], axis=1)
        forced = jnp.where(lane == cur, 1, jnp.where(lane == cur - 1, 1, jnp.where(lane == 0, 1, 0)))
        score = jnp.where(lane > cur, PAD_SCORE, jnp.where(forced > 0, FORCE, imp))
        picks = jnp.zeros((n_seq, width), F32)
        for it in range(SEL_TOPK):
            top = jnp.max(score, axis=-1, keepdims=True)
            first = jnp.min(jnp.where(score == top, lane_f, float(width)), axis=-1, keepdims=True)
            picks = jnp.where(lane == it, first, picks)
            score = jnp.where(lane_f == first, PAD_SCORE, score)
        idx_ref[g] = picks.astype(jnp.int32)
    oc_ref[...] = o_c


def _cmp_select(q3, cmpc):
    n_seq, _, nbp, _ = cmpc.shape
    assert nbp + 1 >= SEL_TOPK and nbp % LANE == 0
    idx_w = nbp + LANE
    return pl.pallas_call(
        functools.partial(_cmp_select_body, n_past_blocks=nbp),
        out_shape=(jax.ShapeDtypeStruct((n_seq, NSA_HEADS, HEAD_DIM), F32),
                   jax.ShapeDtypeStruct((NSA_KV_HEADS, n_seq, idx_w), jnp.int32)),
        grid=(1,),
        in_specs=[
            pl.BlockSpec((n_seq, NSA_HEADS, HEAD_DIM), lambda i: (0, 0, 0)),
            pl.BlockSpec(cmpc.shape, lambda i: (0, 0, 0, 0)),
        ],
        out_specs=(pl.BlockSpec((n_seq, NSA_HEADS, HEAD_DIM), lambda i: (0, 0, 0)),
                   pl.BlockSpec((NSA_KV_HEADS, n_seq, idx_w), lambda i: (0, 0, 0))),
        compiler_params=_cparams(("arbitrary",)),
        name="cmp_select",
    )(q3, cmpc)


def _one_token_attn(q, rows, g, k_new, v_new, extra_bias):
    n_rows = rows.shape[0]
    kind = lax.broadcasted_iota(jnp.int32, (1, n_rows), 1) % KV_ROWS
    s = _dot_nt(q, rows) * SCALE + jnp.where(kind == g, 0.0, NEG)
    if extra_bias is not None:
        s = s + extra_bias
    s_new = jnp.sum(q.astype(F32) * k_new, axis=-1, keepdims=True) * SCALE
    top = jnp.maximum(jnp.max(s, axis=-1, keepdims=True), s_new)
    e = jnp.exp(s - top)
    e_new = jnp.exp(s_new - top)
    e_val = pltpu.roll(e, NSA_KV_HEADS, axis=1)
    return (_dot(e_val.astype(BF16), rows) + e_new * v_new) / (jnp.sum(e, axis=-1, keepdims=True) + e_new)


def _sel_win_body(blk_ref, new_ref, *refs):
    del blk_ref
    n = pl.program_id(0)
    nk = SEL_TOPK
    n_sel = NSA_KV_HEADS * nk
    sel_refs = refs[:n_sel]
    win_ref, q_ref, sn_ref, wn_ref, oc_ref, gt_ref, o_ref = refs[n_sel:]
    q = q_ref[...]
    sel_new = sn_ref[...]
    win_new = wn_ref[...]
    gt = gt_ref[...]
    row_head = lax.broadcasted_iota(jnp.int32, (NSA_HEADS, 1), 0)
    blk_rows = SEL_BLOCK * KV_ROWS
    key_blk = lax.broadcasted_iota(jnp.int32, (1, nk * blk_rows), 1) // blk_rows
    win_rows = win_ref[...].astype(BF16)
    o_s = jnp.zeros((NSA_HEADS, HEAD_DIM), F32)
    o_w = jnp.zeros((NSA_HEADS, HEAD_DIM), F32)
    for g in range(NSA_KV_HEADS):
        k0, v0 = g * HEAD_DIM, NSA_KV_W + g * HEAD_DIM
        rows = jnp.concatenate([r[...] for r in sel_refs[g * nk:(g + 1) * nk]], axis=0).astype(BF16)
        bias = jnp.where(key_blk == new_ref[n * NSA_KV_HEADS + g], NEG, 0.0)
        o_sg = _one_token_attn(q, rows, g, sel_new[:, k0:k0 + HEAD_DIM], sel_new[:, v0:v0 + HEAD_DIM], bias)
        o_wg = _one_token_attn(q, win_rows, g, win_new[:, k0:k0 + HEAD_DIM], win_new[:, v0:v0 + HEAD_DIM], None)
        in_group = row_head // NSA_HPG == g
        o_s = jnp.where(in_group, o_sg, o_s)
        o_w = jnp.where(in_group, o_wg, o_w)
    o_ref[...] = (gt[:, 0:1] * oc_ref[...] + gt[:, 1:2] * o_s + gt[:, 2:3] * o_w).astype(BF16)


def _sel_win(blk_rows, new_slot, cache_sel, cache_win, q3, sel_new, win_new, o_c, gates3, layer):
    n_seq = q3.shape[0]
    nk = SEL_TOPK

    def sel_spec(g, k):
        return pl.BlockSpec((None, None, SEL_BLOCK * KV_ROWS, HEAD_DIM),
                            lambda n, br, ns: (layer, br[(n * NSA_KV_HEADS + g) * nk + k], 0, 0))

    sel_specs = [sel_spec(g, k) for g in range(NSA_KV_HEADS) for k in range(nk)]
    win_specs = [pl.BlockSpec((None, None, cache_win.shape[2], HEAD_DIM), lambda n, br, ns: (layer, n, 0, 0))]
    tok = lambda n, br, ns: (n, 0, 0)
    return pl.pallas_call(
        _sel_win_body,
        out_shape=jax.ShapeDtypeStruct((n_seq, NSA_HEADS, HEAD_DIM), BF16),
        grid_spec=pltpu.PrefetchScalarGridSpec(
            num_scalar_prefetch=2,
            grid=(n_seq,),
            in_specs=sel_specs + win_specs + [
                pl.BlockSpec((None, NSA_HEADS, HEAD_DIM), tok),
                pl.BlockSpec((None, 1, 2 * NSA_KV_W), tok),
                pl.BlockSpec((None, 1, 2 * NSA_KV_W), tok),
                pl.BlockSpec((None, NSA_HEADS, HEAD_DIM), tok),
                pl.BlockSpec((None, NSA_HEADS, 3), tok),
            ],
            out_specs=pl.BlockSpec((None, NSA_HEADS, HEAD_DIM), tok),
        ),
        compiler_params=_cparams(("parallel",)),
        name="sel_win_sample",
    )(blk_rows, new_slot, *([cache_sel] * len(sel_specs)), *([cache_win] * len(win_specs)),
      q3, sel_new, win_new, o_c, gates3)


def _pool_sample_body(uc_ref, pw_ref, ps_ref, y_ref):
    rows = uc_ref.shape[0]
    for gi, win in enumerate(POOL_WINDOWS):
        c0, c1 = gi * POOL_GROUP, (gi + 1) * POOL_GROUP
        acc = uc_ref[rows - 1, :, c0:c1]
        for r in range(rows - 2, rows - 1 - win, -1):
            acc = acc + uc_ref[r, :, c0:c1]
        d = acc / float(win) - uc_ref[rows - 1, :, c0:c1]
        y_ref[:, c0:c1] = (_dot(d.astype(BF16), pw_ref[gi].astype(BF16)) * ps_ref[:, c0:c1]).astype(BF16)


def _pool_sample(u_ctx_t, pool_w, pool_scale):
    rows, n_seq, width = u_ctx_t.shape
    assert rows >= max(POOL_WINDOWS)
    return pl.pallas_call(
        _pool_sample_body,
        out_shape=jax.ShapeDtypeStruct((n_seq, width), BF16),
        name="pool_sample",
    )(u_ctx_t, pool_w, pool_scale)


def _gla_sample_body(q_ref, k_ref, v_ref, ga_ref, gr_ref, s_ref, wa2t_ref, ba_ref, gain_ref, og_ref, so_ref):
    z = jnp.sum(wa2t_ref[...] * ga_ref[...], axis=-1, keepdims=True) + ba_ref[...]
    decay = jnp.exp(_log_sigmoid(z) / GLA_TAU)
    v = v_ref[...]
    v_rows = jnp.concatenate(
        [jnp.broadcast_to(v[:, hd * GLA_DV:(hd + 1) * GLA_DV], (GLA_DK, GLA_DV)) for hd in range(GLA_HEADS)], axis=0)
    state = decay * s_ref[...] + k_ref[...] * v_rows
    so_ref[...] = state
    qs = (q_ref[...] * (GLA_DK ** -0.5)) * state
    gr = gr_ref[...]
    for hd in range(GLA_HEADS):
        c0, c1 = hd * GLA_DV, (hd + 1) * GLA_DV
        o = jnp.sum(qs[hd * GLA_DK:(hd + 1) * GLA_DK], axis=0, keepdims=True)
        og_ref[:, c0:c1] = (_rms(o, gain_ref[:, c0:c1]) * _silu(gr[:, c0:c1])).astype(BF16)


def _gla_sample(q_col, k_col, v, ga, gr, state, wa2_t, ba_col, gain):
    n_seq = v.shape[0]
    tok = lambda n: (n, 0, 0)
    fixed = lambda n: (0, 0)
    return pl.pallas_call(
        _gla_sample_body,
        out_shape=(jax.ShapeDtypeStruct((n_seq, 1, GLA_WIDTH), BF16),
                   jax.ShapeDtypeStruct((n_seq, GLA_QK, GLA_DV), F32)),
        grid=(n_seq,),
        in_specs=[
            pl.BlockSpec((None, GLA_QK, 1), tok),
            pl.BlockSpec((None, GLA_QK, 1), tok),
            pl.BlockSpec((None, 1, GLA_WIDTH), tok),
            pl.BlockSpec((None, 1, LANE), tok),
            pl.BlockSpec((None, 1, GLA_WIDTH), tok),
            pl.BlockSpec((None, GLA_QK, GLA_DV), tok),
            pl.BlockSpec((GLA_QK, LANE), fixed),
            pl.BlockSpec((GLA_QK, 1), fixed),
            pl.BlockSpec((1, GLA_WIDTH), fixed),
        ],
        out_specs=(pl.BlockSpec((None, 1, GLA_WIDTH), tok), pl.BlockSpec((None, GLA_QK, GLA_DV), tok)),
        compiler_params=_cparams(("parallel",)),
        name="gla_sample",
    )(q_col, k_col, v, ga, gr, state, wa2_t, ba_col, gain)


def _rope_tables(pos):
    half = HEAD_DIM // 2
    inv = ROPE_THETA ** (-jnp.arange(half, dtype=F32) / half)
    ang = pos.astype(F32)[:, None] * inv[None, :]
    cos, sin = jnp.cos(ang), jnp.sin(ang)
    return jnp.concatenate([cos, cos], axis=1), jnp.concatenate([-sin, sin], axis=1)


def _layer_weights(p, l):
    w_in = p['w_in'][l]
    n_gate = 3 * NSA_HPG
    gate_cols = [_pad_cols(w_in[:, _C_GT + g * n_gate:_C_GT + (g + 1) * n_gate], LANE) for g in range(NSA_KV_HEADS)]
    gate_b = p['nsa_gate_bias'][l]
    ga_cols = _pad_cols(w_in[:, _C_GA:_C_GR], LANE)
    gla_cols = [w_in[:, _C_GQ:_C_GA], w_in[:, _C_GR:_C_END], ga_cols]
    row = lambda a: a.reshape(1, -1)
    return dict(
        layer=l,
        ffn1_norm=row(p['ffn1_norm'][l]), ffn1_gu=p['ffn1_w_gu_b'], ffn1_down=p['ffn1_w_down_b'],
        ffn2_norm=row(p['ffn2_norm'][l]), ffn2_gu=p['ffn2_w_gu_b'], ffn2_down=p['ffn2_w_down_b'],
        mix_norm=row(p['mix_norm'][l]),
        w_qkv=jnp.concatenate([w_in[:, _C_Q:_C_GT]] + gate_cols, axis=1).astype(BF16),
        gate_b=jnp.concatenate([_pad_cols(row(gate_b[g * n_gate:(g + 1) * n_gate]), LANE) for g in range(NSA_KV_HEADS)], axis=1),
        q_gain=row(p['nsa_q_gain'][l]), k_gain=p['nsa_k_gain'][l],
        wk=p['nsa_cmp_wk'][l], wv=p['nsa_cmp_wv'][l],
        w_cmp=jnp.stack([p['nsa_cmp_wk'][l], p['nsa_cmp_wv'][l]]).transpose(2, 0, 1, 3).reshape(-1, HEAD_DIM),
        w_u=w_in[:, _C_U:_C_GQ].astype(BF16),
        w_gla=jnp.concatenate(gla_cols, axis=1).astype(BF16),
        w_sample=jnp.concatenate([w_in[:, _C_U:_C_GQ]] + gla_cols, axis=1).astype(BF16),
        pool_w=p['pool_w'][l], pool_scale=row(p['pool_scale'][l]),
        wa2=jnp.pad(p['gla_wa2'][l], ((0, LANE - GLA_RANK), (0, 0))).astype(BF16),
        wa2_t=_pad_cols(p['gla_wa2'][l].T, LANE),
        ba=row(p['gla_ba'][l]), ba_col=p['gla_ba'][l].reshape(-1, 1),
        gla_gain=row(p['gla_o_gain'][l]),
        w_out=p['w_out'][l].astype(BF16),
        mem_q_norm=row(p['mem_q_norm'][l]), mem_kv_norm=row(p['mem_kv_norm'][l]),
        mem_wq=p['mem_wq'][l].astype(BF16), mem_wkv=p['mem_wkv'][l].astype(BF16),
        mem_q_gain=row(p['mem_q_gain'][l]), mem_k_gain=row(p['mem_k_gain'][l]),
        mem_wo=p['mem_wo'][l].astype(BF16),
    )


def _prompt_layer(x, mem, w, tables, *, n_seq, s_len, tm):
    cos, sin = tables
    x = _ffn(x, w['ffn1_norm'], w['ffn1_gu'], w['ffn1_down'], w['layer'], tm=min(512, n_seq * s_len))
    q, kv_b, cmp_kv, sel_kv, win_kv, gates = _qkv_proj(
        x, w['mix_norm'], w['w_qkv'], cos, sin, w['q_gain'], w['k_gain'], w['gate_b'], tm=tm, tiles_per_seq=s_len // tm)
    o_nsa = _nsa_prompt(q, gates, kv_b, w['wk'], w['wv'], n_seq=n_seq, s_len=s_len, tq=tm)
    u, y_pool = _pool_prompt(x, w['mix_norm'], w['w_u'], w['pool_w'], w['pool_scale'], n_seq=n_seq, s_len=s_len, tm=tm)
    o_gla, state_t = _gla_prompt(x, w['mix_norm'], w['w_gla'], w['wa2'], w['ba'], w['gla_gain'],
                                 n_seq=n_seq, s_len=s_len, tm=tm)
    mkv = _norm_matmul(mem, w['mem_kv_norm'], w['mem_wkv'], w['mem_k_gain'], tm=tm, n_norm_heads=MEM_HEADS)
    mt = mem.shape[0] // n_seq
    x = _post_mix(x, o_nsa, y_pool, o_gla, w['w_out'], mkv.reshape(n_seq, mt, 2 * MEM_WIDTH), w['mem_q_norm'],
                  w['mem_wq'], w['mem_q_gain'], w['mem_wo'], n_seq=n_seq, t_len=s_len, tm=tm)
    x = _ffn(x, w['ffn2_norm'], w['ffn2_gu'], w['ffn2_down'], w['layer'], tm=min(512, n_seq * s_len))
    w_keep = min(WINDOW, s_len)
    kv_shape = (n_seq, s_len, 2, NSA_KV_HEADS, HEAD_DIM)
    state = state_t.reshape(n_seq, GLA_DV, GLA_HEADS, GLA_DK).transpose(0, 2, 3, 1)
    outs = (cmp_kv.reshape(kv_shape), sel_kv.reshape(kv_shape), win_kv.reshape(kv_shape)[:, s_len - w_keep:],
            u.reshape(n_seq, s_len, POOL_WIDTH)[:, s_len - POOL_BUF:], state,
            mkv.reshape(n_seq, mt, 2, MEM_HEADS, HEAD_DIM))
    return x, outs


def _post_sample_body(x_ref, on_ref, yp_ref, og_ref, wout_ref, mem_ref, qn_ref, wq_ref, qg_ref, wo_ref, o_ref,
                      x1_ref, q_ref, att_ref):
    n = pl.program_id(0)

    @pl.when(n == 0)
    def _():
        c1, c2 = NSA_Q_W, NSA_Q_W + POOL_WIDTH
        x1 = (x_ref[...] + _dot(on_ref[...], wout_ref[0:c1, :]) + _dot(yp_ref[...], wout_ref[c1:c2, :])
              + _dot(og_ref[...], wout_ref[c2:, :]))
        x1_ref[...] = x1
        q = _dot(_rms(x1, qn_ref[...]).astype(BF16), wq_ref[...])
        for hd in range(MEM_HEADS):
            c0 = hd * HEAD_DIM
            q_ref[:, c0:c0 + HEAD_DIM] = _rms(q[:, c0:c0 + HEAD_DIM], qg_ref[...])

    rows = mem_ref[...].astype(BF16)
    q_row = q_ref[pl.ds(n, 1), :]
    q_heads = jnp.concatenate([q_row[:, hd * HEAD_DIM:(hd + 1) * HEAD_DIM] for hd in range(MEM_HEADS)] * 2, axis=0)
    kind = lax.broadcasted_iota(jnp.int32, (1, rows.shape[0]), 1) % (2 * MEM_HEADS)
    head = lax.broadcasted_iota(jnp.int32, (2 * MEM_HEADS, 1), 0) % MEM_HEADS
    s = _dot_nt(q_heads.astype(BF16), rows) * SCALE + jnp.where(kind == head, 0.0, NEG)
    e = jnp.exp(s - jnp.max(s, axis=-1, keepdims=True))
    e_val = pltpu.roll(e, MEM_HEADS, axis=1)
    o = _dot(e_val.astype(BF16), rows) / jnp.sum(e, axis=-1, keepdims=True)
    att_ref[pl.ds(n, 1), :] = jnp.concatenate([o[hd:hd + 1] for hd in range(MEM_HEADS)], axis=1)

    @pl.when(n == pl.num_programs(0) - 1)
    def _():
        o_ref[...] = x1_ref[...] + _dot(att_ref[...].astype(BF16), wo_ref[...])


def _post_mix_sample(x, o_nsa, y_pool, o_gla, w_out, mem_rows, q_norm, wq, q_gain, wo, layer):
    n_seq, d = x.shape
    fixed = lambda n: (0, 0)
    full = lambda a: pl.BlockSpec(a.shape, fixed)
    return pl.pallas_call(
        _post_sample_body,
        out_shape=jax.ShapeDtypeStruct((n_seq, d), F32),
        grid=(n_seq,),
        in_specs=[full(x), full(o_nsa), full(y_pool), full(o_gla), full(w_out),
                  pl.BlockSpec((None, None) + mem_rows.shape[2:], lambda n: (layer, n, 0, 0)),
                  full(q_norm), full(wq), full(q_gain), full(wo)],
        out_specs=pl.BlockSpec((n_seq, d), fixed),
        scratch_shapes=[pltpu.VMEM((n_seq, d), F32), pltpu.VMEM((n_seq, MEM_WIDTH), F32),
                        pltpu.VMEM((n_seq, MEM_WIDTH), F32)],
        compiler_params=_cparams(("arbitrary",)),
        name="mix_out_mem_attn_sample",
    )(x, o_nsa, y_pool, o_gla, w_out, mem_rows, q_norm, wq, q_gain, wo)


def _sample_layer(x, w, tables, caches, page_table, layer):
    cache_cmp, cache_sel, cache_win, cache_pool, state_gla, cache_mem = caches
    cos, sin = tables
    n_seq = x.shape[0]
    n_pages = page_table.shape[1]
    bpp = PAGE_SIZE // SEL_BLOCK
    nbp = n_pages * bpp
    kvw = 2 * NSA_KV_W
    x = _ffn(x, w['ffn1_norm'], w['ffn1_gu'], w['ffn1_down'], w['layer'], tm=n_seq)
    q, _, cmp_new, sel_new, win_new, gates = _qkv_proj(
        x, w['mix_norm'], w['w_qkv'], cos, sin, w['q_gain'], w['k_gain'], w['gate_b'], tm=n_seq, tiles_per_seq=1)
    raw = _norm_matmul(x, w['mix_norm'], w['w_sample'], w['q_gain'], tm=n_seq)
    u = raw[:, :POOL_WIDTH]
    gq, gk = raw[:, POOL_WIDTH:POOL_WIDTH + GLA_QK], raw[:, POOL_WIDTH + GLA_QK:POOL_WIDTH + 2 * GLA_QK]
    c0 = POOL_WIDTH + 2 * GLA_QK
    gv, gr, ga = raw[:, c0:c0 + GLA_WIDTH], raw[:, c0 + GLA_WIDTH:c0 + 2 * GLA_WIDTH], raw[:, c0 + 2 * GLA_WIDTH:]

    q3 = q.reshape(n_seq, NSA_HEADS, HEAD_DIM)
    n_pool = cache_cmp.shape[1]
    cmpc = _cmp_pages(cache_cmp.reshape(-1, n_pool, PAGE_SIZE * KV_ROWS, HEAD_DIM), page_table, w['w_cmp'], layer)
    o_c, idx = _cmp_select(q3, cmpc)
    idx = idx[:, :, :SEL_TOPK].transpose(1, 0, 2)
    pidx = jnp.minimum(idx, nbp - 1)
    phys = jnp.take_along_axis(page_table[:, None, :], pidx // bpp, axis=2)
    blk_rows = (phys * bpp + pidx % bpp).reshape(-1).astype(jnp.int32)
    new_slot = jnp.argmax(idx >= nbp, axis=-1).reshape(-1).astype(jnp.int32)
    gates3 = jnp.concatenate([gates[:, :3 * NSA_HPG], gates[:, LANE:LANE + 3 * NSA_HPG]], axis=1)
    w_buf = cache_win.shape[2]
    o_nsa = _sel_win(blk_rows, new_slot, cache_sel.reshape(-1, n_pool * bpp, SEL_BLOCK * KV_ROWS, HEAD_DIM),
                     cache_win.reshape(-1, n_seq, w_buf * KV_ROWS, HEAD_DIM), q3, sel_new.reshape(n_seq, 1, kvw),
                     win_new.reshape(n_seq, 1, kvw), o_c, gates3.reshape(n_seq, NSA_HEADS, 3), layer)

    u_ctx = jnp.concatenate([cache_pool[layer], u[:, None, :]], axis=1)
    y_pool = _pool_sample(u_ctx.transpose(1, 0, 2), w['pool_w'], w['pool_scale'])
    o_gla, new_state = _gla_sample(
        gq.reshape(n_seq, GLA_QK, 1), gk.reshape(n_seq, GLA_QK, 1), gv.reshape(n_seq, 1, GLA_WIDTH),
        ga.reshape(n_seq, 1, LANE), gr.reshape(n_seq, 1, GLA_WIDTH), state_gla[layer].reshape(n_seq, GLA_QK, GLA_DV),
        w['wa2_t'], w['ba_col'], w['gla_gain'])

    mt = cache_mem.shape[2]
    x = _post_mix_sample(x, o_nsa.reshape(n_seq, NSA_Q_W), y_pool, o_gla.reshape(n_seq, GLA_WIDTH), w['w_out'],
                         cache_mem.reshape(-1, n_seq, mt * 2 * MEM_HEADS, HEAD_DIM), w['mem_q_norm'], w['mem_wq'],
                         w['mem_q_gain'], w['mem_wo'], layer)
    x = _ffn(x, w['ffn2_norm'], w['ffn2_gu'], w['ffn2_down'], w['layer'], tm=n_seq)
    kv_shape = (n_seq, 1, 2, NSA_KV_HEADS, HEAD_DIM)
    win_buf = jnp.concatenate([cache_win[layer], win_new.reshape(kv_shape)], axis=1)[:, 1:]
    outs = (cmp_new.reshape(kv_shape), sel_new.reshape(kv_shape), win_buf, u_ctx[:, 1:],
            new_state.reshape(n_seq, GLA_HEADS, GLA_DK, GLA_DV))
    return x, outs


def kernel(x_prompt, x_sample, cache_cmp_kv, cache_sel_kv, cache_win_kv, cache_pool, state_gla, cache_mem_kv, page_table, mem_prompt, ffn1_norm, ffn1_w_gu, ffn1_w_down, mix_norm, w_in, nsa_q_gain, nsa_k_gain, nsa_gate_bias, nsa_cmp_wk, nsa_cmp_wv, pool_w, pool_scale, gla_wa2, gla_ba, gla_o_gain, w_out, mem_q_norm, mem_kv_norm, mem_wq, mem_wkv, mem_q_gain, mem_k_gain, mem_wo, ffn2_norm, ffn2_w_gu, ffn2_w_down):
    params = dict(ffn1_norm=ffn1_norm, ffn1_w_gu=ffn1_w_gu, ffn1_w_down=ffn1_w_down, mix_norm=mix_norm, w_in=w_in,
                  nsa_q_gain=nsa_q_gain, nsa_k_gain=nsa_k_gain, nsa_gate_bias=nsa_gate_bias, nsa_cmp_wk=nsa_cmp_wk,
                  nsa_cmp_wv=nsa_cmp_wv, pool_w=pool_w, pool_scale=pool_scale, gla_wa2=gla_wa2, gla_ba=gla_ba,
                  gla_o_gain=gla_o_gain, w_out=w_out, mem_q_norm=mem_q_norm, mem_kv_norm=mem_kv_norm, mem_wq=mem_wq,
                  mem_wkv=mem_wkv, mem_q_gain=mem_q_gain, mem_k_gain=mem_k_gain, mem_wo=mem_wo, ffn2_norm=ffn2_norm,
                  ffn2_w_gu=ffn2_w_gu, ffn2_w_down=ffn2_w_down)
    depth = w_in.shape[0]
    n_seq, s_len, d = x_prompt.shape
    n_dec, dec_len, _ = x_sample.shape
    assert dec_len == 1
    past = page_table.shape[1] * PAGE_SIZE
    tables_p = _rope_tables(jnp.arange(s_len))
    tables_s = _rope_tables(jnp.full((n_dec,), past))
    caches = (cache_cmp_kv, cache_sel_kv, cache_win_kv, cache_pool, state_gla, cache_mem_kv)
    xp = x_prompt.reshape(n_seq * s_len, d)
    xs = x_sample.reshape(n_dec, d)
    mem = mem_prompt.reshape(-1, d)
    for name in ('ffn1_w_gu', 'ffn1_w_down', 'ffn2_w_gu', 'ffn2_w_down'):
        params[name + '_b'] = params[name].astype(BF16)
    p_outs, s_outs = [], []
    for l in range(depth):
        w = _layer_weights(params, l)
        xp, po = _prompt_layer(xp, mem, w, tables_p, n_seq=n_seq, s_len=s_len, tm=256)
        xs, so = _sample_layer(xs, w, tables_s, caches, page_table, l)
        p_outs.append(po)
        s_outs.append(so)
    stack = lambda outs, k: jnp.stack([o[k] for o in outs])
    return ((xp.reshape(n_seq, s_len, d), xs.reshape(n_dec, 1, d))
            + tuple(stack(p_outs, k) for k in range(6)) + tuple(stack(s_outs, k) for k in range(5)))
```

```python
import functools

import jax
import jax.numpy as jnp
import numpy as np
from jax import lax
from jax.experimental import pallas as pl
from jax.experimental.pallas import tpu as pltpu

F32 = jnp.float32
BF16 = jnp.bfloat16

D_MODEL = 2048
HEAD_DIM = 128
ROPE_THETA = 10000.0
EPS = 1e-6
NEG = -1e30
FORCE = 1e4
SCALE = HEAD_DIM ** -0.5
PAGE_SIZE = 128
NSA_HEADS = 8
NSA_KV_HEADS = 2
NSA_HPG = NSA_HEADS // NSA_KV_HEADS
NSA_Q_W = NSA_HEADS * HEAD_DIM
NSA_KV_W = NSA_KV_HEADS * HEAD_DIM
CMP_BLOCK = 64
SEL_BLOCK = 64
SEL_TOPK = 16
WINDOW = 512
POOL_WIDTH = 512
POOL_WINDOWS = (2, 4, 8, 16)
POOL_GROUP = POOL_WIDTH // len(POOL_WINDOWS)
POOL_BUF = max(POOL_WINDOWS) - 1
POOL_HIST = 16
GLA_WIDTH = 512
GLA_HEADS = 4
GLA_DK = 64
GLA_DV = 128
GLA_RANK = 16
GLA_TAU = 16.0
GLA_CHUNK = 64
GLA_SUB = 16
GLA_QK = GLA_HEADS * GLA_DK
MEM_HEADS = 4
MEM_WIDTH = MEM_HEADS * HEAD_DIM
LANE = 128
PAD_SCORE = -3.0e38

_C_Q = 0
_C_KV = _C_Q + NSA_Q_W
_C_GT = _C_KV + 6 * NSA_KV_W
_C_U = _C_GT + 3 * NSA_HEADS
_C_GQ = _C_U + POOL_WIDTH
_C_GK = _C_GQ + GLA_QK
_C_GV = _C_GK + GLA_QK
_C_GA = _C_GV + GLA_WIDTH
_C_GR = _C_GA + GLA_RANK
_C_END = _C_GR + GLA_WIDTH

VMEM_LIMIT = 48 * 1024 * 1024


def _cparams(sem):
    return pltpu.CompilerParams(dimension_semantics=sem, vmem_limit_bytes=VMEM_LIMIT)


def _dot(a, b):
    return jnp.dot(a, b, preferred_element_type=F32)


def _dot_nt(a, b):
    return lax.dot_general(a, b, (((1,), (1,)), ((), ())), preferred_element_type=F32)


def _rms(x, g):
    return x * lax.rsqrt(jnp.mean(x * x, axis=-1, keepdims=True) + EPS) * g


def _sigmoid(x):
    return 1.0 / (1.0 + jnp.exp(-x))


def _silu(x):
    return x * _sigmoid(x)


def _log_sigmoid(x):
    return jnp.minimum(x, 0.0) - jnp.log1p(jnp.exp(-jnp.abs(x)))


def _rope(y, cos, sin):
    return y * cos + pltpu.roll(y, HEAD_DIM // 2, axis=1) * sin


def _pad_cols(w, width):
    return jnp.pad(w, ((0, 0), (0, width - w.shape[1])))


KV_ROWS = 2 * NSA_KV_HEADS
SUBLANES = 8
PAGES_PER_STEP = 8


def _ffn_step(x_ref, g_ref, wg_ref, wu_ref, wd_ref, o_ref, xn_ref, side_job=None):
    j = pl.program_id(1)

    @pl.when(j == 0)
    def _():
        xn_ref[...] = _rms(x_ref[...], g_ref[...]).astype(BF16)
        o_ref[...] = jnp.zeros_like(o_ref)

    if side_job is not None:
        side_job()
    xn = xn_ref[...]
    a = _dot(xn, wg_ref[...])
    b = _dot(xn, wu_ref[...])
    h = (_silu(a) * b).astype(BF16)
    o_ref[...] += _dot(h, wd_ref[...])

    @pl.when(j == pl.num_programs(1) - 1)
    def _():
        o_ref[...] = x_ref[...] + 0.5 * o_ref[...]


def _ffn_body(x_ref, g_ref, wg_ref, wu_ref, wd_ref, o_ref, xn_ref):
    _ffn_step(x_ref, g_ref, wg_ref, wu_ref, wd_ref, o_ref, xn_ref)


def _ffn_pages_body(pt_ref, x_ref, g_ref, wg_ref, wu_ref, wd_ref, *refs):
    del pt_ref
    pages, w_ref, o_ref, c_ref, xn_ref = refs[:PAGES_PER_STEP], *refs[PAGES_PER_STEP:]

    def page_sums():
        w = w_ref[...]
        blk_rows = CMP_BLOCK * KV_ROWS
        sums = []
        for page in pages:
            for half in range(PAGE_SIZE // CMP_BLOCK):
                prod = page[half * blk_rows:(half + 1) * blk_rows, :] * w
                part = jnp.sum(prod.reshape(blk_rows // SUBLANES, SUBLANES, HEAD_DIM), axis=0)
                sums.append(part[0:KV_ROWS] + part[KV_ROWS:2 * KV_ROWS])
        for r in range(KV_ROWS):
            c_ref[r] = jnp.concatenate([s[r:r + 1] for s in sums], axis=0)

    _ffn_step(x_ref, g_ref, wg_ref, wu_ref, wd_ref, o_ref, xn_ref, side_job=page_sums)


def _ffn(x, g, w_gu, w_down, layer, *, tm, tf=512, pages=None):
    m, d = x.shape
    f = w_down.shape[1]
    nf = f // tf
    grid = (m // tm, nf)
    x_spec = pl.BlockSpec((tm, d), lambda i, j, *_: (i, 0))
    in_specs = [
        x_spec,
        pl.BlockSpec((1, d), lambda i, j, *_: (0, 0)),
        pl.BlockSpec((None, d, tf), lambda i, j, *_: (layer, 0, j)),
        pl.BlockSpec((None, d, tf), lambda i, j, *_: (layer, 0, j + nf)),
        pl.BlockSpec((None, tf, d), lambda i, j, *_: (layer, j, 0)),
    ]
    scratch = [pltpu.VMEM((tm, d), BF16)]
    params = _cparams(("parallel", "arbitrary"))
    if pages is None:
        return pl.pallas_call(
            _ffn_body, out_shape=jax.ShapeDtypeStruct((m, d), F32), grid=grid, in_specs=in_specs,
            out_specs=x_spec, scratch_shapes=scratch, compiler_params=params, name="ffn_half_step",
        )(x, g, w_gu, w_gu, w_down)

    cache, page_table, w_cmp = pages
    n_seq, n_pages = page_table.shape
    bpp = PAGE_SIZE // CMP_BLOCK
    steps_per_seq = n_pages // PAGES_PER_STEP
    n_page_steps = n_seq * steps_per_seq
    assert n_pages % PAGES_PER_STEP == 0 and 2 * KV_ROWS == SUBLANES and n_page_steps <= grid[0] * grid[1]

    def page_step(i, j):
        return jnp.minimum(i * nf + j, n_page_steps - 1)

    def page_spec(k):
        def index(i, j, pt):
            t = page_step(i, j)
            return (layer, pt[t // steps_per_seq, (t % steps_per_seq) * PAGES_PER_STEP + k], 0, 0)
        return pl.BlockSpec((None, None, cache.shape[2], HEAD_DIM), index)

    def sums_index(i, j, pt):
        t = page_step(i, j)
        return (t // steps_per_seq, 0, t % steps_per_seq, 0)

    return pl.pallas_call(
        _ffn_pages_body,
        out_shape=(jax.ShapeDtypeStruct((m, d), F32),
                   jax.ShapeDtypeStruct((n_seq, KV_ROWS, n_pages * bpp, HEAD_DIM), F32)),
        grid_spec=pltpu.PrefetchScalarGridSpec(
            num_scalar_prefetch=1,
            grid=grid,
            in_specs=in_specs + [page_spec(k) for k in range(PAGES_PER_STEP)]
            + [pl.BlockSpec((CMP_BLOCK * KV_ROWS, HEAD_DIM), lambda i, j, pt: (0, 0))],
            out_specs=(x_spec, pl.BlockSpec((None, KV_ROWS, PAGES_PER_STEP * bpp, HEAD_DIM), sums_index)),
            scratch_shapes=scratch,
        ),
        compiler_params=_cparams(("arbitrary", "arbitrary")),
        name="ffn_half_step_with_page_sums",
    )(page_table, x, g, w_gu, w_gu, w_down, *([cache] * PAGES_PER_STEP), w_cmp)


def _qkv_body(x_ref, g_ref, w_ref, cos_ref, sin_ref, qg_ref, kg_ref, gb_ref,
              q_ref, kvb_ref, cmp_ref, sel_ref, win_ref, gt_ref):
    tm = x_ref.shape[0]
    xn = _rms(x_ref[...], g_ref[...]).astype(BF16)
    h = _dot(xn, w_ref[...])
    cos = cos_ref[...]
    sin = sin_ref[...]

    def head(c0, gain):
        return _rope(_rms(h[:, c0:c0 + HEAD_DIM], gain), cos, sin)

    for hh in range(NSA_HEADS):
        c0 = _C_Q + hh * HEAD_DIM
        q_ref[:, hh * HEAD_DIM:(hh + 1) * HEAD_DIM] = head(c0, qg_ref[...]).astype(BF16)
    for br, out in enumerate((cmp_ref, sel_ref, win_ref)):
        base = _C_KV + br * 2 * NSA_KV_W
        for r in range(KV_ROWS):
            c0 = base + r * HEAD_DIM
            slab = head(c0, kg_ref[br:br + 1, :]) if r < NSA_KV_HEADS else h[:, c0:c0 + HEAD_DIM]
            out[pl.ds(r, tm, stride=KV_ROWS), :] = slab
            kvb_ref[:, c0 - _C_KV:c0 - _C_KV + HEAD_DIM] = slab.astype(BF16)
    gt_ref[...] = _sigmoid(h[:, _C_GT:_C_GT + 2 * LANE] + gb_ref[...])


def _qkv_proj(x, g, w, cos, sin, q_gain, k_gain, gate_b, *, tm, tiles_per_seq):
    m, d = x.shape
    n = w.shape[1]
    row = lambda i: (i, 0)
    fixed = lambda i: (0, 0)
    pos = lambda i: (i % tiles_per_seq, 0)
    return pl.pallas_call(
        _qkv_body,
        out_shape=(
            jax.ShapeDtypeStruct((m, NSA_Q_W), BF16),
            jax.ShapeDtypeStruct((m, 6 * NSA_KV_W), BF16),
            jax.ShapeDtypeStruct((m * KV_ROWS, HEAD_DIM), F32),
            jax.ShapeDtypeStruct((m * KV_ROWS, HEAD_DIM), F32),
            jax.ShapeDtypeStruct((m * KV_ROWS, HEAD_DIM), F32),
            jax.ShapeDtypeStruct((m, 2 * LANE), F32),
        ),
        grid=(m // tm,),
        in_specs=[
            pl.BlockSpec((tm, d), row),
            pl.BlockSpec((1, d), fixed),
            pl.BlockSpec((d, n), fixed),
            pl.BlockSpec((tm, HEAD_DIM), pos),
            pl.BlockSpec((tm, HEAD_DIM), pos),
            pl.BlockSpec((1, HEAD_DIM), fixed),
            pl.BlockSpec((3, HEAD_DIM), fixed),
            pl.BlockSpec((1, 2 * LANE), fixed),
        ],
        out_specs=(
            pl.BlockSpec((tm, NSA_Q_W), row),
            pl.BlockSpec((tm, 6 * NSA_KV_W), row),
            pl.BlockSpec((tm * KV_ROWS, HEAD_DIM), row),
            pl.BlockSpec((tm * KV_ROWS, HEAD_DIM), row),
            pl.BlockSpec((tm * KV_ROWS, HEAD_DIM), row),
            pl.BlockSpec((tm, 2 * LANE), row),
        ),
        compiler_params=_cparams(("parallel",)),
        name="nsa_projection",
    )(x, g, w, cos, sin, q_gain, k_gain, gate_b)


def _softmax_av(s, v):
    e = jnp.exp(s - jnp.max(s, axis=-1, keepdims=True))
    return _dot(e.astype(BF16), v) / jnp.sum(e, axis=-1, keepdims=True)


EXP2_SCALE = SCALE * float(np.log2(np.e))
NSA_PREFIXES = 8


def _nsa_prompt_body(q_ref, gt_ref, kc_ref, vc_ref, ksb_ref, vsb_ref, kwb_ref, vwb_ref, wk_ref, wv_ref, o_ref,
                     kcb_ref, vcb_ref, acc_ref):
    i = pl.program_id(2)
    tq = q_ref.shape[0]
    s_len = kc_ref.shape[0]
    nb = s_len // CMP_BLOCK

    @pl.when(i == 0)
    def _():
        kcf = kc_ref[...].astype(F32).reshape(nb, CMP_BLOCK, HEAD_DIM)
        vcf = vc_ref[...].astype(F32).reshape(nb, CMP_BLOCK, HEAD_DIM)
        kcb_ref[...] = jnp.sum(kcf * wk_ref[...][None], axis=1).astype(BF16)
        vcb_ref[...] = jnp.sum(vcf * wv_ref[...][None], axis=1).astype(BF16)

    qpos = i * tq + lax.broadcasted_iota(jnp.int32, (tq, 1), 0)
    qpos_t = i * tq + lax.broadcasted_iota(jnp.int32, (1, tq), 1)
    kc = kcb_ref[...]
    vc = vcb_ref[...]
    blk = lax.broadcasted_iota(jnp.int32, (1, nb), 1)
    blk_t = lax.broadcasted_iota(jnp.int32, (nb, 1), 0)
    valid_c = ((blk + 1) * CMP_BLOCK - 1) <= qpos
    valid_t = ((blk_t + 1) * CMP_BLOCK - 1) <= qpos_t
    imp_t = jnp.zeros((nb, tq), F32)
    o_c = []
    for hh in range(NSA_HPG):
        qh = q_ref[:, hh * HEAD_DIM:(hh + 1) * HEAD_DIM]
        s = jnp.where(valid_c, _dot_nt(qh, kc) * SCALE, NEG)
        e = jnp.where(valid_c, jnp.exp(s - jnp.max(s, axis=-1, keepdims=True)), 0.0)
        p = e / jnp.maximum(jnp.sum(e, axis=-1, keepdims=True), 1e-30)
        o_c.append(_dot(p.astype(BF16), vc))
        s_t = jnp.where(valid_t, _dot_nt(kc, qh) * SCALE, NEG)
        e_t = jnp.where(valid_t, jnp.exp(s_t - jnp.max(s_t, axis=0, keepdims=True)), 0.0)
        imp_t = imp_t + e_t / jnp.maximum(jnp.sum(e_t, axis=0, keepdims=True), 1e-30)

    cur_t = qpos_t // SEL_BLOCK
    forced = jnp.where(blk_t == cur_t, 1, jnp.where(blk_t == cur_t - 1, 1, jnp.where(blk_t == 0, 1, 0)))
    score = jnp.where(blk_t > cur_t, -FORCE, jnp.where(forced > 0, FORCE, imp_t))
    rank = jnp.zeros((nb, tq), jnp.int32)
    for b2 in range(nb):
        other = score[b2:b2 + 1, :]
        tie = jnp.where(other == score, jnp.where(blk_t > b2, 1, 0), 0)
        rank = rank + jnp.where(other > score, 1, tie)
    keep = jnp.where(rank < min(SEL_TOPK, nb), 1.0, 0.0).T.astype(BF16)

    nt = s_len // tq
    n_prefix = min(NSA_PREFIXES, nt)
    tiles_per_prefix = nt // n_prefix
    for k in range(n_prefix):
        @pl.when(i // tiles_per_prefix == k)
        def _(k=k):
            width = (k + 1) * tiles_per_prefix * tq
            expand = jnp.where(
                lax.broadcasted_iota(jnp.int32, (nb, width), 1) // SEL_BLOCK
                == lax.broadcasted_iota(jnp.int32, (nb, width), 0), 1.0, 0.0).astype(BF16)
            keep_keys = _dot(keep, expand)
            kpos = lax.broadcasted_iota(jnp.int32, (1, width), 1)
            bias = jnp.where(keep_keys > 0.5, jnp.where(kpos <= qpos, 0.0, NEG), NEG)
            ks = ksb_ref[0:width, :]
            vs = vsb_ref[0:width, :]
            for hh in range(NSA_HPG):
                s = _dot_nt(q_ref[:, hh * HEAD_DIM:(hh + 1) * HEAD_DIM], ks) + bias
                e = jnp.exp2((s - jnp.max(s, axis=-1, keepdims=True)) * EXP2_SCALE)
                acc_ref[hh] = _dot(e.astype(BF16), vs) / jnp.sum(e, axis=-1, keepdims=True)

    span = WINDOW + tq
    kstart = pl.multiple_of(jnp.maximum(i * tq - WINDOW, 0), tq)
    kw = kwb_ref[pl.ds(kstart, span), :]
    vw = vwb_ref[pl.ds(kstart, span), :]
    rel = qpos - (kstart + lax.broadcasted_iota(jnp.int32, (1, span), 1))
    bias_w = jnp.where(rel >= 0, jnp.where(rel <= WINDOW, 0.0, NEG), NEG)

    gt = gt_ref[...]
    for hh in range(NSA_HPG):
        s = _dot_nt(q_ref[:, hh * HEAD_DIM:(hh + 1) * HEAD_DIM], kw) + bias_w
        e = jnp.exp2((s - jnp.max(s, axis=-1, keepdims=True)) * EXP2_SCALE)
        o_w = _dot(e.astype(BF16), vw) / jnp.sum(e, axis=-1, keepdims=True)
        o = (gt[:, 3 * hh:3 * hh + 1] * o_c[hh] + gt[:, 3 * hh + 1:3 * hh + 2] * acc_ref[hh]
             + gt[:, 3 * hh + 2:3 * hh + 3] * o_w)
        o_ref[:, hh * HEAD_DIM:(hh + 1) * HEAD_DIM] = o.astype(BF16)


def _nsa_prompt(q, gates, kv, wk, wv, *, n_seq, s_len, tq):
    m = q.shape[0]
    nt = s_len // tq
    nb = s_len // CMP_BLOCK
    assert s_len >= WINDOW + tq and s_len % tq == 0 and tq % SEL_BLOCK == 0 and nt % min(NSA_PREFIXES, nt) == 0
    gw = NSA_HPG * HEAD_DIM
    row = lambda n, g, i: (n * nt + i, g)

    def kv_spec(branch, is_value):
        col = branch * KV_ROWS + is_value * NSA_KV_HEADS
        return pl.BlockSpec((s_len, HEAD_DIM), lambda n, g, i: (n, col + g))

    kv_specs = [kv_spec(br, v) for br in range(3) for v in range(2)]
    wspec = pl.BlockSpec((None, CMP_BLOCK, HEAD_DIM), lambda n, g, i: (g, 0, 0))
    return pl.pallas_call(
        _nsa_prompt_body,
        out_shape=jax.ShapeDtypeStruct((m, NSA_Q_W), BF16),
        grid=(n_seq, NSA_KV_HEADS, nt),
        in_specs=[
            pl.BlockSpec((tq, gw), row),
            pl.BlockSpec((tq, LANE), row),
            *kv_specs, wspec, wspec,
        ],
        out_specs=pl.BlockSpec((tq, gw), row),
        scratch_shapes=[pltpu.VMEM((nb, HEAD_DIM), BF16), pltpu.VMEM((nb, HEAD_DIM), BF16),
                        pltpu.VMEM((NSA_HPG, tq, HEAD_DIM), F32)],
        compiler_params=_cparams(("parallel", "parallel", "arbitrary")),
        name="nsa_prompt",
    )(q, gates, *([kv] * len(kv_specs)), wk, wv)


def _pool_tile(u, t0, hist_ref, pw_ref, ps_ref, y_ref):
    tm = u.shape[0]
    ext = jnp.concatenate([hist_ref[...], u], axis=0)
    t = t0 + lax.broadcasted_iota(jnp.int32, (tm, 1), 0)
    for gi, win in enumerate(POOL_WINDOWS):
        c0, c1 = gi * POOL_GROUP, (gi + 1) * POOL_GROUP
        acc = ext[:, c0:c1]
        span = 1
        while span < win:
            acc = acc + pltpu.roll(acc, span, axis=0)
            span *= 2
        mean = acc[POOL_HIST:] / jnp.minimum(t + 1, win).astype(F32)
        y = _dot((mean - u[:, c0:c1]).astype(BF16), pw_ref[gi].astype(BF16))
        y_ref[:, c0:c1] = (y * ps_ref[:, c0:c1]).astype(BF16)
    hist_ref[...] = u[tm - POOL_HIST:]


def _gla_pool_prompt_body(x_ref, g_ref, w_ref, wa2_ref, ba_ref, gain_ref, pw_ref, ps_ref,
                          og_ref, st_ref, u_ref, y_ref, state_ref, hist_ref):
    j = pl.program_id(1)
    tm = x_ref.shape[0]

    @pl.when(j == 0)
    def _():
        state_ref[...] = jnp.zeros_like(state_ref)
        hist_ref[...] = jnp.zeros_like(hist_ref)

    h = _dot(_rms(x_ref[...], g_ref[...]).astype(BF16), w_ref[...])
    c_k, c_v, c_r, c_a = GLA_QK, 2 * GLA_QK, 2 * GLA_QK + GLA_WIDTH, 2 * GLA_QK + 2 * GLA_WIDTH
    c_u = c_a + LANE
    u = h[:, c_u:c_u + POOL_WIDTH]
    u_ref[...] = u
    _pool_tile(u, j * tm, hist_ref, pw_ref, ps_ref, y_ref)
    z = _dot(h[:, c_a:c_a + LANE].astype(BF16), wa2_ref[...]) + ba_ref[...]
    la_all = _log_sigmoid(z) / GLA_TAU

    lane = lax.broadcasted_iota(jnp.int32, (1, GLA_QK), 1)
    head_mask = [jnp.where(lane // GLA_DK == hd, 1.0, 0.0) for hd in range(GLA_HEADS)]
    rows = lax.broadcasted_iota(jnp.int32, (GLA_CHUNK, 1), 0)
    causal = rows >= lax.broadcasted_iota(jnp.int32, (1, GLA_CHUNK), 1)
    n_sub = GLA_CHUNK // GLA_SUB

    for c in range(tm // GLA_CHUNK):
        r0, r1 = c * GLA_CHUNK, (c + 1) * GLA_CHUNK
        q = h[r0:r1, 0:GLA_QK] * (GLA_DK ** -0.5)
        k = h[r0:r1, c_k:c_k + GLA_QK]
        b = la_all[r0:r1]
        sh = 1
        while sh < GLA_CHUNK:
            b = b + jnp.where(rows >= sh, pltpu.roll(b, sh, axis=0), 0.0)
            sh *= 2
        state = state_ref[...]
        state_b = state.astype(BF16)
        q_in = q * jnp.exp(b)
        a_parts = [[] for _ in range(GLA_HEADS)]
        for sb in range(n_sub):
            s0, s1 = sb * GLA_SUB, (sb + 1) * GLA_SUB
            ref_row = b[s0 - 1:s0] if sb > 0 else jnp.zeros((1, GLA_QK), F32)
            q_t = q[s0:s1] * jnp.exp(b[s0:s1] - ref_row)
            k_t = (k * jnp.exp(jnp.where(rows < s1, ref_row - b, 0.0))).astype(BF16)
            for hd in range(GLA_HEADS):
                a_parts[hd].append(_dot_nt((q_t * head_mask[hd]).astype(BF16), k_t))
        b_last = b[GLA_CHUNK - 1:GLA_CHUNK]
        k_out = k * jnp.exp(b_last - b)
        new_state = state * jnp.exp(b_last)
        for hd in range(GLA_HEADS):
            v = h[r0:r1, c_v + hd * GLA_DV:c_v + (hd + 1) * GLA_DV]
            vb = v.astype(BF16)
            a = jnp.where(causal, jnp.concatenate(a_parts[hd], axis=0), 0.0)
            o = _dot(a.astype(BF16), vb) + _dot_nt((q_in * head_mask[hd]).astype(BF16), state_b)
            gr = h[r0:r1, c_r + hd * GLA_DV:c_r + (hd + 1) * GLA_DV]
            og = _rms(o, gain_ref[:, hd * GLA_DV:(hd + 1) * GLA_DV]) * _silu(gr)
            og_ref[r0:r1, hd * GLA_DV:(hd + 1) * GLA_DV] = og.astype(BF16)
            new_state = new_state + _dot(v.T.astype(BF16), (k_out * head_mask[hd]).astype(BF16))
        state_ref[...] = new_state

    @pl.when(j == pl.num_programs(1) - 1)
    def _():
        st_ref[...] = state_ref[...]


def _gla_pool_prompt(x, g, w, wa2, ba, gain, pool_w, pool_scale, *, n_seq, s_len, tm):
    d = x.shape[1]
    nt = s_len // tm
    rows = n_seq * s_len
    row = lambda n, j: (n * nt + j, 0)
    fixed = lambda n, j: (0, 0)
    return pl.pallas_call(
        _gla_pool_prompt_body,
        out_shape=(jax.ShapeDtypeStruct((rows, GLA_WIDTH), BF16),
                   jax.ShapeDtypeStruct((n_seq, GLA_DV, GLA_QK), F32),
                   jax.ShapeDtypeStruct((rows, POOL_WIDTH), F32),
                   jax.ShapeDtypeStruct((rows, POOL_WIDTH), BF16)),
        grid=(n_seq, nt),
        in_specs=[
            pl.BlockSpec((tm, d), row),
            pl.BlockSpec((1, d), fixed),
            pl.BlockSpec((d, w.shape[1]), fixed),
            pl.BlockSpec((LANE, GLA_QK), fixed),
            pl.BlockSpec((1, GLA_QK), fixed),
            pl.BlockSpec((1, GLA_WIDTH), fixed),
            pl.BlockSpec((len(POOL_WINDOWS), POOL_GROUP, POOL_GROUP), lambda n, j: (0, 0, 0)),
            pl.BlockSpec((1, POOL_WIDTH), fixed),
        ],
        out_specs=(pl.BlockSpec((tm, GLA_WIDTH), row),
                   pl.BlockSpec((None, GLA_DV, GLA_QK), lambda n, j: (n, 0, 0)),
                   pl.BlockSpec((tm, POOL_WIDTH), row), pl.BlockSpec((tm, POOL_WIDTH), row)),
        scratch_shapes=[pltpu.VMEM((GLA_DV, GLA_QK), F32), pltpu.VMEM((POOL_HIST, POOL_WIDTH), F32)],
        compiler_params=_cparams(("parallel", "arbitrary")),
        name="gla_pool_prompt",
    )(x, g, w, wa2, ba, gain, pool_w, pool_scale)


def _norm_matmul_body(x_ref, g_ref, w_ref, hg_ref, o_ref, *, n_norm_heads):
    h = _dot(_rms(x_ref[...], g_ref[...]).astype(BF16), w_ref[...])
    if n_norm_heads:
        for hd in range(n_norm_heads):
            c0 = hd * HEAD_DIM
            o_ref[:, c0:c0 + HEAD_DIM] = _rms(h[:, c0:c0 + HEAD_DIM], hg_ref[...])
        c0 = n_norm_heads * HEAD_DIM
        o_ref[:, c0:] = h[:, c0:]
    else:
        o_ref[...] = h


def _norm_matmul(x, g, w, head_gain, *, tm, n_norm_heads=0):
    m, d = x.shape
    n = w.shape[1]
    return pl.pallas_call(
        functools.partial(_norm_matmul_body, n_norm_heads=n_norm_heads),
        out_shape=jax.ShapeDtypeStruct((m, n), F32),
        grid=(m // tm,),
        in_specs=[
            pl.BlockSpec((tm, d), lambda i: (i, 0)),
            pl.BlockSpec((1, d), lambda i: (0, 0)),
            pl.BlockSpec((d, n), lambda i: (0, 0)),
            pl.BlockSpec((1, HEAD_DIM), lambda i: (0, 0)),
        ],
        out_specs=pl.BlockSpec((tm, n), lambda i: (i, 0)),
        compiler_params=_cparams(("parallel",)),
        name="norm_matmul",
    )(x, g, w, head_gain)


def _post_body(x_ref, on_ref, yp_ref, og_ref, wout_ref, mk_ref, mv_ref, qn_ref, wq_ref, qg_ref, wo_ref, o_ref):
    c1, c2 = NSA_Q_W, NSA_Q_W + POOL_WIDTH
    x1 = (x_ref[...] + _dot(on_ref[...], wout_ref[0:c1, :]) + _dot(yp_ref[...], wout_ref[c1:c2, :])
          + _dot(og_ref[...], wout_ref[c2:, :]))
    q = _dot(_rms(x1, qn_ref[...]).astype(BF16), wq_ref[...])
    outs = []
    for hd in range(MEM_HEADS):
        c0 = hd * HEAD_DIM
        qh = _rms(q[:, c0:c0 + HEAD_DIM], qg_ref[...]).astype(BF16)
        s = _dot_nt(qh, mk_ref[:, c0:c0 + HEAD_DIM].astype(BF16)) * SCALE
        outs.append(_softmax_av(s, mv_ref[:, c0:c0 + HEAD_DIM].astype(BF16)).astype(BF16))
    o_ref[...] = x1 + _dot(jnp.concatenate(outs, axis=1), wo_ref[...])


def _post_mix(x, o_nsa, y_pool, o_gla, w_out, mem_kv, q_norm, wq, q_gain, wo, *, n_seq, t_len, tm):
    d = x.shape[1]
    nt = t_len // tm
    mt = mem_kv.shape[1]
    row = lambda n, j: (n * nt + j, 0)
    fixed = lambda n, j: (0, 0)
    return pl.pallas_call(
        _post_body,
        out_shape=jax.ShapeDtypeStruct(x.shape, F32),
        grid=(n_seq, nt),
        in_specs=[
            pl.BlockSpec((tm, d), row),
            pl.BlockSpec((tm, NSA_Q_W), row),
            pl.BlockSpec((tm, POOL_WIDTH), row),
            pl.BlockSpec((tm, GLA_WIDTH), row),
            pl.BlockSpec(w_out.shape, fixed),
            pl.BlockSpec((None, mt, MEM_WIDTH), lambda n, j: (n, 0, 0)),
            pl.BlockSpec((None, mt, MEM_WIDTH), lambda n, j: (n, 0, 1)),
            pl.BlockSpec((1, d), fixed),
            pl.BlockSpec((d, MEM_WIDTH), fixed),
            pl.BlockSpec((1, HEAD_DIM), fixed),
            pl.BlockSpec((MEM_WIDTH, d), fixed),
        ],
        out_specs=pl.BlockSpec((tm, d), row),
        compiler_params=_cparams(("parallel", "parallel")),
        name="mix_out_mem_attn",
    )(x, o_nsa, y_pool, o_gla, w_out, mem_kv, mem_kv, q_norm, wq, q_gain, wo)


def _cmp_select_body(q_ref, c_ref, oc_ref, idx_ref, *, n_past_blocks):
    n_seq = q_ref.shape[0]
    nbp = n_past_blocks
    width = idx_ref.shape[-1]
    cur = nbp
    q = q_ref[...]
    lane = lax.broadcasted_iota(jnp.int32, (n_seq, width), 1)
    lane_f = lane.astype(F32)
    row_head = lax.broadcasted_iota(jnp.int32, (1, NSA_HEADS, 1), 1)
    o_c = jnp.zeros((n_seq, NSA_HEADS, HEAD_DIM), F32)
    for g in range(NSA_KV_HEADS):
        kc = c_ref[:, g].astype(BF16)
        vc = c_ref[:, NSA_KV_HEADS + g].astype(BF16)
        s = lax.dot_general(q, kc, (((2,), (2,)), ((0,), (0,))), preferred_element_type=F32) * SCALE
        e = jnp.exp(s - jnp.max(s, axis=-1, keepdims=True))
        p = e / jnp.maximum(jnp.sum(e, axis=-1, keepdims=True), 1e-30)
        o_g = lax.dot_general(p.astype(BF16), vc, (((2,), (1,)), ((0,), (0,))), preferred_element_type=F32)
        in_group = row_head // NSA_HPG == g
        o_c = jnp.where(in_group, o_g, o_c)
        imp = jnp.sum(jnp.where(in_group, p, 0.0), axis=1)
        imp = jnp.concatenate([imp, jnp.zeros((n_seq, width - nbp), F32)], axis=1)
        forced = jnp.where(lane == cur, 1, jnp.where(lane == cur - 1, 1, jnp.where(lane == 0, 1, 0)))
        score = jnp.where(lane > cur, PAD_SCORE, jnp.where(forced > 0, FORCE, imp))
        picks = jnp.zeros((n_seq, width), F32)
        for it in range(SEL_TOPK):
            top = jnp.max(score, axis=-1, keepdims=True)
            first = jnp.min(jnp.where(score == top, lane_f, float(width)), axis=-1, keepdims=True)
            picks = jnp.where(lane == it, first, picks)
            score = jnp.where(lane_f == first, PAD_SCORE, score)
        idx_ref[g] = picks.astype(jnp.int32)
    oc_ref[...] = o_c


def _cmp_select(q3, cmpc):
    n_seq, _, nbp, _ = cmpc.shape
    assert nbp + 1 >= SEL_TOPK and nbp % LANE == 0
    idx_w = nbp + LANE
    return pl.pallas_call(
        functools.partial(_cmp_select_body, n_past_blocks=nbp),
        out_shape=(jax.ShapeDtypeStruct((n_seq, NSA_HEADS, HEAD_DIM), F32),
                   jax.ShapeDtypeStruct((NSA_KV_HEADS, n_seq, idx_w), jnp.int32)),
        grid=(1,),
        in_specs=[
            pl.BlockSpec((n_seq, NSA_HEADS, HEAD_DIM), lambda i: (0, 0, 0)),
            pl.BlockSpec(cmpc.shape, lambda i: (0, 0, 0, 0)),
        ],
        out_specs=(pl.BlockSpec((n_seq, NSA_HEADS, HEAD_DIM), lambda i: (0, 0, 0)),
                   pl.BlockSpec((NSA_KV_HEADS, n_seq, idx_w), lambda i: (0, 0, 0))),
        compiler_params=_cparams(("arbitrary",)),
        name="cmp_select",
    )(q3, cmpc)


def _one_token_attn(q, rows, g, k_new, v_new, extra_bias):
    n_rows = rows.shape[0]
    kind = lax.broadcasted_iota(jnp.int32, (1, n_rows), 1) % KV_ROWS
    s = _dot_nt(q, rows) * SCALE + jnp.where(kind == g, 0.0, NEG)
    if extra_bias is not None:
        s = s + extra_bias
    s_new = jnp.sum(q.astype(F32) * k_new, axis=-1, keepdims=True) * SCALE
    top = jnp.maximum(jnp.max(s, axis=-1, keepdims=True), s_new)
    e = jnp.exp(s - top)
    e_new = jnp.exp(s_new - top)
    e_val = pltpu.roll(e, NSA_KV_HEADS, axis=1)
    return (_dot(e_val.astype(BF16), rows) + e_new * v_new) / (jnp.sum(e, axis=-1, keepdims=True) + e_new)


def _sel_win_body(blk_ref, new_ref, *refs):
    del blk_ref
    n = pl.program_id(0)
    nk = SEL_TOPK
    n_sel = NSA_KV_HEADS * nk
    sel_refs = refs[:n_sel]
    win_ref, q_ref, sn_ref, wn_ref, oc_ref, gt_ref, o_ref = refs[n_sel:]
    q = q_ref[...]
    sel_new = sn_ref[...]
    win_new = wn_ref[...]
    gt = gt_ref[...]
    row_head = lax.broadcasted_iota(jnp.int32, (NSA_HEADS, 1), 0)
    blk_rows = SEL_BLOCK * KV_ROWS
    key_blk = lax.broadcasted_iota(jnp.int32, (1, nk * blk_rows), 1) // blk_rows
    win_rows = win_ref[...].astype(BF16)
    o_s = jnp.zeros((NSA_HEADS, HEAD_DIM), F32)
    o_w = jnp.zeros((NSA_HEADS, HEAD_DIM), F32)
    for g in range(NSA_KV_HEADS):
        k0, v0 = g * HEAD_DIM, NSA_KV_W + g * HEAD_DIM
        rows = jnp.concatenate([r[...] for r in sel_refs[g * nk:(g + 1) * nk]], axis=0).astype(BF16)
        bias = jnp.where(key_blk == new_ref[n * NSA_KV_HEADS + g], NEG, 0.0)
        o_sg = _one_token_attn(q, rows, g, sel_new[:, k0:k0 + HEAD_DIM], sel_new[:, v0:v0 + HEAD_DIM], bias)
        o_wg = _one_token_attn(q, win_rows, g, win_new[:, k0:k0 + HEAD_DIM], win_new[:, v0:v0 + HEAD_DIM], None)
        in_group = row_head // NSA_HPG == g
        o_s = jnp.where(in_group, o_sg, o_s)
        o_w = jnp.where(in_group, o_wg, o_w)
    o_ref[...] = (gt[:, 0:1] * oc_ref[...] + gt[:, 1:2] * o_s + gt[:, 2:3] * o_w).astype(BF16)


def _sel_win(blk_rows, new_slot, cache_sel, cache_win, q3, sel_new, win_new, o_c, gates3, layer):
    n_seq = q3.shape[0]
    nk = SEL_TOPK

    def sel_spec(g, k):
        return pl.BlockSpec((None, None, SEL_BLOCK * KV_ROWS, HEAD_DIM),
                            lambda n, br, ns: (layer, br[(n * NSA_KV_HEADS + g) * nk + k], 0, 0))

    sel_specs = [sel_spec(g, k) for g in range(NSA_KV_HEADS) for k in range(nk)]
    win_specs = [pl.BlockSpec((None, None, cache_win.shape[2], HEAD_DIM), lambda n, br, ns: (layer, n, 0, 0))]
    tok = lambda n, br, ns: (n, 0, 0)
    return pl.pallas_call(
        _sel_win_body,
        out_shape=jax.ShapeDtypeStruct((n_seq, NSA_HEADS, HEAD_DIM), BF16),
        grid_spec=pltpu.PrefetchScalarGridSpec(
            num_scalar_prefetch=2,
            grid=(n_seq,),
            in_specs=sel_specs + win_specs + [
                pl.BlockSpec((None, NSA_HEADS, HEAD_DIM), tok),
                pl.BlockSpec((None, 1, 2 * NSA_KV_W), tok),
                pl.BlockSpec((None, 1, 2 * NSA_KV_W), tok),
                pl.BlockSpec((None, NSA_HEADS, HEAD_DIM), tok),
                pl.BlockSpec((None, NSA_HEADS, 3), tok),
            ],
            out_specs=pl.BlockSpec((None, NSA_HEADS, HEAD_DIM), tok),
        ),
        compiler_params=_cparams(("parallel",)),
        name="sel_win_sample",
    )(blk_rows, new_slot, *([cache_sel] * len(sel_specs)), *([cache_win] * len(win_specs)),
      q3, sel_new, win_new, o_c, gates3)


def _pool_sample_body(uc_ref, pw_ref, ps_ref, y_ref):
    rows = uc_ref.shape[0]
    for gi, win in enumerate(POOL_WINDOWS):
        c0, c1 = gi * POOL_GROUP, (gi + 1) * POOL_GROUP
        acc = uc_ref[rows - 1, :, c0:c1]
        for r in range(rows - 2, rows - 1 - win, -1):
            acc = acc + uc_ref[r, :, c0:c1]
        d = acc / float(win) - uc_ref[rows - 1, :, c0:c1]
        y_ref[:, c0:c1] = (_dot(d.astype(BF16), pw_ref[gi].astype(BF16)) * ps_ref[:, c0:c1]).astype(BF16)


def _pool_sample(u_ctx_t, pool_w, pool_scale):
    rows, n_seq, width = u_ctx_t.shape
    assert rows >= max(POOL_WINDOWS)
    return pl.pallas_call(
        _pool_sample_body,
        out_shape=jax.ShapeDtypeStruct((n_seq, width), BF16),
        name="pool_sample",
    )(u_ctx_t, pool_w, pool_scale)


def _gla_sample_body(q_ref, k_ref, v_ref, ga_ref, gr_ref, s_ref, wa2t_ref, ba_ref, gain_ref, og_ref, so_ref):
    z = jnp.sum(wa2t_ref[...] * ga_ref[...], axis=-1, keepdims=True) + ba_ref[...]
    decay = jnp.exp(_log_sigmoid(z) / GLA_TAU)
    v = v_ref[...]
    v_rows = jnp.concatenate(
        [jnp.broadcast_to(v[:, hd * GLA_DV:(hd + 1) * GLA_DV], (GLA_DK, GLA_DV)) for hd in range(GLA_HEADS)], axis=0)
    state = decay * s_ref[...] + k_ref[...] * v_rows
    so_ref[...] = state
    qs = (q_ref[...] * (GLA_DK ** -0.5)) * state
    gr = gr_ref[...]
    for hd in range(GLA_HEADS):
        c0, c1 = hd * GLA_DV, (hd + 1) * GLA_DV
        o = jnp.sum(qs[hd * GLA_DK:(hd + 1) * GLA_DK], axis=0, keepdims=True)
        og_ref[:, c0:c1] = (_rms(o, gain_ref[:, c0:c1]) * _silu(gr[:, c0:c1])).astype(BF16)


def _gla_sample(q_col, k_col, v, ga, gr, state, wa2_t, ba_col, gain):
    n_seq = v.shape[0]
    tok = lambda n: (n, 0, 0)
    fixed = lambda n: (0, 0)
    return pl.pallas_call(
        _gla_sample_body,
        out_shape=(jax.ShapeDtypeStruct((n_seq, 1, GLA_WIDTH), BF16),
                   jax.ShapeDtypeStruct((n_seq, GLA_QK, GLA_DV), F32)),
        grid=(n_seq,),
        in_specs=[
            pl.BlockSpec((None, GLA_QK, 1), tok),
            pl.BlockSpec((None, GLA_QK, 1), tok),
            pl.BlockSpec((None, 1, GLA_WIDTH), tok),
            pl.BlockSpec((None, 1, LANE), tok),
            pl.BlockSpec((None, 1, GLA_WIDTH), tok),
            pl.BlockSpec((None, GLA_QK, GLA_DV), tok),
            pl.BlockSpec((GLA_QK, LANE), fixed),
            pl.BlockSpec((GLA_QK, 1), fixed),
            pl.BlockSpec((1, GLA_WIDTH), fixed),
        ],
        out_specs=(pl.BlockSpec((None, 1, GLA_WIDTH), tok), pl.BlockSpec((None, GLA_QK, GLA_DV), tok)),
        compiler_params=_cparams(("parallel",)),
        name="gla_sample",
    )(q_col, k_col, v, ga, gr, state, wa2_t, ba_col, gain)


def _rope_tables(pos):
    half = HEAD_DIM // 2
    inv = ROPE_THETA ** (-jnp.arange(half, dtype=F32) / half)
    ang = pos.astype(F32)[:, None] * inv[None, :]
    cos, sin = jnp.cos(ang), jnp.sin(ang)
    return jnp.concatenate([cos, cos], axis=1), jnp.concatenate([-sin, sin], axis=1)


def _layer_weights(p, l):
    w_in = p['w_in'][l]
    n_gate = 3 * NSA_HPG
    gate_cols = [_pad_cols(w_in[:, _C_GT + g * n_gate:_C_GT + (g + 1) * n_gate], LANE) for g in range(NSA_KV_HEADS)]
    gate_b = p['nsa_gate_bias'][l]
    ga_cols = _pad_cols(w_in[:, _C_GA:_C_GR], LANE)
    gla_cols = [w_in[:, _C_GQ:_C_GA], w_in[:, _C_GR:_C_END], ga_cols]
    row = lambda a: a.reshape(1, -1)
    return dict(
        layer=l,
        ffn1_norm=row(p['ffn1_norm'][l]), ffn1_gu=p['ffn1_w_gu_b'], ffn1_down=p['ffn1_w_down_b'],
        ffn2_norm=row(p['ffn2_norm'][l]), ffn2_gu=p['ffn2_w_gu_b'], ffn2_down=p['ffn2_w_down_b'],
        mix_norm=row(p['mix_norm'][l]),
        w_qkv=jnp.concatenate([w_in[:, _C_Q:_C_GT]] + gate_cols, axis=1).astype(BF16),
        gate_b=jnp.concatenate([_pad_cols(row(gate_b[g * n_gate:(g + 1) * n_gate]), LANE) for g in range(NSA_KV_HEADS)], axis=1),
        q_gain=row(p['nsa_q_gain'][l]), k_gain=p['nsa_k_gain'][l],
        wk=p['nsa_cmp_wk'][l], wv=p['nsa_cmp_wv'][l],
        w_cmp=jnp.stack([p['nsa_cmp_wk'][l], p['nsa_cmp_wv'][l]]).transpose(2, 0, 1, 3).reshape(-1, HEAD_DIM),
        w_gla_pool=jnp.concatenate(gla_cols + [w_in[:, _C_U:_C_GQ]], axis=1).astype(BF16),
        pool_w=p['pool_w'][l], pool_scale=row(p['pool_scale'][l]),
        wa2=jnp.pad(p['gla_wa2'][l], ((0, LANE - GLA_RANK), (0, 0))).astype(BF16),
        wa2_t=_pad_cols(p['gla_wa2'][l].T, LANE),
        ba=row(p['gla_ba'][l]), ba_col=p['gla_ba'][l].reshape(-1, 1),
        gla_gain=row(p['gla_o_gain'][l]),
        w_out=p['w_out'][l].astype(BF16),
        mem_q_norm=row(p['mem_q_norm'][l]), mem_kv_norm=row(p['mem_kv_norm'][l]),
        mem_wq=p['mem_wq'][l].astype(BF16), mem_wkv=p['mem_wkv'][l].astype(BF16),
        mem_q_gain=row(p['mem_q_gain'][l]), mem_k_gain=row(p['mem_k_gain'][l]),
        mem_wo=p['mem_wo'][l].astype(BF16),
    )


def _prompt_layer(x, mem, w, tables, page_job, *, n_seq, s_len, tm):
    cos, sin = tables
    x, page_sums = _ffn(x, w['ffn1_norm'], w['ffn1_gu'], w['ffn1_down'], w['layer'], tm=min(512, n_seq * s_len),
                        pages=page_job + (w['w_cmp'],))
    q, kv_b, cmp_kv, sel_kv, win_kv, gates = _qkv_proj(
        x, w['mix_norm'], w['w_qkv'], cos, sin, w['q_gain'], w['k_gain'], w['gate_b'], tm=tm, tiles_per_seq=s_len // tm)
    o_nsa = _nsa_prompt(q, gates, kv_b, w['wk'], w['wv'], n_seq=n_seq, s_len=s_len, tq=tm)
    o_gla, state_t, u, y_pool = _gla_pool_prompt(
        x, w['mix_norm'], w['w_gla_pool'], w['wa2'], w['ba'], w['gla_gain'], w['pool_w'], w['pool_scale'],
        n_seq=n_seq, s_len=s_len, tm=tm)
    mkv = _norm_matmul(mem, w['mem_kv_norm'], w['mem_wkv'], w['mem_k_gain'], tm=tm, n_norm_heads=MEM_HEADS)
    mt = mem.shape[0] // n_seq
    x = _post_mix(x, o_nsa, y_pool, o_gla, w['w_out'], mkv.reshape(n_seq, mt, 2 * MEM_WIDTH), w['mem_q_norm'],
                  w['mem_wq'], w['mem_q_gain'], w['mem_wo'], n_seq=n_seq, t_len=s_len, tm=tm)
    x = _ffn(x, w['ffn2_norm'], w['ffn2_gu'], w['ffn2_down'], w['layer'], tm=min(512, n_seq * s_len))
    w_keep = min(WINDOW, s_len)
    kv_shape = (n_seq, s_len, 2, NSA_KV_HEADS, HEAD_DIM)
    state = state_t.reshape(n_seq, GLA_DV, GLA_HEADS, GLA_DK).transpose(0, 2, 3, 1)
    outs = (cmp_kv.reshape(kv_shape), sel_kv.reshape(kv_shape), win_kv.reshape(kv_shape)[:, s_len - w_keep:],
            u.reshape(n_seq, s_len, POOL_WIDTH)[:, s_len - POOL_BUF:], state,
            mkv.reshape(n_seq, mt, 2, MEM_HEADS, HEAD_DIM))
    return x, outs, page_sums


def _post_sample_body(x_ref, on_ref, yp_ref, og_ref, wout_ref, mem_ref, qn_ref, wq_ref, qg_ref, wo_ref, o_ref,
                      x1_ref, q_ref, att_ref):
    n = pl.program_id(0)

    @pl.when(n == 0)
    def _():
        c1, c2 = NSA_Q_W, NSA_Q_W + POOL_WIDTH
        x1 = (x_ref[...] + _dot(on_ref[...], wout_ref[0:c1, :]) + _dot(yp_ref[...], wout_ref[c1:c2, :])
              + _dot(og_ref[...], wout_ref[c2:, :]))
        x1_ref[...] = x1
        q = _dot(_rms(x1, qn_ref[...]).astype(BF16), wq_ref[...])
        for hd in range(MEM_HEADS):
            c0 = hd * HEAD_DIM
            q_ref[:, c0:c0 + HEAD_DIM] = _rms(q[:, c0:c0 + HEAD_DIM], qg_ref[...])

    rows = mem_ref[...].astype(BF16)
    q_row = q_ref[pl.ds(n, 1), :]
    q_heads = jnp.concatenate([q_row[:, hd * HEAD_DIM:(hd + 1) * HEAD_DIM] for hd in range(MEM_HEADS)] * 2, axis=0)
    kind = lax.broadcasted_iota(jnp.int32, (1, rows.shape[0]), 1) % (2 * MEM_HEADS)
    head = lax.broadcasted_iota(jnp.int32, (2 * MEM_HEADS, 1), 0) % MEM_HEADS
    s = _dot_nt(q_heads.astype(BF16), rows) * SCALE + jnp.where(kind == head, 0.0, NEG)
    e = jnp.exp(s - jnp.max(s, axis=-1, keepdims=True))
    e_val = pltpu.roll(e, MEM_HEADS, axis=1)
    o = _dot(e_val.astype(BF16), rows) / jnp.sum(e, axis=-1, keepdims=True)
    att_ref[pl.ds(n, 1), :] = jnp.concatenate([o[hd:hd + 1] for hd in range(MEM_HEADS)], axis=1)

    @pl.when(n == pl.num_programs(0) - 1)
    def _():
        o_ref[...] = x1_ref[...] + _dot(att_ref[...].astype(BF16), wo_ref[...])


def _post_mix_sample(x, o_nsa, y_pool, o_gla, w_out, mem_rows, q_norm, wq, q_gain, wo, layer):
    n_seq, d = x.shape
    fixed = lambda n: (0, 0)
    full = lambda a: pl.BlockSpec(a.shape, fixed)
    return pl.pallas_call(
        _post_sample_body,
        out_shape=jax.ShapeDtypeStruct((n_seq, d), F32),
        grid=(n_seq,),
        in_specs=[full(x), full(o_nsa), full(y_pool), full(o_gla), full(w_out),
                  pl.BlockSpec((None, None) + mem_rows.shape[2:], lambda n: (layer, n, 0, 0)),
                  full(q_norm), full(wq), full(q_gain), full(wo)],
        out_specs=pl.BlockSpec((n_seq, d), fixed),
        scratch_shapes=[pltpu.VMEM((n_seq, d), F32), pltpu.VMEM((n_seq, MEM_WIDTH), F32),
                        pltpu.VMEM((n_seq, MEM_WIDTH), F32)],
        compiler_params=_cparams(("arbitrary",)),
        name="mix_out_mem_attn_sample",
    )(x, o_nsa, y_pool, o_gla, w_out, mem_rows, q_norm, wq, q_gain, wo)


def _sample_layer(x, w, tables, caches, page_sums, page_table, layer):
    cache_sel, cache_win, cache_pool, state_gla, cache_mem = caches
    cos, sin = tables
    n_seq = x.shape[0]
    n_pages = page_table.shape[1]
    bpp = PAGE_SIZE // SEL_BLOCK
    nbp = n_pages * bpp
    kvw = 2 * NSA_KV_W
    x = _ffn(x, w['ffn1_norm'], w['ffn1_gu'], w['ffn1_down'], w['layer'], tm=n_seq)
    q, _, cmp_new, sel_new, win_new, gates = _qkv_proj(
        x, w['mix_norm'], w['w_qkv'], cos, sin, w['q_gain'], w['k_gain'], w['gate_b'], tm=n_seq, tiles_per_seq=1)
    raw = _norm_matmul(x, w['mix_norm'], w['w_gla_pool'], w['q_gain'], tm=n_seq)
    gq, gk = raw[:, :GLA_QK], raw[:, GLA_QK:2 * GLA_QK]
    c0 = 2 * GLA_QK
    gv, gr = raw[:, c0:c0 + GLA_WIDTH], raw[:, c0 + GLA_WIDTH:c0 + 2 * GLA_WIDTH]
    ga, u = raw[:, c0 + 2 * GLA_WIDTH:c0 + 2 * GLA_WIDTH + LANE], raw[:, c0 + 2 * GLA_WIDTH + LANE:]

    q3 = q.reshape(n_seq, NSA_HEADS, HEAD_DIM)
    n_pool = cache_sel.shape[1]
    o_c, idx = _cmp_select(q3, page_sums)
    idx = idx[:, :, :SEL_TOPK].transpose(1, 0, 2)
    pidx = jnp.minimum(idx, nbp - 1)
    phys = jnp.take_along_axis(page_table[:, None, :], pidx // bpp, axis=2)
    blk_rows = (phys * bpp + pidx % bpp).reshape(-1).astype(jnp.int32)
    new_slot = jnp.argmax(idx >= nbp, axis=-1).reshape(-1).astype(jnp.int32)
    gates3 = jnp.concatenate([gates[:, :3 * NSA_HPG], gates[:, LANE:LANE + 3 * NSA_HPG]], axis=1)
    w_buf = cache_win.shape[2]
    o_nsa = _sel_win(blk_rows, new_slot, cache_sel.reshape(-1, n_pool * bpp, SEL_BLOCK * KV_ROWS, HEAD_DIM),
                     cache_win.reshape(-1, n_seq, w_buf * KV_ROWS, HEAD_DIM), q3, sel_new.reshape(n_seq, 1, kvw),
                     win_new.reshape(n_seq, 1, kvw), o_c, gates3.reshape(n_seq, NSA_HEADS, 3), layer)

    u_ctx = jnp.concatenate([cache_pool[layer], u[:, None, :]], axis=1)
    y_pool = _pool_sample(u_ctx.transpose(1, 0, 2), w['pool_w'], w['pool_scale'])
    o_gla, new_state = _gla_sample(
        gq.reshape(n_seq, GLA_QK, 1), gk.reshape(n_seq, GLA_QK, 1), gv.reshape(n_seq, 1, GLA_WIDTH),
        ga.reshape(n_seq, 1, LANE), gr.reshape(n_seq, 1, GLA_WIDTH), state_gla[layer].reshape(n_seq, GLA_QK, GLA_DV),
        w['wa2_t'], w['ba_col'], w['gla_gain'])

    mt = cache_mem.shape[2]
    x = _post_mix_sample(x, o_nsa.reshape(n_seq, NSA_Q_W), y_pool, o_gla.reshape(n_seq, GLA_WIDTH), w['w_out'],
                         cache_mem.reshape(-1, n_seq, mt * 2 * MEM_HEADS, HEAD_DIM), w['mem_q_norm'], w['mem_wq'],
                         w['mem_q_gain'], w['mem_wo'], layer)
    x = _ffn(x, w['ffn2_norm'], w['ffn2_gu'], w['ffn2_down'], w['layer'], tm=n_seq)
    kv_shape = (n_seq, 1, 2, NSA_KV_HEADS, HEAD_DIM)
    win_buf = jnp.concatenate([cache_win[layer], win_new.reshape(kv_shape)], axis=1)[:, 1:]
    outs = (cmp_new.reshape(kv_shape), sel_new.reshape(kv_shape), win_buf, u_ctx[:, 1:],
            new_state.reshape(n_seq, GLA_HEADS, GLA_DK, GLA_DV))
    return x, outs


def kernel(x_prompt, x_sample, cache_cmp_kv, cache_sel_kv, cache_win_kv, cache_pool, state_gla, cache_mem_kv, page_table, mem_prompt, ffn1_norm, ffn1_w_gu, ffn1_w_down, mix_norm, w_in, nsa_q_gain, nsa_k_gain, nsa_gate_bias, nsa_cmp_wk, nsa_cmp_wv, pool_w, pool_scale, gla_wa2, gla_ba, gla_o_gain, w_out, mem_q_norm, mem_kv_norm, mem_wq, mem_wkv, mem_q_gain, mem_k_gain, mem_wo, ffn2_norm, ffn2_w_gu, ffn2_w_down):
    params = dict(ffn1_norm=ffn1_norm, ffn1_w_gu=ffn1_w_gu, ffn1_w_down=ffn1_w_down, mix_norm=mix_norm, w_in=w_in,
                  nsa_q_gain=nsa_q_gain, nsa_k_gain=nsa_k_gain, nsa_gate_bias=nsa_gate_bias, nsa_cmp_wk=nsa_cmp_wk,
                  nsa_cmp_wv=nsa_cmp_wv, pool_w=pool_w, pool_scale=pool_scale, gla_wa2=gla_wa2, gla_ba=gla_ba,
                  gla_o_gain=gla_o_gain, w_out=w_out, mem_q_norm=mem_q_norm, mem_kv_norm=mem_kv_norm, mem_wq=mem_wq,
                  mem_wkv=mem_wkv, mem_q_gain=mem_q_gain, mem_k_gain=mem_k_gain, mem_wo=mem_wo, ffn2_norm=ffn2_norm,
                  ffn2_w_gu=ffn2_w_gu, ffn2_w_down=ffn2_w_down)
    depth = w_in.shape[0]
    n_seq, s_len, d = x_prompt.shape
    n_dec, dec_len, _ = x_sample.shape
    assert dec_len == 1
    past = page_table.shape[1] * PAGE_SIZE
    tables_p = _rope_tables(jnp.arange(s_len))
    tables_s = _rope_tables(jnp.full((n_dec,), past))
    caches = (cache_sel_kv, cache_win_kv, cache_pool, state_gla, cache_mem_kv)
    page_job = (cache_cmp_kv.reshape(depth, -1, PAGE_SIZE * KV_ROWS, HEAD_DIM), page_table)
    xp = x_prompt.reshape(n_seq * s_len, d)
    xs = x_sample.reshape(n_dec, d)
    mem = mem_prompt.reshape(-1, d)
    for name in ('ffn1_w_gu', 'ffn1_w_down', 'ffn2_w_gu', 'ffn2_w_down'):
        params[name + '_b'] = params[name].astype(BF16)
    p_outs, s_outs = [], []
    for l in range(depth):
        w = _layer_weights(params, l)
        xp, po, page_sums = _prompt_layer(xp, mem, w, tables_p, page_job, n_seq=n_seq, s_len=s_len, tm=256)
        xs, so = _sample_layer(xs, w, tables_s, caches, page_sums, page_table, l)
        p_outs.append(po)
        s_outs.append(so)
    stack = lambda outs, k: jnp.stack([o[k] for o in outs])
    return ((xp.reshape(n_seq, s_len, d), xs.reshape(n_dec, 1, d))
            + tuple(stack(p_outs, k) for k in range(6)) + tuple(stack(s_outs, k) for k in range(5)))
```

```python
import functools

import jax
import jax.numpy as jnp
import numpy as np
from jax import lax
from jax.experimental import pallas as pl
from jax.experimental.pallas import tpu as pltpu

F32 = jnp.float32
BF16 = jnp.bfloat16

D_MODEL = 2048
HEAD_DIM = 128
ROPE_THETA = 10000.0
EPS = 1e-6
NEG = -1e30
FORCE = 1e4
SCALE = HEAD_DIM ** -0.5
PAGE_SIZE = 128
NSA_HEADS = 8
NSA_KV_HEADS = 2
NSA_HPG = NSA_HEADS // NSA_KV_HEADS
NSA_Q_W = NSA_HEADS * HEAD_DIM
NSA_KV_W = NSA_KV_HEADS * HEAD_DIM
CMP_BLOCK = 64
SEL_BLOCK = 64
SEL_TOPK = 16
WINDOW = 512
POOL_WIDTH = 512
POOL_WINDOWS = (2, 4, 8, 16)
POOL_GROUP = POOL_WIDTH // len(POOL_WINDOWS)
POOL_BUF = max(POOL_WINDOWS) - 1
POOL_HIST = 16
GLA_WIDTH = 512
GLA_HEADS = 4
GLA_DK = 64
GLA_DV = 128
GLA_RANK = 16
GLA_TAU = 16.0
GLA_CHUNK = 64
GLA_SUB = 16
GLA_QK = GLA_HEADS * GLA_DK
MEM_HEADS = 4
MEM_WIDTH = MEM_HEADS * HEAD_DIM
LANE = 128
PAD_SCORE = -3.0e38

_C_Q = 0
_C_KV = _C_Q + NSA_Q_W
_C_GT = _C_KV + 6 * NSA_KV_W
_C_U = _C_GT + 3 * NSA_HEADS
_C_GQ = _C_U + POOL_WIDTH
_C_GK = _C_GQ + GLA_QK
_C_GV = _C_GK + GLA_QK
_C_GA = _C_GV + GLA_WIDTH
_C_GR = _C_GA + GLA_RANK
_C_END = _C_GR + GLA_WIDTH

VMEM_LIMIT = 48 * 1024 * 1024


def _cparams(sem):
    return pltpu.CompilerParams(dimension_semantics=sem, vmem_limit_bytes=VMEM_LIMIT)


def _dot(a, b):
    return jnp.dot(a, b, preferred_element_type=F32)


def _dot_nt(a, b):
    return lax.dot_general(a, b, (((1,), (1,)), ((), ())), preferred_element_type=F32)


def _rms(x, g):
    return x * lax.rsqrt(jnp.mean(x * x, axis=-1, keepdims=True) + EPS) * g


def _sigmoid(x):
    return 1.0 / (1.0 + jnp.exp(-x))


def _silu(x):
    return x * _sigmoid(x)


def _log_sigmoid(x):
    return jnp.minimum(x, 0.0) - jnp.log1p(jnp.exp(-jnp.abs(x)))


def _rope(y, cos, sin):
    return y * cos + pltpu.roll(y, HEAD_DIM // 2, axis=1) * sin


def _pad_cols(w, width):
    return jnp.pad(w, ((0, 0), (0, width - w.shape[1])))


KV_ROWS = 2 * NSA_KV_HEADS
SUBLANES = 8
PAGES_PER_STEP = 8


def _ffn_step(x_ref, g_ref, wgu_ref, wd_ref, o_ref, xn_ref, side_job=None):
    j = pl.program_id(1)
    tf = wd_ref.shape[0]

    @pl.when(j == 0)
    def _():
        xn_ref[...] = _rms(x_ref[...], g_ref[...]).astype(BF16)
        o_ref[...] = jnp.zeros_like(o_ref)

    if side_job is not None:
        side_job()
    xn = xn_ref[...]
    a = _dot(xn, wgu_ref[:, :tf])
    b = _dot(xn, wgu_ref[:, tf:])
    h = (_silu(a) * b).astype(BF16)
    o_ref[...] += _dot(h, wd_ref[...])

    @pl.when(j == pl.num_programs(1) - 1)
    def _():
        o_ref[...] = x_ref[...] + 0.5 * o_ref[...]


def _ffn_body(x_ref, g_ref, wgu_ref, wd_ref, o_ref, xn_ref):
    _ffn_step(x_ref, g_ref, wgu_ref, wd_ref, o_ref, xn_ref)


def _ffn_cast_body(x_ref, g_ref, wg_ref, wu_ref, wd_ref, o_ref, wgub_ref, wdb_ref, xn_ref):
    tf = wd_ref.shape[0]
    wgub_ref[:, :tf] = wg_ref[...].astype(BF16)
    wgub_ref[:, tf:] = wu_ref[...].astype(BF16)
    wdb_ref[...] = wd_ref[...].astype(BF16)
    _ffn_step(x_ref, g_ref, wgub_ref, wdb_ref, o_ref, xn_ref)


def _ffn_cast(x, g, w_gu, w_down, layer, *, tf=512):
    m, d = x.shape
    f = w_down.shape[1]
    nf = f // tf
    x_spec = pl.BlockSpec((m, d), lambda i, j: (0, 0))
    return pl.pallas_call(
        _ffn_cast_body,
        out_shape=(jax.ShapeDtypeStruct((m, d), F32), jax.ShapeDtypeStruct((d, 2 * f), BF16),
                   jax.ShapeDtypeStruct((f, d), BF16)),
        grid=(1, nf),
        in_specs=[
            x_spec,
            pl.BlockSpec((1, d), lambda i, j: (0, 0)),
            pl.BlockSpec((None, d, tf), lambda i, j: (layer, 0, j)),
            pl.BlockSpec((None, d, tf), lambda i, j: (layer, 0, j + nf)),
            pl.BlockSpec((None, tf, d), lambda i, j: (layer, j, 0)),
        ],
        out_specs=(x_spec, pl.BlockSpec((d, 2 * tf), lambda i, j: (0, j)), pl.BlockSpec((tf, d), lambda i, j: (j, 0))),
        scratch_shapes=[pltpu.VMEM((m, d), BF16)],
        compiler_params=_cparams(("arbitrary", "arbitrary")),
        name="ffn_half_step_casting_weights",
    )(x, g, w_gu, w_gu, w_down)


def _ffn_pages_body(pt_ref, x_ref, g_ref, wgu_ref, wd_ref, *refs):
    del pt_ref
    pages, w_ref, o_ref, c_ref, xn_ref = refs[:PAGES_PER_STEP], *refs[PAGES_PER_STEP:]

    def page_sums():
        w = w_ref[...]
        blk_rows = CMP_BLOCK * KV_ROWS
        sums = []
        for page in pages:
            for half in range(PAGE_SIZE // CMP_BLOCK):
                prod = page[half * blk_rows:(half + 1) * blk_rows, :] * w
                part = jnp.sum(prod.reshape(blk_rows // SUBLANES, SUBLANES, HEAD_DIM), axis=0)
                sums.append(part[0:KV_ROWS] + part[KV_ROWS:2 * KV_ROWS])
        for r in range(KV_ROWS):
            c_ref[r] = jnp.concatenate([s[r:r + 1] for s in sums], axis=0)

    _ffn_step(x_ref, g_ref, wgu_ref, wd_ref, o_ref, xn_ref, side_job=page_sums)


def _ffn(x, g, w_gu, w_down, layer, *, tm, tf=512, pages=None):
    m, d = x.shape
    f = w_down.shape[0]
    nf = f // tf
    grid = (m // tm, nf)
    x_spec = pl.BlockSpec((tm, d), lambda i, j, *_: (i, 0))
    in_specs = [
        x_spec,
        pl.BlockSpec((1, d), lambda i, j, *_: (0, 0)),
        pl.BlockSpec((d, 2 * tf), lambda i, j, *_: (0, j)),
        pl.BlockSpec((tf, d), lambda i, j, *_: (j, 0)),
    ]
    scratch = [pltpu.VMEM((tm, d), BF16)]
    params = _cparams(("parallel", "arbitrary"))
    if pages is None:
        return pl.pallas_call(
            _ffn_body, out_shape=jax.ShapeDtypeStruct((m, d), F32), grid=grid, in_specs=in_specs,
            out_specs=x_spec, scratch_shapes=scratch, compiler_params=params, name="ffn_half_step",
        )(x, g, w_gu, w_down)

    cache, page_table, w_cmp = pages
    n_seq, n_pages = page_table.shape
    bpp = PAGE_SIZE // CMP_BLOCK
    steps_per_seq = n_pages // PAGES_PER_STEP
    n_page_steps = n_seq * steps_per_seq
    assert n_pages % PAGES_PER_STEP == 0 and 2 * KV_ROWS == SUBLANES and n_page_steps <= grid[0] * grid[1]

    def page_step(i, j):
        return jnp.minimum(i * nf + j, n_page_steps - 1)

    def page_spec(k):
        def index(i, j, pt):
            t = page_step(i, j)
            return (layer, pt[t // steps_per_seq, (t % steps_per_seq) * PAGES_PER_STEP + k], 0, 0)
        return pl.BlockSpec((None, None, cache.shape[2], HEAD_DIM), index)

    def sums_index(i, j, pt):
        t = page_step(i, j)
        return (t // steps_per_seq, 0, t % steps_per_seq, 0)

    return pl.pallas_call(
        _ffn_pages_body,
        out_shape=(jax.ShapeDtypeStruct((m, d), F32),
                   jax.ShapeDtypeStruct((n_seq, KV_ROWS, n_pages * bpp, HEAD_DIM), F32)),
        grid_spec=pltpu.PrefetchScalarGridSpec(
            num_scalar_prefetch=1,
            grid=grid,
            in_specs=in_specs + [page_spec(k) for k in range(PAGES_PER_STEP)]
            + [pl.BlockSpec((CMP_BLOCK * KV_ROWS, HEAD_DIM), lambda i, j, pt: (0, 0))],
            out_specs=(x_spec, pl.BlockSpec((None, KV_ROWS, PAGES_PER_STEP * bpp, HEAD_DIM), sums_index)),
            scratch_shapes=scratch,
        ),
        compiler_params=_cparams(("arbitrary", "arbitrary")),
        name="ffn_half_step_with_page_sums",
    )(page_table, x, g, w_gu, w_down, *([cache] * PAGES_PER_STEP), w_cmp)


def _qkv_body(x_ref, g_ref, w_ref, cos_ref, sin_ref, qg_ref, kg_ref, gb_ref,
              q_ref, kvb_ref, cmp_ref, sel_ref, win_ref, gt_ref):
    tm = x_ref.shape[0]
    xn = _rms(x_ref[...], g_ref[...]).astype(BF16)
    h = _dot(xn, w_ref[...])
    cos = cos_ref[...]
    sin = sin_ref[...]

    def head(c0, gain):
        return _rope(_rms(h[:, c0:c0 + HEAD_DIM], gain), cos, sin)

    for hh in range(NSA_HEADS):
        c0 = _C_Q + hh * HEAD_DIM
        q_ref[:, hh * HEAD_DIM:(hh + 1) * HEAD_DIM] = head(c0, qg_ref[...]).astype(BF16)
    for br, out in enumerate((cmp_ref, sel_ref, win_ref)):
        base = _C_KV + br * 2 * NSA_KV_W
        for r in range(KV_ROWS):
            c0 = base + r * HEAD_DIM
            slab = head(c0, kg_ref[br:br + 1, :]) if r < NSA_KV_HEADS else h[:, c0:c0 + HEAD_DIM]
            out[pl.ds(r, tm, stride=KV_ROWS), :] = slab
            kvb_ref[:, c0 - _C_KV:c0 - _C_KV + HEAD_DIM] = slab.astype(BF16)
    gt_ref[...] = _sigmoid(h[:, _C_GT:_C_GT + 2 * LANE] + gb_ref[...])


def _qkv_proj(x, g, w, cos, sin, q_gain, k_gain, gate_b, *, tm, tiles_per_seq):
    m, d = x.shape
    n = w.shape[1]
    row = lambda i: (i, 0)
    fixed = lambda i: (0, 0)
    pos = lambda i: (i % tiles_per_seq, 0)
    return pl.pallas_call(
        _qkv_body,
        out_shape=(
            jax.ShapeDtypeStruct((m, NSA_Q_W), BF16),
            jax.ShapeDtypeStruct((m, 6 * NSA_KV_W), BF16),
            jax.ShapeDtypeStruct((m * KV_ROWS, HEAD_DIM), F32),
            jax.ShapeDtypeStruct((m * KV_ROWS, HEAD_DIM), F32),
            jax.ShapeDtypeStruct((m * KV_ROWS, HEAD_DIM), F32),
            jax.ShapeDtypeStruct((m, 2 * LANE), F32),
        ),
        grid=(m // tm,),
        in_specs=[
            pl.BlockSpec((tm, d), row),
            pl.BlockSpec((1, d), fixed),
            pl.BlockSpec((d, n), fixed),
            pl.BlockSpec((tm, HEAD_DIM), pos),
            pl.BlockSpec((tm, HEAD_DIM), pos),
            pl.BlockSpec((1, HEAD_DIM), fixed),
            pl.BlockSpec((3, HEAD_DIM), fixed),
            pl.BlockSpec((1, 2 * LANE), fixed),
        ],
        out_specs=(
            pl.BlockSpec((tm, NSA_Q_W), row),
            pl.BlockSpec((tm, 6 * NSA_KV_W), row),
            pl.BlockSpec((tm * KV_ROWS, HEAD_DIM), row),
            pl.BlockSpec((tm * KV_ROWS, HEAD_DIM), row),
            pl.BlockSpec((tm * KV_ROWS, HEAD_DIM), row),
            pl.BlockSpec((tm, 2 * LANE), row),
        ),
        compiler_params=_cparams(("parallel",)),
        name="nsa_projection",
    )(x, g, w, cos, sin, q_gain, k_gain, gate_b)


def _softmax_av(s, v):
    e = jnp.exp(s - jnp.max(s, axis=-1, keepdims=True))
    return _dot(e.astype(BF16), v) / jnp.sum(e, axis=-1, keepdims=True)


EXP2_SCALE = SCALE * float(np.log2(np.e))
NSA_PREFIXES = 8


def _nsa_prompt_body(q_ref, gt_ref, kc_ref, vc_ref, ksb_ref, vsb_ref, kwb_ref, vwb_ref, wk_ref, wv_ref, o_ref,
                     kcb_ref, vcb_ref, acc_ref):
    i = pl.program_id(2)
    tq = q_ref.shape[0]
    s_len = kc_ref.shape[0]
    nb = s_len // CMP_BLOCK

    @pl.when(i == 0)
    def _():
        kcf = kc_ref[...].astype(F32).reshape(nb, CMP_BLOCK, HEAD_DIM)
        vcf = vc_ref[...].astype(F32).reshape(nb, CMP_BLOCK, HEAD_DIM)
        kcb_ref[...] = jnp.sum(kcf * wk_ref[...][None], axis=1).astype(BF16)
        vcb_ref[...] = jnp.sum(vcf * wv_ref[...][None], axis=1).astype(BF16)

    qpos = i * tq + lax.broadcasted_iota(jnp.int32, (tq, 1), 0)
    qpos_t = i * tq + lax.broadcasted_iota(jnp.int32, (1, tq), 1)
    kc = kcb_ref[...]
    vc = vcb_ref[...]
    blk = lax.broadcasted_iota(jnp.int32, (1, nb), 1)
    blk_t = lax.broadcasted_iota(jnp.int32, (nb, 1), 0)
    valid_c = ((blk + 1) * CMP_BLOCK - 1) <= qpos
    valid_t = ((blk_t + 1) * CMP_BLOCK - 1) <= qpos_t
    imp_t = jnp.zeros((nb, tq), F32)
    o_c = []
    for hh in range(NSA_HPG):
        qh = q_ref[:, hh * HEAD_DIM:(hh + 1) * HEAD_DIM]
        s = jnp.where(valid_c, _dot_nt(qh, kc) * SCALE, NEG)
        e = jnp.where(valid_c, jnp.exp(s - jnp.max(s, axis=-1, keepdims=True)), 0.0)
        p = e / jnp.maximum(jnp.sum(e, axis=-1, keepdims=True), 1e-30)
        o_c.append(_dot(p.astype(BF16), vc))
        s_t = jnp.where(valid_t, _dot_nt(kc, qh) * SCALE, NEG)
        e_t = jnp.where(valid_t, jnp.exp(s_t - jnp.max(s_t, axis=0, keepdims=True)), 0.0)
        imp_t = imp_t + e_t / jnp.maximum(jnp.sum(e_t, axis=0, keepdims=True), 1e-30)

    cur_t = qpos_t // SEL_BLOCK
    forced = jnp.where(blk_t == cur_t, 1, jnp.where(blk_t == cur_t - 1, 1, jnp.where(blk_t == 0, 1, 0)))
    score = jnp.where(blk_t > cur_t, -FORCE, jnp.where(forced > 0, FORCE, imp_t))
    rank = jnp.zeros((nb, tq), jnp.int32)
    for b2 in range(nb):
        other = score[b2:b2 + 1, :]
        tie = jnp.where(other == score, jnp.where(blk_t > b2, 1, 0), 0)
        rank = rank + jnp.where(other > score, 1, tie)
    keep = jnp.where(rank < min(SEL_TOPK, nb), 1.0, 0.0).T.astype(BF16)

    nt = s_len // tq
    n_prefix = min(NSA_PREFIXES, nt)
    tiles_per_prefix = nt // n_prefix
    for k in range(n_prefix):
        @pl.when(i // tiles_per_prefix == k)
        def _(k=k):
            width = (k + 1) * tiles_per_prefix * tq
            expand = jnp.where(
                lax.broadcasted_iota(jnp.int32, (nb, width), 1) // SEL_BLOCK
                == lax.broadcasted_iota(jnp.int32, (nb, width), 0), 1.0, 0.0).astype(BF16)
            keep_keys = _dot(keep, expand)
            kpos = lax.broadcasted_iota(jnp.int32, (1, width), 1)
            bias = jnp.where(keep_keys > 0.5, jnp.where(kpos <= qpos, 0.0, NEG), NEG)
            ks = ksb_ref[0:width, :]
            vs = vsb_ref[0:width, :]
            for hh in range(NSA_HPG):
                s = _dot_nt(q_ref[:, hh * HEAD_DIM:(hh + 1) * HEAD_DIM], ks) + bias
                e = jnp.exp2((s - jnp.max(s, axis=-1, keepdims=True)) * EXP2_SCALE)
                acc_ref[hh] = _dot(e.astype(BF16), vs) / jnp.sum(e, axis=-1, keepdims=True)

    gt = gt_ref[...]

    def window_and_store(kstart, span):
        kw = kwb_ref[pl.ds(kstart, span), :]
        vw = vwb_ref[pl.ds(kstart, span), :]
        rel = qpos - (kstart + lax.broadcasted_iota(jnp.int32, (1, span), 1))
        bias_w = jnp.where(rel >= 0, jnp.where(rel <= WINDOW, 0.0, NEG), NEG)
        for hh in range(NSA_HPG):
            s = _dot_nt(q_ref[:, hh * HEAD_DIM:(hh + 1) * HEAD_DIM], kw) + bias_w
            e = jnp.exp2((s - jnp.max(s, axis=-1, keepdims=True)) * EXP2_SCALE)
            o_w = _dot(e.astype(BF16), vw) / jnp.sum(e, axis=-1, keepdims=True)
            o = (gt[:, 3 * hh:3 * hh + 1] * o_c[hh] + gt[:, 3 * hh + 1:3 * hh + 2] * acc_ref[hh]
                 + gt[:, 3 * hh + 2:3 * hh + 3] * o_w)
            o_ref[:, hh * HEAD_DIM:(hh + 1) * HEAD_DIM] = o.astype(BF16)

    head_tiles = WINDOW // tq
    for k in range(head_tiles):
        @pl.when(i == k)
        def _(k=k):
            window_and_store(0, (k + 1) * tq)

    @pl.when(i >= head_tiles)
    def _():
        window_and_store(pl.multiple_of(i * tq - WINDOW, tq), WINDOW + tq)


def _nsa_prompt(q, gates, kv, wk, wv, *, n_seq, s_len, tq):
    m = q.shape[0]
    nt = s_len // tq
    nb = s_len // CMP_BLOCK
    assert s_len >= WINDOW + tq and s_len % tq == 0 and tq % SEL_BLOCK == 0 and nt % min(NSA_PREFIXES, nt) == 0
    assert WINDOW % tq == 0
    gw = NSA_HPG * HEAD_DIM
    row = lambda n, g, i: (n * nt + i, g)

    def kv_spec(branch, is_value):
        col = branch * KV_ROWS + is_value * NSA_KV_HEADS
        return pl.BlockSpec((s_len, HEAD_DIM), lambda n, g, i: (n, col + g))

    kv_specs = [kv_spec(br, v) for br in range(3) for v in range(2)]
    wspec = pl.BlockSpec((None, CMP_BLOCK, HEAD_DIM), lambda n, g, i: (g, 0, 0))
    return pl.pallas_call(
        _nsa_prompt_body,
        out_shape=jax.ShapeDtypeStruct((m, NSA_Q_W), BF16),
        grid=(n_seq, NSA_KV_HEADS, nt),
        in_specs=[
            pl.BlockSpec((tq, gw), row),
            pl.BlockSpec((tq, LANE), row),
            *kv_specs, wspec, wspec,
        ],
        out_specs=pl.BlockSpec((tq, gw), row),
        scratch_shapes=[pltpu.VMEM((nb, HEAD_DIM), BF16), pltpu.VMEM((nb, HEAD_DIM), BF16),
                        pltpu.VMEM((NSA_HPG, tq, HEAD_DIM), F32)],
        compiler_params=_cparams(("parallel", "parallel", "arbitrary")),
        name="nsa_prompt",
    )(q, gates, *([kv] * len(kv_specs)), wk, wv)


def _pool_tile(u, t0, hist_ref, pw_ref, ps_ref, y_ref):
    tm = u.shape[0]
    ext = jnp.concatenate([hist_ref[...], u], axis=0)
    t = t0 + lax.broadcasted_iota(jnp.int32, (tm, 1), 0)
    for gi, win in enumerate(POOL_WINDOWS):
        c0, c1 = gi * POOL_GROUP, (gi + 1) * POOL_GROUP
        acc = ext[:, c0:c1]
        span = 1
        while span < win:
            acc = acc + pltpu.roll(acc, span, axis=0)
            span *= 2
        mean = acc[POOL_HIST:] / jnp.minimum(t + 1, win).astype(F32)
        y = _dot((mean - u[:, c0:c1]).astype(BF16), pw_ref[gi].astype(BF16))
        y_ref[:, c0:c1] = (y * ps_ref[:, c0:c1]).astype(BF16)
    hist_ref[...] = u[tm - POOL_HIST:]


def _gla_pool_prompt_body(x_ref, g_ref, w_ref, wa2_ref, ba_ref, gain_ref, pw_ref, ps_ref,
                          og_ref, st_ref, u_ref, y_ref, state_ref, hist_ref):
    j = pl.program_id(1)
    tm = x_ref.shape[0]

    @pl.when(j == 0)
    def _():
        state_ref[...] = jnp.zeros_like(state_ref)
        hist_ref[...] = jnp.zeros_like(hist_ref)

    h = _dot(_rms(x_ref[...], g_ref[...]).astype(BF16), w_ref[...])
    c_k, c_v, c_r, c_a = GLA_QK, 2 * GLA_QK, 2 * GLA_QK + GLA_WIDTH, 2 * GLA_QK + 2 * GLA_WIDTH
    c_u = c_a + LANE
    u = h[:, c_u:c_u + POOL_WIDTH]
    u_ref[...] = u
    _pool_tile(u, j * tm, hist_ref, pw_ref, ps_ref, y_ref)
    z = _dot(h[:, c_a:c_a + LANE].astype(BF16), wa2_ref[...]) + ba_ref[...]
    la_all = _log_sigmoid(z) / GLA_TAU

    lane = lax.broadcasted_iota(jnp.int32, (1, GLA_QK), 1)
    head_mask = [jnp.where(lane // GLA_DK == hd, 1.0, 0.0) for hd in range(GLA_HEADS)]
    rows = lax.broadcasted_iota(jnp.int32, (GLA_CHUNK, 1), 0)
    causal = rows >= lax.broadcasted_iota(jnp.int32, (1, GLA_CHUNK), 1)
    n_sub = GLA_CHUNK // GLA_SUB

    for c in range(tm // GLA_CHUNK):
        r0, r1 = c * GLA_CHUNK, (c + 1) * GLA_CHUNK
        q = h[r0:r1, 0:GLA_QK] * (GLA_DK ** -0.5)
        k = h[r0:r1, c_k:c_k + GLA_QK]
        b = la_all[r0:r1]
        sh = 1
        while sh < GLA_CHUNK:
            b = b + jnp.where(rows >= sh, pltpu.roll(b, sh, axis=0), 0.0)
            sh *= 2
        state = state_ref[...]
        state_b = state.astype(BF16)
        q_in = q * jnp.exp(b)
        a_parts = [[] for _ in range(GLA_HEADS)]
        for sb in range(n_sub):
            s0, s1 = sb * GLA_SUB, (sb + 1) * GLA_SUB
            ref_row = b[s0 - 1:s0] if sb > 0 else jnp.zeros((1, GLA_QK), F32)
            q_t = q[s0:s1] * jnp.exp(b[s0:s1] - ref_row)
            k_t = (k * jnp.exp(jnp.where(rows < s1, ref_row - b, 0.0))).astype(BF16)
            for hd in range(GLA_HEADS):
                a_parts[hd].append(_dot_nt((q_t * head_mask[hd]).astype(BF16), k_t))
        b_last = b[GLA_CHUNK - 1:GLA_CHUNK]
        k_out = k * jnp.exp(b_last - b)
        new_state = state * jnp.exp(b_last)
        for hd in range(GLA_HEADS):
            v = h[r0:r1, c_v + hd * GLA_DV:c_v + (hd + 1) * GLA_DV]
            vb = v.astype(BF16)
            a = jnp.where(causal, jnp.concatenate(a_parts[hd], axis=0), 0.0)
            o = _dot(a.astype(BF16), vb) + _dot_nt((q_in * head_mask[hd]).astype(BF16), state_b)
            gr = h[r0:r1, c_r + hd * GLA_DV:c_r + (hd + 1) * GLA_DV]
            og = _rms(o, gain_ref[:, hd * GLA_DV:(hd + 1) * GLA_DV]) * _silu(gr)
            og_ref[r0:r1, hd * GLA_DV:(hd + 1) * GLA_DV] = og.astype(BF16)
            new_state = new_state + _dot(v.T.astype(BF16), (k_out * head_mask[hd]).astype(BF16))
        state_ref[...] = new_state

    @pl.when(j == pl.num_programs(1) - 1)
    def _():
        st_ref[...] = state_ref[...]


def _gla_pool_prompt(x, g, w, wa2, ba, gain, pool_w, pool_scale, *, n_seq, s_len, tm):
    d = x.shape[1]
    nt = s_len // tm
    rows = n_seq * s_len
    row = lambda n, j: (n * nt + j, 0)
    fixed = lambda n, j: (0, 0)
    return pl.pallas_call(
        _gla_pool_prompt_body,
        out_shape=(jax.ShapeDtypeStruct((rows, GLA_WIDTH), BF16),
                   jax.ShapeDtypeStruct((n_seq, GLA_DV, GLA_QK), F32),
                   jax.ShapeDtypeStruct((rows, POOL_WIDTH), F32),
                   jax.ShapeDtypeStruct((rows, POOL_WIDTH), BF16)),
        grid=(n_seq, nt),
        in_specs=[
            pl.BlockSpec((tm, d), row),
            pl.BlockSpec((1, d), fixed),
            pl.BlockSpec((d, w.shape[1]), fixed),
            pl.BlockSpec((LANE, GLA_QK), fixed),
            pl.BlockSpec((1, GLA_QK), fixed),
            pl.BlockSpec((1, GLA_WIDTH), fixed),
            pl.BlockSpec((len(POOL_WINDOWS), POOL_GROUP, POOL_GROUP), lambda n, j: (0, 0, 0)),
            pl.BlockSpec((1, POOL_WIDTH), fixed),
        ],
        out_specs=(pl.BlockSpec((tm, GLA_WIDTH), row),
                   pl.BlockSpec((None, GLA_DV, GLA_QK), lambda n, j: (n, 0, 0)),
                   pl.BlockSpec((tm, POOL_WIDTH), row), pl.BlockSpec((tm, POOL_WIDTH), row)),
        scratch_shapes=[pltpu.VMEM((GLA_DV, GLA_QK), F32), pltpu.VMEM((POOL_HIST, POOL_WIDTH), F32)],
        compiler_params=_cparams(("parallel", "arbitrary")),
        name="gla_pool_prompt",
    )(x, g, w, wa2, ba, gain, pool_w, pool_scale)


def _norm_matmul_body(x_ref, g_ref, w_ref, hg_ref, o_ref, *, n_norm_heads):
    h = _dot(_rms(x_ref[...], g_ref[...]).astype(BF16), w_ref[...])
    if n_norm_heads:
        for hd in range(n_norm_heads):
            c0 = hd * HEAD_DIM
            o_ref[:, c0:c0 + HEAD_DIM] = _rms(h[:, c0:c0 + HEAD_DIM], hg_ref[...])
        c0 = n_norm_heads * HEAD_DIM
        o_ref[:, c0:] = h[:, c0:]
    else:
        o_ref[...] = h


def _norm_matmul(x, g, w, head_gain, *, tm, n_norm_heads=0):
    m, d = x.shape
    n = w.shape[1]
    return pl.pallas_call(
        functools.partial(_norm_matmul_body, n_norm_heads=n_norm_heads),
        out_shape=jax.ShapeDtypeStruct((m, n), F32),
        grid=(m // tm,),
        in_specs=[
            pl.BlockSpec((tm, d), lambda i: (i, 0)),
            pl.BlockSpec((1, d), lambda i: (0, 0)),
            pl.BlockSpec((d, n), lambda i: (0, 0)),
            pl.BlockSpec((1, HEAD_DIM), lambda i: (0, 0)),
        ],
        out_specs=pl.BlockSpec((tm, n), lambda i: (i, 0)),
        compiler_params=_cparams(("parallel",)),
        name="norm_matmul",
    )(x, g, w, head_gain)


def _post_body(x_ref, on_ref, yp_ref, og_ref, wout_ref, mk_ref, mv_ref, qn_ref, wq_ref, qg_ref, wo_ref, o_ref):
    c1, c2 = NSA_Q_W, NSA_Q_W + POOL_WIDTH
    x1 = (x_ref[...] + _dot(on_ref[...], wout_ref[0:c1, :]) + _dot(yp_ref[...], wout_ref[c1:c2, :])
          + _dot(og_ref[...], wout_ref[c2:, :]))
    q = _dot(_rms(x1, qn_ref[...]).astype(BF16), wq_ref[...])
    outs = []
    for hd in range(MEM_HEADS):
        c0 = hd * HEAD_DIM
        qh = _rms(q[:, c0:c0 + HEAD_DIM], qg_ref[...]).astype(BF16)
        s = _dot_nt(qh, mk_ref[:, c0:c0 + HEAD_DIM].astype(BF16)) * SCALE
        outs.append(_softmax_av(s, mv_ref[:, c0:c0 + HEAD_DIM].astype(BF16)).astype(BF16))
    o_ref[...] = x1 + _dot(jnp.concatenate(outs, axis=1), wo_ref[...])


def _post_mix(x, o_nsa, y_pool, o_gla, w_out, mem_kv, q_norm, wq, q_gain, wo, *, n_seq, t_len, tm):
    d = x.shape[1]
    nt = t_len // tm
    mt = mem_kv.shape[1]
    row = lambda n, j: (n * nt + j, 0)
    fixed = lambda n, j: (0, 0)
    return pl.pallas_call(
        _post_body,
        out_shape=jax.ShapeDtypeStruct(x.shape, F32),
        grid=(n_seq, nt),
        in_specs=[
            pl.BlockSpec((tm, d), row),
            pl.BlockSpec((tm, NSA_Q_W), row),
            pl.BlockSpec((tm, POOL_WIDTH), row),
            pl.BlockSpec((tm, GLA_WIDTH), row),
            pl.BlockSpec(w_out.shape, fixed),
            pl.BlockSpec((None, mt, MEM_WIDTH), lambda n, j: (n, 0, 0)),
            pl.BlockSpec((None, mt, MEM_WIDTH), lambda n, j: (n, 0, 1)),
            pl.BlockSpec((1, d), fixed),
            pl.BlockSpec((d, MEM_WIDTH), fixed),
            pl.BlockSpec((1, HEAD_DIM), fixed),
            pl.BlockSpec((MEM_WIDTH, d), fixed),
        ],
        out_specs=pl.BlockSpec((tm, d), row),
        compiler_params=_cparams(("parallel", "parallel")),
        name="mix_out_mem_attn",
    )(x, o_nsa, y_pool, o_gla, w_out, mem_kv, mem_kv, q_norm, wq, q_gain, wo)


def _cmp_select_body(q_ref, c_ref, oc_ref, idx_ref, *, n_past_blocks):
    n_seq = q_ref.shape[0]
    nbp = n_past_blocks
    width = idx_ref.shape[-1]
    cur = nbp
    q = q_ref[...]
    lane = lax.broadcasted_iota(jnp.int32, (n_seq, width), 1)
    lane_f = lane.astype(F32)
    row_head = lax.broadcasted_iota(jnp.int32, (1, NSA_HEADS, 1), 1)
    o_c = jnp.zeros((n_seq, NSA_HEADS, HEAD_DIM), F32)
    for g in range(NSA_KV_HEADS):
        kc = c_ref[:, g].astype(BF16)
        vc = c_ref[:, NSA_KV_HEADS + g].astype(BF16)
        s = lax.dot_general(q, kc, (((2,), (2,)), ((0,), (0,))), preferred_element_type=F32) * SCALE
        e = jnp.exp(s - jnp.max(s, axis=-1, keepdims=True))
        p = e / jnp.maximum(jnp.sum(e, axis=-1, keepdims=True), 1e-30)
        o_g = lax.dot_general(p.astype(BF16), vc, (((2,), (1,)), ((0,), (0,))), preferred_element_type=F32)
        in_group = row_head // NSA_HPG == g
        o_c = jnp.where(in_group, o_g, o_c)
        imp = jnp.sum(jnp.where(in_group, p, 0.0), axis=1)
        imp = jnp.concatenate([imp, jnp.zeros((n_seq, width - nbp), F32)], axis=1)
        forced = jnp.where(lane == cur, 1, jnp.where(lane == cur - 1, 1, jnp.where(lane == 0, 1, 0)))
        score = jnp.where(lane > cur, PAD_SCORE, jnp.where(forced > 0, FORCE, imp))
        picks = jnp.zeros((n_seq, width), F32)
        for it in range(SEL_TOPK):
            top = jnp.max(score, axis=-1, keepdims=True)
            first = jnp.min(jnp.where(score == top, lane_f, float(width)), axis=-1, keepdims=True)
            picks = jnp.where(lane == it, first, picks)
            score = jnp.where(lane_f == first, PAD_SCORE, score)
        idx_ref[g] = picks.astype(jnp.int32)
    oc_ref[...] = o_c


def _cmp_select(q3, cmpc):
    n_seq, _, nbp, _ = cmpc.shape
    assert nbp + 1 >= SEL_TOPK and nbp % LANE == 0
    idx_w = nbp + LANE
    return pl.pallas_call(
        functools.partial(_cmp_select_body, n_past_blocks=nbp),
        out_shape=(jax.ShapeDtypeStruct((n_seq, NSA_HEADS, HEAD_DIM), F32),
                   jax.ShapeDtypeStruct((NSA_KV_HEADS, n_seq, idx_w), jnp.int32)),
        grid=(1,),
        in_specs=[
            pl.BlockSpec((n_seq, NSA_HEADS, HEAD_DIM), lambda i: (0, 0, 0)),
            pl.BlockSpec(cmpc.shape, lambda i: (0, 0, 0, 0)),
        ],
        out_specs=(pl.BlockSpec((n_seq, NSA_HEADS, HEAD_DIM), lambda i: (0, 0, 0)),
                   pl.BlockSpec((NSA_KV_HEADS, n_seq, idx_w), lambda i: (0, 0, 0))),
        compiler_params=_cparams(("arbitrary",)),
        name="cmp_select",
    )(q3, cmpc)


def _one_token_attn(q, rows, g, k_new, v_new, extra_bias):
    n_rows = rows.shape[0]
    kind = lax.broadcasted_iota(jnp.int32, (1, n_rows), 1) % KV_ROWS
    s = _dot_nt(q, rows) * SCALE + jnp.where(kind == g, 0.0, NEG)
    if extra_bias is not None:
        s = s + extra_bias
    s_new = jnp.sum(q.astype(F32) * k_new, axis=-1, keepdims=True) * SCALE
    top = jnp.maximum(jnp.max(s, axis=-1, keepdims=True), s_new)
    e = jnp.exp(s - top)
    e_new = jnp.exp(s_new - top)
    e_val = pltpu.roll(e, NSA_KV_HEADS, axis=1)
    return (_dot(e_val.astype(BF16), rows) + e_new * v_new) / (jnp.sum(e, axis=-1, keepdims=True) + e_new)


def _sel_win_body(blk_ref, new_ref, *refs):
    del blk_ref
    n = pl.program_id(0)
    nk = SEL_TOPK
    n_sel = NSA_KV_HEADS * nk
    sel_refs = refs[:n_sel]
    win_ref, q_ref, sn_ref, wn_ref, oc_ref, gt_ref, o_ref = refs[n_sel:]
    q = q_ref[...]
    sel_new = sn_ref[...]
    win_new = wn_ref[...]
    gt = gt_ref[...]
    row_head = lax.broadcasted_iota(jnp.int32, (NSA_HEADS, 1), 0)
    blk_rows = SEL_BLOCK * KV_ROWS
    key_blk = lax.broadcasted_iota(jnp.int32, (1, nk * blk_rows), 1) // blk_rows
    win_rows = win_ref[...].astype(BF16)
    o_s = jnp.zeros((NSA_HEADS, HEAD_DIM), F32)
    o_w = jnp.zeros((NSA_HEADS, HEAD_DIM), F32)
    for g in range(NSA_KV_HEADS):
        k0, v0 = g * HEAD_DIM, NSA_KV_W + g * HEAD_DIM
        rows = jnp.concatenate([r[...] for r in sel_refs[g * nk:(g + 1) * nk]], axis=0).astype(BF16)
        bias = jnp.where(key_blk == new_ref[n * NSA_KV_HEADS + g], NEG, 0.0)
        o_sg = _one_token_attn(q, rows, g, sel_new[:, k0:k0 + HEAD_DIM], sel_new[:, v0:v0 + HEAD_DIM], bias)
        o_wg = _one_token_attn(q, win_rows, g, win_new[:, k0:k0 + HEAD_DIM], win_new[:, v0:v0 + HEAD_DIM], None)
        in_group = row_head // NSA_HPG == g
        o_s = jnp.where(in_group, o_sg, o_s)
        o_w = jnp.where(in_group, o_wg, o_w)
    o_ref[...] = (gt[:, 0:1] * oc_ref[...] + gt[:, 1:2] * o_s + gt[:, 2:3] * o_w).astype(BF16)


def _sel_win(blk_rows, new_slot, cache_sel, cache_win, q3, sel_new, win_new, o_c, gates3, layer):
    n_seq = q3.shape[0]
    nk = SEL_TOPK

    def sel_spec(g, k):
        return pl.BlockSpec((None, None, SEL_BLOCK * KV_ROWS, HEAD_DIM),
                            lambda n, br, ns: (layer, br[(n * NSA_KV_HEADS + g) * nk + k], 0, 0))

    sel_specs = [sel_spec(g, k) for g in range(NSA_KV_HEADS) for k in range(nk)]
    win_specs = [pl.BlockSpec((None, None, cache_win.shape[2], HEAD_DIM), lambda n, br, ns: (layer, n, 0, 0))]
    tok = lambda n, br, ns: (n, 0, 0)
    return pl.pallas_call(
        _sel_win_body,
        out_shape=jax.ShapeDtypeStruct((n_seq, NSA_HEADS, HEAD_DIM), BF16),
        grid_spec=pltpu.PrefetchScalarGridSpec(
            num_scalar_prefetch=2,
            grid=(n_seq,),
            in_specs=sel_specs + win_specs + [
                pl.BlockSpec((None, NSA_HEADS, HEAD_DIM), tok),
                pl.BlockSpec((None, 1, 2 * NSA_KV_W), tok),
                pl.BlockSpec((None, 1, 2 * NSA_KV_W), tok),
                pl.BlockSpec((None, NSA_HEADS, HEAD_DIM), tok),
                pl.BlockSpec((None, NSA_HEADS, 3), tok),
            ],
            out_specs=pl.BlockSpec((None, NSA_HEADS, HEAD_DIM), tok),
        ),
        compiler_params=_cparams(("parallel",)),
        name="sel_win_sample",
    )(blk_rows, new_slot, *([cache_sel] * len(sel_specs)), *([cache_win] * len(win_specs)),
      q3, sel_new, win_new, o_c, gates3)


def _pool_sample_body(uc_ref, pw_ref, ps_ref, y_ref):
    rows = uc_ref.shape[0]
    for gi, win in enumerate(POOL_WINDOWS):
        c0, c1 = gi * POOL_GROUP, (gi + 1) * POOL_GROUP
        acc = uc_ref[rows - 1, :, c0:c1]
        for r in range(rows - 2, rows - 1 - win, -1):
            acc = acc + uc_ref[r, :, c0:c1]
        d = acc / float(win) - uc_ref[rows - 1, :, c0:c1]
        y_ref[:, c0:c1] = (_dot(d.astype(BF16), pw_ref[gi].astype(BF16)) * ps_ref[:, c0:c1]).astype(BF16)


def _pool_sample(u_ctx_t, pool_w, pool_scale):
    rows, n_seq, width = u_ctx_t.shape
    assert rows >= max(POOL_WINDOWS)
    return pl.pallas_call(
        _pool_sample_body,
        out_shape=jax.ShapeDtypeStruct((n_seq, width), BF16),
        name="pool_sample",
    )(u_ctx_t, pool_w, pool_scale)


def _gla_sample_body(q_ref, k_ref, v_ref, ga_ref, gr_ref, s_ref, wa2t_ref, ba_ref, gain_ref, og_ref, so_ref):
    z = jnp.sum(wa2t_ref[...] * ga_ref[...], axis=-1, keepdims=True) + ba_ref[...]
    decay = jnp.exp(_log_sigmoid(z) / GLA_TAU)
    v = v_ref[...]
    v_rows = jnp.concatenate(
        [jnp.broadcast_to(v[:, hd * GLA_DV:(hd + 1) * GLA_DV], (GLA_DK, GLA_DV)) for hd in range(GLA_HEADS)], axis=0)
    state = decay * s_ref[...] + k_ref[...] * v_rows
    so_ref[...] = state
    qs = (q_ref[...] * (GLA_DK ** -0.5)) * state
    gr = gr_ref[...]
    for hd in range(GLA_HEADS):
        c0, c1 = hd * GLA_DV, (hd + 1) * GLA_DV
        o = jnp.sum(qs[hd * GLA_DK:(hd + 1) * GLA_DK], axis=0, keepdims=True)
        og_ref[:, c0:c1] = (_rms(o, gain_ref[:, c0:c1]) * _silu(gr[:, c0:c1])).astype(BF16)


def _gla_sample(q_col, k_col, v, ga, gr, state, wa2_t, ba_col, gain):
    n_seq = v.shape[0]
    tok = lambda n: (n, 0, 0)
    fixed = lambda n: (0, 0)
    return pl.pallas_call(
        _gla_sample_body,
        out_shape=(jax.ShapeDtypeStruct((n_seq, 1, GLA_WIDTH), BF16),
                   jax.ShapeDtypeStruct((n_seq, GLA_QK, GLA_DV), F32)),
        grid=(n_seq,),
        in_specs=[
            pl.BlockSpec((None, GLA_QK, 1), tok),
            pl.BlockSpec((None, GLA_QK, 1), tok),
            pl.BlockSpec((None, 1, GLA_WIDTH), tok),
            pl.BlockSpec((None, 1, LANE), tok),
            pl.BlockSpec((None, 1, GLA_WIDTH), tok),
            pl.BlockSpec((None, GLA_QK, GLA_DV), tok),
            pl.BlockSpec((GLA_QK, LANE), fixed),
            pl.BlockSpec((GLA_QK, 1), fixed),
            pl.BlockSpec((1, GLA_WIDTH), fixed),
        ],
        out_specs=(pl.BlockSpec((None, 1, GLA_WIDTH), tok), pl.BlockSpec((None, GLA_QK, GLA_DV), tok)),
        compiler_params=_cparams(("parallel",)),
        name="gla_sample",
    )(q_col, k_col, v, ga, gr, state, wa2_t, ba_col, gain)


def _rope_tables(pos):
    half = HEAD_DIM // 2
    inv = ROPE_THETA ** (-jnp.arange(half, dtype=F32) / half)
    ang = pos.astype(F32)[:, None] * inv[None, :]
    cos, sin = jnp.cos(ang), jnp.sin(ang)
    return jnp.concatenate([cos, cos], axis=1), jnp.concatenate([-sin, sin], axis=1)


def _layer_weights(p, l):
    w_in = p['w_in'][l]
    n_gate = 3 * NSA_HPG
    gate_cols = [_pad_cols(w_in[:, _C_GT + g * n_gate:_C_GT + (g + 1) * n_gate], LANE) for g in range(NSA_KV_HEADS)]
    gate_b = p['nsa_gate_bias'][l]
    ga_cols = _pad_cols(w_in[:, _C_GA:_C_GR], LANE)
    gla_cols = [w_in[:, _C_GQ:_C_GA], w_in[:, _C_GR:_C_END], ga_cols]
    row = lambda a: a.reshape(1, -1)
    return dict(
        ffn1_norm=row(p['ffn1_norm'][l]), ffn2_norm=row(p['ffn2_norm'][l]),
        mix_norm=row(p['mix_norm'][l]),
        w_qkv=jnp.concatenate([w_in[:, _C_Q:_C_GT]] + gate_cols, axis=1).astype(BF16),
        gate_b=jnp.concatenate([_pad_cols(row(gate_b[g * n_gate:(g + 1) * n_gate]), LANE) for g in range(NSA_KV_HEADS)], axis=1),
        q_gain=row(p['nsa_q_gain'][l]), k_gain=p['nsa_k_gain'][l],
        wk=p['nsa_cmp_wk'][l], wv=p['nsa_cmp_wv'][l],
        w_cmp=jnp.stack([p['nsa_cmp_wk'][l], p['nsa_cmp_wv'][l]]).transpose(2, 0, 1, 3).reshape(-1, HEAD_DIM),
        w_gla_pool=jnp.concatenate(gla_cols + [w_in[:, _C_U:_C_GQ]], axis=1).astype(BF16),
        pool_w=p['pool_w'][l], pool_scale=row(p['pool_scale'][l]),
        wa2=jnp.pad(p['gla_wa2'][l], ((0, LANE - GLA_RANK), (0, 0))).astype(BF16),
        wa2_t=_pad_cols(p['gla_wa2'][l].T, LANE),
        ba=row(p['gla_ba'][l]), ba_col=p['gla_ba'][l].reshape(-1, 1),
        gla_gain=row(p['gla_o_gain'][l]),
        w_out=p['w_out'][l].astype(BF16),
        mem_q_norm=row(p['mem_q_norm'][l]), mem_kv_norm=row(p['mem_kv_norm'][l]),
        mem_wq=p['mem_wq'][l].astype(BF16), mem_wkv=p['mem_wkv'][l].astype(BF16),
        mem_q_gain=row(p['mem_q_gain'][l]), mem_k_gain=row(p['mem_k_gain'][l]),
        mem_wo=p['mem_wo'][l].astype(BF16),
    )


def _prompt_mixer(x, mem, w, tables, *, n_seq, s_len, tm):
    cos, sin = tables
    q, kv_b, cmp_kv, sel_kv, win_kv, gates = _qkv_proj(
        x, w['mix_norm'], w['w_qkv'], cos, sin, w['q_gain'], w['k_gain'], w['gate_b'], tm=tm, tiles_per_seq=s_len // tm)
    o_nsa = _nsa_prompt(q, gates, kv_b, w['wk'], w['wv'], n_seq=n_seq, s_len=s_len, tq=tm)
    o_gla, state_t, u, y_pool = _gla_pool_prompt(
        x, w['mix_norm'], w['w_gla_pool'], w['wa2'], w['ba'], w['gla_gain'], w['pool_w'], w['pool_scale'],
        n_seq=n_seq, s_len=s_len, tm=tm)
    mkv = _norm_matmul(mem, w['mem_kv_norm'], w['mem_wkv'], w['mem_k_gain'], tm=tm, n_norm_heads=MEM_HEADS)
    mt = mem.shape[0] // n_seq
    x = _post_mix(x, o_nsa, y_pool, o_gla, w['w_out'], mkv.reshape(n_seq, mt, 2 * MEM_WIDTH), w['mem_q_norm'],
                  w['mem_wq'], w['mem_q_gain'], w['mem_wo'], n_seq=n_seq, t_len=s_len, tm=tm)
    w_keep = min(WINDOW, s_len)
    kv_shape = (n_seq, s_len, 2, NSA_KV_HEADS, HEAD_DIM)
    state = state_t.reshape(n_seq, GLA_DV, GLA_HEADS, GLA_DK).transpose(0, 2, 3, 1)
    outs = (cmp_kv.reshape(kv_shape), sel_kv.reshape(kv_shape), win_kv.reshape(kv_shape)[:, s_len - w_keep:],
            u.reshape(n_seq, s_len, POOL_WIDTH)[:, s_len - POOL_BUF:], state,
            mkv.reshape(n_seq, mt, 2, MEM_HEADS, HEAD_DIM))
    return x, outs


def _post_sample_body(x_ref, on_ref, yp_ref, og_ref, wout_ref, mem_ref, qn_ref, wq_ref, qg_ref, wo_ref, o_ref,
                      x1_ref, q_ref, att_ref):
    n = pl.program_id(0)

    @pl.when(n == 0)
    def _():
        c1, c2 = NSA_Q_W, NSA_Q_W + POOL_WIDTH
        x1 = (x_ref[...] + _dot(on_ref[...], wout_ref[0:c1, :]) + _dot(yp_ref[...], wout_ref[c1:c2, :])
              + _dot(og_ref[...], wout_ref[c2:, :]))
        x1_ref[...] = x1
        q = _dot(_rms(x1, qn_ref[...]).astype(BF16), wq_ref[...])
        for hd in range(MEM_HEADS):
            c0 = hd * HEAD_DIM
            q_ref[:, c0:c0 + HEAD_DIM] = _rms(q[:, c0:c0 + HEAD_DIM], qg_ref[...])

    rows = mem_ref[...].astype(BF16)
    q_row = q_ref[pl.ds(n, 1), :]
    q_heads = jnp.concatenate([q_row[:, hd * HEAD_DIM:(hd + 1) * HEAD_DIM] for hd in range(MEM_HEADS)] * 2, axis=0)
    kind = lax.broadcasted_iota(jnp.int32, (1, rows.shape[0]), 1) % (2 * MEM_HEADS)
    head = lax.broadcasted_iota(jnp.int32, (2 * MEM_HEADS, 1), 0) % MEM_HEADS
    s = _dot_nt(q_heads.astype(BF16), rows) * SCALE + jnp.where(kind == head, 0.0, NEG)
    e = jnp.exp(s - jnp.max(s, axis=-1, keepdims=True))
    e_val = pltpu.roll(e, MEM_HEADS, axis=1)
    o = _dot(e_val.astype(BF16), rows) / jnp.sum(e, axis=-1, keepdims=True)
    att_ref[pl.ds(n, 1), :] = jnp.concatenate([o[hd:hd + 1] for hd in range(MEM_HEADS)], axis=1)

    @pl.when(n == pl.num_programs(0) - 1)
    def _():
        o_ref[...] = x1_ref[...] + _dot(att_ref[...].astype(BF16), wo_ref[...])


def _post_mix_sample(x, o_nsa, y_pool, o_gla, w_out, mem_rows, q_norm, wq, q_gain, wo, layer):
    n_seq, d = x.shape
    fixed = lambda n: (0, 0)
    full = lambda a: pl.BlockSpec(a.shape, fixed)
    return pl.pallas_call(
        _post_sample_body,
        out_shape=jax.ShapeDtypeStruct((n_seq, d), F32),
        grid=(n_seq,),
        in_specs=[full(x), full(o_nsa), full(y_pool), full(o_gla), full(w_out),
                  pl.BlockSpec((None, None) + mem_rows.shape[2:], lambda n: (layer, n, 0, 0)),
                  full(q_norm), full(wq), full(q_gain), full(wo)],
        out_specs=pl.BlockSpec((n_seq, d), fixed),
        scratch_shapes=[pltpu.VMEM((n_seq, d), F32), pltpu.VMEM((n_seq, MEM_WIDTH), F32),
                        pltpu.VMEM((n_seq, MEM_WIDTH), F32)],
        compiler_params=_cparams(("arbitrary",)),
        name="mix_out_mem_attn_sample",
    )(x, o_nsa, y_pool, o_gla, w_out, mem_rows, q_norm, wq, q_gain, wo)


def _sample_mixer(x, w, tables, caches, page_sums, page_table, layer):
    cache_sel, cache_win, cache_pool, state_gla, cache_mem = caches
    cos, sin = tables
    n_seq = x.shape[0]
    n_pages = page_table.shape[1]
    bpp = PAGE_SIZE // SEL_BLOCK
    nbp = n_pages * bpp
    kvw = 2 * NSA_KV_W
    q, _, cmp_new, sel_new, win_new, gates = _qkv_proj(
        x, w['mix_norm'], w['w_qkv'], cos, sin, w['q_gain'], w['k_gain'], w['gate_b'], tm=n_seq, tiles_per_seq=1)
    raw = _norm_matmul(x, w['mix_norm'], w['w_gla_pool'], w['q_gain'], tm=n_seq)
    gq, gk = raw[:, :GLA_QK], raw[:, GLA_QK:2 * GLA_QK]
    c0 = 2 * GLA_QK
    gv, gr = raw[:, c0:c0 + GLA_WIDTH], raw[:, c0 + GLA_WIDTH:c0 + 2 * GLA_WIDTH]
    ga, u = raw[:, c0 + 2 * GLA_WIDTH:c0 + 2 * GLA_WIDTH + LANE], raw[:, c0 + 2 * GLA_WIDTH + LANE:]

    q3 = q.reshape(n_seq, NSA_HEADS, HEAD_DIM)
    n_pool = cache_sel.shape[1]
    o_c, idx = _cmp_select(q3, page_sums)
    idx = idx[:, :, :SEL_TOPK].transpose(1, 0, 2)
    pidx = jnp.minimum(idx, nbp - 1)
    phys = jnp.take_along_axis(page_table[:, None, :], pidx // bpp, axis=2)
    blk_rows = (phys * bpp + pidx % bpp).reshape(-1).astype(jnp.int32)
    new_slot = jnp.argmax(idx >= nbp, axis=-1).reshape(-1).astype(jnp.int32)
    gates3 = jnp.concatenate([gates[:, :3 * NSA_HPG], gates[:, LANE:LANE + 3 * NSA_HPG]], axis=1)
    w_buf = cache_win.shape[2]
    o_nsa = _sel_win(blk_rows, new_slot, cache_sel.reshape(-1, n_pool * bpp, SEL_BLOCK * KV_ROWS, HEAD_DIM),
                     cache_win.reshape(-1, n_seq, w_buf * KV_ROWS, HEAD_DIM), q3, sel_new.reshape(n_seq, 1, kvw),
                     win_new.reshape(n_seq, 1, kvw), o_c, gates3.reshape(n_seq, NSA_HEADS, 3), layer)

    u_ctx = jnp.concatenate([cache_pool[layer], u[:, None, :]], axis=1)
    y_pool = _pool_sample(u_ctx.transpose(1, 0, 2), w['pool_w'], w['pool_scale'])
    o_gla, new_state = _gla_sample(
        gq.reshape(n_seq, GLA_QK, 1), gk.reshape(n_seq, GLA_QK, 1), gv.reshape(n_seq, 1, GLA_WIDTH),
        ga.reshape(n_seq, 1, LANE), gr.reshape(n_seq, 1, GLA_WIDTH), state_gla[layer].reshape(n_seq, GLA_QK, GLA_DV),
        w['wa2_t'], w['ba_col'], w['gla_gain'])

    mt = cache_mem.shape[2]
    x = _post_mix_sample(x, o_nsa.reshape(n_seq, NSA_Q_W), y_pool, o_gla.reshape(n_seq, GLA_WIDTH), w['w_out'],
                         cache_mem.reshape(-1, n_seq, mt * 2 * MEM_HEADS, HEAD_DIM), w['mem_q_norm'], w['mem_wq'],
                         w['mem_q_gain'], w['mem_wo'], layer)
    kv_shape = (n_seq, 1, 2, NSA_KV_HEADS, HEAD_DIM)
    outs = (cmp_new.reshape(kv_shape), sel_new.reshape(kv_shape), win_new.reshape(kv_shape), u_ctx[:, 1:],
            new_state.reshape(n_seq, GLA_HEADS, GLA_DK, GLA_DV))
    return x, outs


def kernel(x_prompt, x_sample, cache_cmp_kv, cache_sel_kv, cache_win_kv, cache_pool, state_gla, cache_mem_kv, page_table, mem_prompt, ffn1_norm, ffn1_w_gu, ffn1_w_down, mix_norm, w_in, nsa_q_gain, nsa_k_gain, nsa_gate_bias, nsa_cmp_wk, nsa_cmp_wv, pool_w, pool_scale, gla_wa2, gla_ba, gla_o_gain, w_out, mem_q_norm, mem_kv_norm, mem_wq, mem_wkv, mem_q_gain, mem_k_gain, mem_wo, ffn2_norm, ffn2_w_gu, ffn2_w_down):
    params = dict(ffn1_norm=ffn1_norm, ffn1_w_gu=ffn1_w_gu, ffn1_w_down=ffn1_w_down, mix_norm=mix_norm, w_in=w_in,
                  nsa_q_gain=nsa_q_gain, nsa_k_gain=nsa_k_gain, nsa_gate_bias=nsa_gate_bias, nsa_cmp_wk=nsa_cmp_wk,
                  nsa_cmp_wv=nsa_cmp_wv, pool_w=pool_w, pool_scale=pool_scale, gla_wa2=gla_wa2, gla_ba=gla_ba,
                  gla_o_gain=gla_o_gain, w_out=w_out, mem_q_norm=mem_q_norm, mem_kv_norm=mem_kv_norm, mem_wq=mem_wq,
                  mem_wkv=mem_wkv, mem_q_gain=mem_q_gain, mem_k_gain=mem_k_gain, mem_wo=mem_wo, ffn2_norm=ffn2_norm,
                  ffn2_w_gu=ffn2_w_gu, ffn2_w_down=ffn2_w_down)
    depth = w_in.shape[0]
    n_seq, s_len, d = x_prompt.shape
    n_dec, dec_len, _ = x_sample.shape
    assert dec_len == 1
    past = page_table.shape[1] * PAGE_SIZE
    tables_p = _rope_tables(jnp.arange(s_len))
    tables_s = _rope_tables(jnp.full((n_dec,), past))
    caches = (cache_sel_kv, cache_win_kv, cache_pool, state_gla, cache_mem_kv)
    page_job = (cache_cmp_kv.reshape(depth, -1, PAGE_SIZE * KV_ROWS, HEAD_DIM), page_table)
    xp = x_prompt.reshape(n_seq * s_len, d)
    xs = x_sample.reshape(n_dec, d)
    mem = mem_prompt.reshape(-1, d)
    ffn_tm = min(512, n_seq * s_len)
    p_outs, s_outs = [], []
    for l in range(depth):
        w = _layer_weights(params, l)
        xs, *ffn1_w = _ffn_cast(xs, w['ffn1_norm'], ffn1_w_gu, ffn1_w_down, l)
        xp, page_sums = _ffn(xp, w['ffn1_norm'], *ffn1_w, l, tm=ffn_tm, pages=page_job + (w['w_cmp'],))
        xp, po = _prompt_mixer(xp, mem, w, tables_p, n_seq=n_seq, s_len=s_len, tm=256)
        xs, so = _sample_mixer(xs, w, tables_s, caches, page_sums, page_table, l)
        xs, *ffn2_w = _ffn_cast(xs, w['ffn2_norm'], ffn2_w_gu, ffn2_w_down, l)
        xp = _ffn(xp, w['ffn2_norm'], *ffn2_w, l, tm=ffn_tm)
        p_outs.append(po)
        s_outs.append(so)
    stack = lambda outs, k: jnp.stack([o[k] for o in outs])
    return ((xp.reshape(n_seq, s_len, d), xs.reshape(n_dec, 1, d))
            + tuple(stack(p_outs, k) for k in range(6)) + tuple(stack(s_outs, k) for k in range(2))
            + (jnp.concatenate([cache_win_kv[:, :, dec_len:], stack(s_outs, 2)], axis=2),)
            + tuple(stack(s_outs, k) for k in range(3, 5)))
```

```python
import functools

import jax
import jax.numpy as jnp
import numpy as np
from jax import lax
from jax.experimental import pallas as pl
from jax.experimental.pallas import tpu as pltpu

F32 = jnp.float32
BF16 = jnp.bfloat16

D_MODEL = 2048
HEAD_DIM = 128
ROPE_THETA = 10000.0
EPS = 1e-6
NEG = -1e30
FORCE = 1e4
SCALE = HEAD_DIM ** -0.5
PAGE_SIZE = 128
NSA_HEADS = 8
NSA_KV_HEADS = 2
NSA_HPG = NSA_HEADS // NSA_KV_HEADS
NSA_Q_W = NSA_HEADS * HEAD_DIM
NSA_KV_W = NSA_KV_HEADS * HEAD_DIM
CMP_BLOCK = 64
SEL_BLOCK = 64
SEL_TOPK = 16
WINDOW = 512
POOL_WIDTH = 512
POOL_WINDOWS = (2, 4, 8, 16)
POOL_GROUP = POOL_WIDTH // len(POOL_WINDOWS)
POOL_BUF = max(POOL_WINDOWS) - 1
POOL_HIST = 16
GLA_WIDTH = 512
GLA_HEADS = 4
GLA_DK = 64
GLA_DV = 128
GLA_RANK = 16
GLA_TAU = 16.0
GLA_CHUNK = 64
GLA_SUB = 16
GLA_QK = GLA_HEADS * GLA_DK
MEM_HEADS = 4
MEM_WIDTH = MEM_HEADS * HEAD_DIM
LANE = 128
PAD_SCORE = -3.0e38

_C_Q = 0
_C_KV = _C_Q + NSA_Q_W
_C_GT = _C_KV + 6 * NSA_KV_W
_C_U = _C_GT + 3 * NSA_HEADS
_C_GQ = _C_U + POOL_WIDTH
_C_GK = _C_GQ + GLA_QK
_C_GV = _C_GK + GLA_QK
_C_GA = _C_GV + GLA_WIDTH
_C_GR = _C_GA + GLA_RANK
_C_END = _C_GR + GLA_WIDTH

VMEM_LIMIT = 48 * 1024 * 1024


def _cparams(sem):
    return pltpu.CompilerParams(dimension_semantics=sem, vmem_limit_bytes=VMEM_LIMIT)


def _dot(a, b):
    return jnp.dot(a, b, preferred_element_type=F32)


def _dot_nt(a, b):
    return lax.dot_general(a, b, (((1,), (1,)), ((), ())), preferred_element_type=F32)


def _rms(x, g):
    return x * lax.rsqrt(jnp.mean(x * x, axis=-1, keepdims=True) + EPS) * g


def _sigmoid(x):
    return 1.0 / (1.0 + jnp.exp(-x))


def _silu(x):
    return x * _sigmoid(x)


def _log_sigmoid(x):
    return jnp.minimum(x, 0.0) - jnp.log1p(jnp.exp(-jnp.abs(x)))


def _rope(y, cos, sin):
    return y * cos + pltpu.roll(y, HEAD_DIM // 2, axis=1) * sin


def _pad_cols(w, width):
    return jnp.pad(w, ((0, 0), (0, width - w.shape[1])))


KV_ROWS = 2 * NSA_KV_HEADS
SUBLANES = 8
PAGES_PER_STEP = 8


def _ffn_step(x_ref, g_ref, wgu_ref, wd_ref, o_ref, xn_ref, side_job=None):
    j = pl.program_id(1)
    tf = wd_ref.shape[0]

    @pl.when(j == 0)
    def _():
        xn_ref[...] = _rms(x_ref[...], g_ref[...]).astype(BF16)
        o_ref[...] = jnp.zeros_like(o_ref)

    if side_job is not None:
        side_job()
    xn = xn_ref[...]
    a = _dot(xn, wgu_ref[:, :tf])
    b = _dot(xn, wgu_ref[:, tf:])
    h = (_silu(a) * b).astype(BF16)
    o_ref[...] += _dot(h, wd_ref[...])

    @pl.when(j == pl.num_programs(1) - 1)
    def _():
        o_ref[...] = x_ref[...] + 0.5 * o_ref[...]


def _ffn_body(x_ref, g_ref, wgu_ref, wd_ref, o_ref, xn_ref):
    _ffn_step(x_ref, g_ref, wgu_ref, wd_ref, o_ref, xn_ref)


def _ffn_cast_body(x_ref, g_ref, wg_ref, wu_ref, wd_ref, o_ref, wgub_ref, wdb_ref, xn_ref):
    tf = wd_ref.shape[0]
    wgub_ref[:, :tf] = wg_ref[...].astype(BF16)
    wgub_ref[:, tf:] = wu_ref[...].astype(BF16)
    wdb_ref[...] = wd_ref[...].astype(BF16)
    _ffn_step(x_ref, g_ref, wgub_ref, wdb_ref, o_ref, xn_ref)


def _ffn_cast(x, g, w_gu, w_down, layer, *, tf=512):
    m, d = x.shape
    f = w_down.shape[1]
    nf = f // tf
    x_spec = pl.BlockSpec((m, d), lambda i, j: (0, 0))
    return pl.pallas_call(
        _ffn_cast_body,
        out_shape=(jax.ShapeDtypeStruct((m, d), F32), jax.ShapeDtypeStruct((d, 2 * f), BF16),
                   jax.ShapeDtypeStruct((f, d), BF16)),
        grid=(1, nf),
        in_specs=[
            x_spec,
            pl.BlockSpec((1, d), lambda i, j: (0, 0)),
            pl.BlockSpec((None, d, tf), lambda i, j: (layer, 0, j)),
            pl.BlockSpec((None, d, tf), lambda i, j: (layer, 0, j + nf)),
            pl.BlockSpec((None, tf, d), lambda i, j: (layer, j, 0)),
        ],
        out_specs=(x_spec, pl.BlockSpec((d, 2 * tf), lambda i, j: (0, j)), pl.BlockSpec((tf, d), lambda i, j: (j, 0))),
        scratch_shapes=[pltpu.VMEM((m, d), BF16)],
        compiler_params=_cparams(("arbitrary", "arbitrary")),
        name="ffn_half_step_casting_weights",
    )(x, g, w_gu, w_gu, w_down)


def _ffn_pages_body(pt_ref, x_ref, g_ref, wgu_ref, wd_ref, *refs):
    del pt_ref
    pages, w_ref, o_ref, c_ref, xn_ref = refs[:PAGES_PER_STEP], *refs[PAGES_PER_STEP:]

    def page_sums():
        w = w_ref[...]
        blk_rows = CMP_BLOCK * KV_ROWS
        sums = []
        for page in pages:
            for half in range(PAGE_SIZE // CMP_BLOCK):
                prod = page[half * blk_rows:(half + 1) * blk_rows, :] * w
                part = jnp.sum(prod.reshape(blk_rows // SUBLANES, SUBLANES, HEAD_DIM), axis=0)
                sums.append(part[0:KV_ROWS] + part[KV_ROWS:2 * KV_ROWS])
        for r in range(KV_ROWS):
            c_ref[r] = jnp.concatenate([s[r:r + 1] for s in sums], axis=0)

    _ffn_step(x_ref, g_ref, wgu_ref, wd_ref, o_ref, xn_ref, side_job=page_sums)


def _ffn(x, g, w_gu, w_down, layer, *, tm, tf=512, pages=None):
    m, d = x.shape
    f = w_down.shape[0]
    nf = f // tf
    grid = (m // tm, nf)
    x_spec = pl.BlockSpec((tm, d), lambda i, j, *_: (i, 0))
    in_specs = [
        x_spec,
        pl.BlockSpec((1, d), lambda i, j, *_: (0, 0)),
        pl.BlockSpec((d, 2 * tf), lambda i, j, *_: (0, j)),
        pl.BlockSpec((tf, d), lambda i, j, *_: (j, 0)),
    ]
    scratch = [pltpu.VMEM((tm, d), BF16)]
    params = _cparams(("parallel", "arbitrary"))
    if pages is None:
        return pl.pallas_call(
            _ffn_body, out_shape=jax.ShapeDtypeStruct((m, d), F32), grid=grid, in_specs=in_specs,
            out_specs=x_spec, scratch_shapes=scratch, compiler_params=params, name="ffn_half_step",
        )(x, g, w_gu, w_down)

    cache, page_table, w_cmp = pages
    n_seq, n_pages = page_table.shape
    bpp = PAGE_SIZE // CMP_BLOCK
    steps_per_seq = n_pages // PAGES_PER_STEP
    n_page_steps = n_seq * steps_per_seq
    assert n_pages % PAGES_PER_STEP == 0 and 2 * KV_ROWS == SUBLANES and n_page_steps <= grid[0] * grid[1]

    def page_step(i, j):
        return jnp.minimum(i * nf + j, n_page_steps - 1)

    def page_spec(k):
        def index(i, j, pt):
            t = page_step(i, j)
            return (layer, pt[t // steps_per_seq, (t % steps_per_seq) * PAGES_PER_STEP + k], 0, 0)
        return pl.BlockSpec((None, None, cache.shape[2], HEAD_DIM), index)

    def sums_index(i, j, pt):
        t = page_step(i, j)
        return (t // steps_per_seq, 0, t % steps_per_seq, 0)

    return pl.pallas_call(
        _ffn_pages_body,
        out_shape=(jax.ShapeDtypeStruct((m, d), F32),
                   jax.ShapeDtypeStruct((n_seq, KV_ROWS, n_pages * bpp, HEAD_DIM), F32)),
        grid_spec=pltpu.PrefetchScalarGridSpec(
            num_scalar_prefetch=1,
            grid=grid,
            in_specs=in_specs + [page_spec(k) for k in range(PAGES_PER_STEP)]
            + [pl.BlockSpec((CMP_BLOCK * KV_ROWS, HEAD_DIM), lambda i, j, pt: (0, 0))],
            out_specs=(x_spec, pl.BlockSpec((None, KV_ROWS, PAGES_PER_STEP * bpp, HEAD_DIM), sums_index)),
            scratch_shapes=scratch,
        ),
        compiler_params=_cparams(("arbitrary", "arbitrary")),
        name="ffn_half_step_with_page_sums",
    )(page_table, x, g, w_gu, w_down, *([cache] * PAGES_PER_STEP), w_cmp)


def _qkv_body(x_ref, g_ref, w_ref, cos_ref, sin_ref, qg_ref, kg_ref, gb_ref,
              q_ref, kvb_ref, cmp_ref, sel_ref, win_ref, gt_ref):
    tm = x_ref.shape[0]
    xn = _rms(x_ref[...], g_ref[...]).astype(BF16)
    h = _dot(xn, w_ref[...])
    cos = cos_ref[...]
    sin = sin_ref[...]

    def head(c0, gain):
        return _rope(_rms(h[:, c0:c0 + HEAD_DIM], gain), cos, sin)

    for hh in range(NSA_HEADS):
        c0 = _C_Q + hh * HEAD_DIM
        q_ref[:, hh * HEAD_DIM:(hh + 1) * HEAD_DIM] = head(c0, qg_ref[...]).astype(BF16)
    for br, out in enumerate((cmp_ref, sel_ref, win_ref)):
        base = _C_KV + br * 2 * NSA_KV_W
        for r in range(KV_ROWS):
            c0 = base + r * HEAD_DIM
            slab = head(c0, kg_ref[br:br + 1, :]) if r < NSA_KV_HEADS else h[:, c0:c0 + HEAD_DIM]
            out[pl.ds(r, tm, stride=KV_ROWS), :] = slab
            kvb_ref[:, c0 - _C_KV:c0 - _C_KV + HEAD_DIM] = slab.astype(BF16)
    gt_ref[...] = _sigmoid(h[:, _C_GT:_C_GT + 2 * LANE] + gb_ref[...])


def _qkv_proj(x, g, w, cos, sin, q_gain, k_gain, gate_b, *, tm, tiles_per_seq):
    m, d = x.shape
    n = w.shape[1]
    row = lambda i: (i, 0)
    fixed = lambda i: (0, 0)
    pos = lambda i: (i % tiles_per_seq, 0)
    return pl.pallas_call(
        _qkv_body,
        out_shape=(
            jax.ShapeDtypeStruct((m, NSA_Q_W), BF16),
            jax.ShapeDtypeStruct((m, 6 * NSA_KV_W), BF16),
            jax.ShapeDtypeStruct((m * KV_ROWS, HEAD_DIM), F32),
            jax.ShapeDtypeStruct((m * KV_ROWS, HEAD_DIM), F32),
            jax.ShapeDtypeStruct((m * KV_ROWS, HEAD_DIM), F32),
            jax.ShapeDtypeStruct((m, 2 * LANE), F32),
        ),
        grid=(m // tm,),
        in_specs=[
            pl.BlockSpec((tm, d), row),
            pl.BlockSpec((1, d), fixed),
            pl.BlockSpec((d, n), fixed),
            pl.BlockSpec((tm, HEAD_DIM), pos),
            pl.BlockSpec((tm, HEAD_DIM), pos),
            pl.BlockSpec((1, HEAD_DIM), fixed),
            pl.BlockSpec((3, HEAD_DIM), fixed),
            pl.BlockSpec((1, 2 * LANE), fixed),
        ],
        out_specs=(
            pl.BlockSpec((tm, NSA_Q_W), row),
            pl.BlockSpec((tm, 6 * NSA_KV_W), row),
            pl.BlockSpec((tm * KV_ROWS, HEAD_DIM), row),
            pl.BlockSpec((tm * KV_ROWS, HEAD_DIM), row),
            pl.BlockSpec((tm * KV_ROWS, HEAD_DIM), row),
            pl.BlockSpec((tm, 2 * LANE), row),
        ),
        compiler_params=_cparams(("parallel",)),
        name="nsa_projection",
    )(x, g, w, cos, sin, q_gain, k_gain, gate_b)


def _softmax_av(s, v):
    e = jnp.exp(s - jnp.max(s, axis=-1, keepdims=True))
    return _dot(e.astype(BF16), v) / jnp.sum(e, axis=-1, keepdims=True)


EXP2_SCALE = SCALE * float(np.log2(np.e))
NSA_PREFIXES = 8


def _nsa_prompt_body(q_ref, gt_ref, kc_ref, vc_ref, ksb_ref, vsb_ref, kwb_ref, vwb_ref, wk_ref, wv_ref, o_ref,
                     kcb_ref, vcb_ref, acc_ref):
    i = pl.program_id(2)
    tq = q_ref.shape[0]
    s_len = kc_ref.shape[0]
    nb = s_len // CMP_BLOCK

    @pl.when(i == 0)
    def _():
        kcf = kc_ref[...].astype(F32).reshape(nb, CMP_BLOCK, HEAD_DIM)
        vcf = vc_ref[...].astype(F32).reshape(nb, CMP_BLOCK, HEAD_DIM)
        kcb_ref[...] = jnp.sum(kcf * wk_ref[...][None], axis=1).astype(BF16)
        vcb_ref[...] = jnp.sum(vcf * wv_ref[...][None], axis=1).astype(BF16)

    qpos = i * tq + lax.broadcasted_iota(jnp.int32, (tq, 1), 0)
    qpos_t = i * tq + lax.broadcasted_iota(jnp.int32, (1, tq), 1)
    kc = kcb_ref[...]
    vc = vcb_ref[...]
    blk = lax.broadcasted_iota(jnp.int32, (1, nb), 1)
    blk_t = lax.broadcasted_iota(jnp.int32, (nb, 1), 0)
    valid_c = ((blk + 1) * CMP_BLOCK - 1) <= qpos
    valid_t = ((blk_t + 1) * CMP_BLOCK - 1) <= qpos_t
    imp_t = jnp.zeros((nb, tq), F32)
    o_c = []
    for hh in range(NSA_HPG):
        qh = q_ref[:, hh * HEAD_DIM:(hh + 1) * HEAD_DIM]
        s = jnp.where(valid_c, _dot_nt(qh, kc) * SCALE, NEG)
        e = jnp.where(valid_c, jnp.exp(s - jnp.max(s, axis=-1, keepdims=True)), 0.0)
        p = e / jnp.maximum(jnp.sum(e, axis=-1, keepdims=True), 1e-30)
        o_c.append(_dot(p.astype(BF16), vc))
        s_t = jnp.where(valid_t, _dot_nt(kc, qh) * SCALE, NEG)
        e_t = jnp.where(valid_t, jnp.exp(s_t - jnp.max(s_t, axis=0, keepdims=True)), 0.0)
        imp_t = imp_t + e_t / jnp.maximum(jnp.sum(e_t, axis=0, keepdims=True), 1e-30)

    cur_t = qpos_t // SEL_BLOCK
    forced = jnp.where(blk_t == cur_t, 1, jnp.where(blk_t == cur_t - 1, 1, jnp.where(blk_t == 0, 1, 0)))
    score = jnp.where(blk_t > cur_t, -FORCE, jnp.where(forced > 0, FORCE, imp_t))
    rank = jnp.zeros((nb, tq), jnp.int32)
    for b2 in range(nb):
        other = score[b2:b2 + 1, :]
        tie = jnp.where(other == score, jnp.where(blk_t > b2, 1, 0), 0)
        rank = rank + jnp.where(other > score, 1, tie)
    keep = jnp.where(rank < min(SEL_TOPK, nb), 1.0, 0.0).T.astype(BF16)

    nt = s_len // tq
    n_prefix = min(NSA_PREFIXES, nt)
    tiles_per_prefix = nt // n_prefix
    for k in range(n_prefix):
        @pl.when(i // tiles_per_prefix == k)
        def _(k=k):
            width = (k + 1) * tiles_per_prefix * tq
            expand = jnp.where(
                lax.broadcasted_iota(jnp.int32, (nb, width), 1) // SEL_BLOCK
                == lax.broadcasted_iota(jnp.int32, (nb, width), 0), 1.0, 0.0).astype(BF16)
            keep_keys = _dot(keep, expand)
            kpos = lax.broadcasted_iota(jnp.int32, (1, width), 1)
            bias = jnp.where(keep_keys > 0.5, jnp.where(kpos <= qpos, 0.0, NEG), NEG)
            ks = ksb_ref[0:width, :]
            vs = vsb_ref[0:width, :]
            for hh in range(NSA_HPG):
                s = _dot_nt(q_ref[:, hh * HEAD_DIM:(hh + 1) * HEAD_DIM], ks) + bias
                e = jnp.exp2((s - jnp.max(s, axis=-1, keepdims=True)) * EXP2_SCALE)
                acc_ref[hh] = _dot(e.astype(BF16), vs) / jnp.sum(e, axis=-1, keepdims=True)

    span = WINDOW + tq
    kstart = pl.multiple_of(jnp.maximum(i * tq - WINDOW, 0), tq)
    kw = kwb_ref[pl.ds(kstart, span), :]
    vw = vwb_ref[pl.ds(kstart, span), :]
    rel = qpos - (kstart + lax.broadcasted_iota(jnp.int32, (1, span), 1))
    bias_w = jnp.where(rel >= 0, jnp.where(rel <= WINDOW, 0.0, NEG), NEG)

    gt = gt_ref[...]
    for hh in range(NSA_HPG):
        s = _dot_nt(q_ref[:, hh * HEAD_DIM:(hh + 1) * HEAD_DIM], kw) + bias_w
        e = jnp.exp2((s - jnp.max(s, axis=-1, keepdims=True)) * EXP2_SCALE)
        o_w = _dot(e.astype(BF16), vw) / jnp.sum(e, axis=-1, keepdims=True)
        o = (gt[:, 3 * hh:3 * hh + 1] * o_c[hh] + gt[:, 3 * hh + 1:3 * hh + 2] * acc_ref[hh]
             + gt[:, 3 * hh + 2:3 * hh + 3] * o_w)
        o_ref[:, hh * HEAD_DIM:(hh + 1) * HEAD_DIM] = o.astype(BF16)


def _nsa_prompt(q, gates, kv, wk, wv, *, n_seq, s_len, tq):
    m = q.shape[0]
    nt = s_len // tq
    nb = s_len // CMP_BLOCK
    assert s_len >= WINDOW + tq and s_len % tq == 0 and tq % SEL_BLOCK == 0 and nt % min(NSA_PREFIXES, nt) == 0
    gw = NSA_HPG * HEAD_DIM
    row = lambda n, g, i: (n * nt + i, g)

    def kv_spec(branch, is_value):
        col = branch * KV_ROWS + is_value * NSA_KV_HEADS
        return pl.BlockSpec((s_len, HEAD_DIM), lambda n, g, i: (n, col + g))

    kv_specs = [kv_spec(br, v) for br in range(3) for v in range(2)]
    wspec = pl.BlockSpec((None, CMP_BLOCK, HEAD_DIM), lambda n, g, i: (g, 0, 0))
    return pl.pallas_call(
        _nsa_prompt_body,
        out_shape=jax.ShapeDtypeStruct((m, NSA_Q_W), BF16),
        grid=(n_seq, NSA_KV_HEADS, nt),
        in_specs=[
            pl.BlockSpec((tq, gw), row),
            pl.BlockSpec((tq, LANE), row),
            *kv_specs, wspec, wspec,
        ],
        out_specs=pl.BlockSpec((tq, gw), row),
        scratch_shapes=[pltpu.VMEM((nb, HEAD_DIM), BF16), pltpu.VMEM((nb, HEAD_DIM), BF16),
                        pltpu.VMEM((NSA_HPG, tq, HEAD_DIM), F32)],
        compiler_params=_cparams(("parallel", "parallel", "arbitrary")),
        name="nsa_prompt",
    )(q, gates, *([kv] * len(kv_specs)), wk, wv)


def _pool_tile(u, t0, hist_ref, pw_ref, ps_ref, y_ref):
    tm = u.shape[0]
    ext = jnp.concatenate([hist_ref[...], u], axis=0)
    t = t0 + lax.broadcasted_iota(jnp.int32, (tm, 1), 0)
    for gi, win in enumerate(POOL_WINDOWS):
        c0, c1 = gi * POOL_GROUP, (gi + 1) * POOL_GROUP
        acc = ext[:, c0:c1]
        span = 1
        while span < win:
            acc = acc + pltpu.roll(acc, span, axis=0)
            span *= 2
        mean = acc[POOL_HIST:] / jnp.minimum(t + 1, win).astype(F32)
        y = _dot((mean - u[:, c0:c1]).astype(BF16), pw_ref[gi].astype(BF16))
        y_ref[:, c0:c1] = (y * ps_ref[:, c0:c1]).astype(BF16)
    hist_ref[...] = u[tm - POOL_HIST:]


def _gla_pool_prompt_body(x_ref, g_ref, w_ref, wa2_ref, ba_ref, gain_ref, pw_ref, ps_ref,
                          og_ref, st_ref, u_ref, y_ref, state_ref, hist_ref):
    j = pl.program_id(1)
    tm = x_ref.shape[0]

    @pl.when(j == 0)
    def _():
        state_ref[...] = jnp.zeros_like(state_ref)
        hist_ref[...] = jnp.zeros_like(hist_ref)

    h = _dot(_rms(x_ref[...], g_ref[...]).astype(BF16), w_ref[...])
    c_k, c_v, c_r, c_a = GLA_QK, 2 * GLA_QK, 2 * GLA_QK + GLA_WIDTH, 2 * GLA_QK + 2 * GLA_WIDTH
    c_u = c_a + LANE
    u = h[:, c_u:c_u + POOL_WIDTH]
    u_ref[...] = u
    _pool_tile(u, j * tm, hist_ref, pw_ref, ps_ref, y_ref)
    z = _dot(h[:, c_a:c_a + LANE].astype(BF16), wa2_ref[...]) + ba_ref[...]
    la_all = _log_sigmoid(z) / GLA_TAU

    lane = lax.broadcasted_iota(jnp.int32, (1, GLA_QK), 1)
    head_mask = [jnp.where(lane // GLA_DK == hd, 1.0, 0.0) for hd in range(GLA_HEADS)]
    rows = lax.broadcasted_iota(jnp.int32, (GLA_CHUNK, 1), 0)
    causal = rows >= lax.broadcasted_iota(jnp.int32, (1, GLA_CHUNK), 1)
    n_sub = GLA_CHUNK // GLA_SUB

    for c in range(tm // GLA_CHUNK):
        r0, r1 = c * GLA_CHUNK, (c + 1) * GLA_CHUNK
        q = h[r0:r1, 0:GLA_QK] * (GLA_DK ** -0.5)
        k = h[r0:r1, c_k:c_k + GLA_QK]
        b = la_all[r0:r1]
        sh = 1
        while sh < GLA_CHUNK:
            b = b + jnp.where(rows >= sh, pltpu.roll(b, sh, axis=0), 0.0)
            sh *= 2
        state = state_ref[...]
        state_b = state.astype(BF16)
        q_in = q * jnp.exp(b)
        a_parts = [[] for _ in range(GLA_HEADS)]
        for sb in range(n_sub):
            s0, s1 = sb * GLA_SUB, (sb + 1) * GLA_SUB
            ref_row = b[s0 - 1:s0] if sb > 0 else jnp.zeros((1, GLA_QK), F32)
            q_t = q[s0:s1] * jnp.exp(b[s0:s1] - ref_row)
            k_t = (k * jnp.exp(jnp.where(rows < s1, ref_row - b, 0.0))).astype(BF16)
            for hd in range(GLA_HEADS):
                a_parts[hd].append(_dot_nt((q_t * head_mask[hd]).astype(BF16), k_t))
        b_last = b[GLA_CHUNK - 1:GLA_CHUNK]
        k_out = k * jnp.exp(b_last - b)
        new_state = state * jnp.exp(b_last)
        for hd in range(GLA_HEADS):
            v = h[r0:r1, c_v + hd * GLA_DV:c_v + (hd + 1) * GLA_DV]
            vb = v.astype(BF16)
            a = jnp.where(causal, jnp.concatenate(a_parts[hd], axis=0), 0.0)
            o = _dot(a.astype(BF16), vb) + _dot_nt((q_in * head_mask[hd]).astype(BF16), state_b)
            gr = h[r0:r1, c_r + hd * GLA_DV:c_r + (hd + 1) * GLA_DV]
            og = _rms(o, gain_ref[:, hd * GLA_DV:(hd + 1) * GLA_DV]) * _silu(gr)
            og_ref[r0:r1, hd * GLA_DV:(hd + 1) * GLA_DV] = og.astype(BF16)
            new_state = new_state + _dot(v.T.astype(BF16), (k_out * head_mask[hd]).astype(BF16))
        state_ref[...] = new_state

    @pl.when(j == pl.num_programs(1) - 1)
    def _():
        st_ref[...] = state_ref[...]


def _gla_pool_prompt(x, g, w, wa2, ba, gain, pool_w, pool_scale, *, n_seq, s_len, tm):
    d = x.shape[1]
    nt = s_len // tm
    rows = n_seq * s_len
    row = lambda n, j: (n * nt + j, 0)
    fixed = lambda n, j: (0, 0)
    return pl.pallas_call(
        _gla_pool_prompt_body,
        out_shape=(jax.ShapeDtypeStruct((rows, GLA_WIDTH), BF16),
                   jax.ShapeDtypeStruct((n_seq, GLA_DV, GLA_QK), F32),
                   jax.ShapeDtypeStruct((rows, POOL_WIDTH), F32),
                   jax.ShapeDtypeStruct((rows, POOL_WIDTH), BF16)),
        grid=(n_seq, nt),
        in_specs=[
            pl.BlockSpec((tm, d), row),
            pl.BlockSpec((1, d), fixed),
            pl.BlockSpec((d, w.shape[1]), fixed),
            pl.BlockSpec((LANE, GLA_QK), fixed),
            pl.BlockSpec((1, GLA_QK), fixed),
            pl.BlockSpec((1, GLA_WIDTH), fixed),
            pl.BlockSpec((len(POOL_WINDOWS), POOL_GROUP, POOL_GROUP), lambda n, j: (0, 0, 0)),
            pl.BlockSpec((1, POOL_WIDTH), fixed),
        ],
        out_specs=(pl.BlockSpec((tm, GLA_WIDTH), row),
                   pl.BlockSpec((None, GLA_DV, GLA_QK), lambda n, j: (n, 0, 0)),
                   pl.BlockSpec((tm, POOL_WIDTH), row), pl.BlockSpec((tm, POOL_WIDTH), row)),
        scratch_shapes=[pltpu.VMEM((GLA_DV, GLA_QK), F32), pltpu.VMEM((POOL_HIST, POOL_WIDTH), F32)],
        compiler_params=_cparams(("parallel", "arbitrary")),
        name="gla_pool_prompt",
    )(x, g, w, wa2, ba, gain, pool_w, pool_scale)


def _norm_matmul_body(x_ref, g_ref, w_ref, hg_ref, o_ref, *, n_norm_heads):
    h = _dot(_rms(x_ref[...], g_ref[...]).astype(BF16), w_ref[...])
    if n_norm_heads:
        for hd in range(n_norm_heads):
            c0 = hd * HEAD_DIM
            o_ref[:, c0:c0 + HEAD_DIM] = _rms(h[:, c0:c0 + HEAD_DIM], hg_ref[...])
        c0 = n_norm_heads * HEAD_DIM
        o_ref[:, c0:] = h[:, c0:]
    else:
        o_ref[...] = h


def _norm_matmul(x, g, w, head_gain, *, tm, n_norm_heads=0):
    m, d = x.shape
    n = w.shape[1]
    return pl.pallas_call(
        functools.partial(_norm_matmul_body, n_norm_heads=n_norm_heads),
        out_shape=jax.ShapeDtypeStruct((m, n), F32),
        grid=(m // tm,),
        in_specs=[
            pl.BlockSpec((tm, d), lambda i: (i, 0)),
            pl.BlockSpec((1, d), lambda i: (0, 0)),
            pl.BlockSpec((d, n), lambda i: (0, 0)),
            pl.BlockSpec((1, HEAD_DIM), lambda i: (0, 0)),
        ],
        out_specs=pl.BlockSpec((tm, n), lambda i: (i, 0)),
        compiler_params=_cparams(("parallel",)),
        name="norm_matmul",
    )(x, g, w, head_gain)


def _post_body(x_ref, on_ref, yp_ref, og_ref, wout_ref, mk_ref, mv_ref, qn_ref, wq_ref, qg_ref, wo_ref, o_ref):
    c1, c2 = NSA_Q_W, NSA_Q_W + POOL_WIDTH
    x1 = (x_ref[...] + _dot(on_ref[...], wout_ref[0:c1, :]) + _dot(yp_ref[...], wout_ref[c1:c2, :])
          + _dot(og_ref[...], wout_ref[c2:, :]))
    q = _dot(_rms(x1, qn_ref[...]).astype(BF16), wq_ref[...])
    outs = []
    for hd in range(MEM_HEADS):
        c0 = hd * HEAD_DIM
        qh = _rms(q[:, c0:c0 + HEAD_DIM], qg_ref[...]).astype(BF16)
        s = _dot_nt(qh, mk_ref[:, c0:c0 + HEAD_DIM].astype(BF16)) * SCALE
        outs.append(_softmax_av(s, mv_ref[:, c0:c0 + HEAD_DIM].astype(BF16)).astype(BF16))
    o_ref[...] = x1 + _dot(jnp.concatenate(outs, axis=1), wo_ref[...])


def _post_mix(x, o_nsa, y_pool, o_gla, w_out, mem_kv, q_norm, wq, q_gain, wo, *, n_seq, t_len, tm):
    d = x.shape[1]
    nt = t_len // tm
    mt = mem_kv.shape[1]
    row = lambda n, j: (n * nt + j, 0)
    fixed = lambda n, j: (0, 0)
    return pl.pallas_call(
        _post_body,
        out_shape=jax.ShapeDtypeStruct(x.shape, F32),
        grid=(n_seq, nt),
        in_specs=[
            pl.BlockSpec((tm, d), row),
            pl.BlockSpec((tm, NSA_Q_W), row),
            pl.BlockSpec((tm, POOL_WIDTH), row),
            pl.BlockSpec((tm, GLA_WIDTH), row),
            pl.BlockSpec(w_out.shape, fixed),
            pl.BlockSpec((None, mt, MEM_WIDTH), lambda n, j: (n, 0, 0)),
            pl.BlockSpec((None, mt, MEM_WIDTH), lambda n, j: (n, 0, 1)),
            pl.BlockSpec((1, d), fixed),
            pl.BlockSpec((d, MEM_WIDTH), fixed),
            pl.BlockSpec((1, HEAD_DIM), fixed),
            pl.BlockSpec((MEM_WIDTH, d), fixed),
        ],
        out_specs=pl.BlockSpec((tm, d), row),
        compiler_params=_cparams(("parallel", "parallel")),
        name="mix_out_mem_attn",
    )(x, o_nsa, y_pool, o_gla, w_out, mem_kv, mem_kv, q_norm, wq, q_gain, wo)


def _cmp_select_body(q_ref, c_ref, oc_ref, idx_ref, *, n_past_blocks):
    n_seq = q_ref.shape[0]
    nbp = n_past_blocks
    width = idx_ref.shape[-1]
    cur = nbp
    q = q_ref[...]
    lane = lax.broadcasted_iota(jnp.int32, (n_seq, width), 1)
    lane_f = lane.astype(F32)
    row_head = lax.broadcasted_iota(jnp.int32, (1, NSA_HEADS, 1), 1)
    o_c = jnp.zeros((n_seq, NSA_HEADS, HEAD_DIM), F32)
    for g in range(NSA_KV_HEADS):
        kc = c_ref[:, g].astype(BF16)
        vc = c_ref[:, NSA_KV_HEADS + g].astype(BF16)
        s = lax.dot_general(q, kc, (((2,), (2,)), ((0,), (0,))), preferred_element_type=F32) * SCALE
        e = jnp.exp(s - jnp.max(s, axis=-1, keepdims=True))
        p = e / jnp.maximum(jnp.sum(e, axis=-1, keepdims=True), 1e-30)
        o_g = lax.dot_general(p.astype(BF16), vc, (((2,), (1,)), ((0,), (0,))), preferred_element_type=F32)
        in_group = row_head // NSA_HPG == g
        o_c = jnp.where(in_group, o_g, o_c)
        imp = jnp.sum(jnp.where(in_group, p, 0.0), axis=1)
        imp = jnp.concatenate([imp, jnp.zeros((n_seq, width - nbp), F32)], axis=1)
        forced = jnp.where(lane == cur, 1, jnp.where(lane == cur - 1, 1, jnp.where(lane == 0, 1, 0)))
        score = jnp.where(lane > cur, PAD_SCORE, jnp.where(forced > 0, FORCE, imp))
        picks = jnp.zeros((n_seq, width), F32)
        for it in range(SEL_TOPK):
            top = jnp.max(score, axis=-1, keepdims=True)
            first = jnp.min(jnp.where(score == top, lane_f, float(width)), axis=-1, keepdims=True)
            picks = jnp.where(lane == it, first, picks)
            score = jnp.where(lane_f == first, PAD_SCORE, score)
        idx_ref[g] = picks.astype(jnp.int32)
    oc_ref[...] = o_c


def _cmp_select(q3, cmpc):
    n_seq, _, nbp, _ = cmpc.shape
    assert nbp + 1 >= SEL_TOPK and nbp % LANE == 0
    idx_w = nbp + LANE
    return pl.pallas_call(
        functools.partial(_cmp_select_body, n_past_blocks=nbp),
        out_shape=(jax.ShapeDtypeStruct((n_seq, NSA_HEADS, HEAD_DIM), F32),
                   jax.ShapeDtypeStruct((NSA_KV_HEADS, n_seq, idx_w), jnp.int32)),
        grid=(1,),
        in_specs=[
            pl.BlockSpec((n_seq, NSA_HEADS, HEAD_DIM), lambda i: (0, 0, 0)),
            pl.BlockSpec(cmpc.shape, lambda i: (0, 0, 0, 0)),
        ],
        out_specs=(pl.BlockSpec((n_seq, NSA_HEADS, HEAD_DIM), lambda i: (0, 0, 0)),
                   pl.BlockSpec((NSA_KV_HEADS, n_seq, idx_w), lambda i: (0, 0, 0))),
        compiler_params=_cparams(("arbitrary",)),
        name="cmp_select",
    )(q3, cmpc)


def _one_token_attn(q, rows, g, k_new, v_new, extra_bias):
    n_rows = rows.shape[0]
    kind = lax.broadcasted_iota(jnp.int32, (1, n_rows), 1) % KV_ROWS
    s = _dot_nt(q, rows) * SCALE + jnp.where(kind == g, 0.0, NEG)
    if extra_bias is not None:
        s = s + extra_bias
    s_new = jnp.sum(q.astype(F32) * k_new, axis=-1, keepdims=True) * SCALE
    top = jnp.maximum(jnp.max(s, axis=-1, keepdims=True), s_new)
    e = jnp.exp(s - top)
    e_new = jnp.exp(s_new - top)
    e_val = pltpu.roll(e, NSA_KV_HEADS, axis=1)
    return (_dot(e_val.astype(BF16), rows) + e_new * v_new) / (jnp.sum(e, axis=-1, keepdims=True) + e_new)


def _sel_win_body(blk_ref, new_ref, *refs):
    del blk_ref
    n = pl.program_id(0)
    nk = SEL_TOPK
    n_sel = NSA_KV_HEADS * nk
    sel_refs = refs[:n_sel]
    win_ref, q_ref, sn_ref, wn_ref, oc_ref, gt_ref, o_ref = refs[n_sel:]
    q = q_ref[...]
    sel_new = sn_ref[...]
    win_new = wn_ref[...]
    gt = gt_ref[...]
    row_head = lax.broadcasted_iota(jnp.int32, (NSA_HEADS, 1), 0)
    blk_rows = SEL_BLOCK * KV_ROWS
    key_blk = lax.broadcasted_iota(jnp.int32, (1, nk * blk_rows), 1) // blk_rows
    win_rows = win_ref[...].astype(BF16)
    o_s = jnp.zeros((NSA_HEADS, HEAD_DIM), F32)
    o_w = jnp.zeros((NSA_HEADS, HEAD_DIM), F32)
    for g in range(NSA_KV_HEADS):
        k0, v0 = g * HEAD_DIM, NSA_KV_W + g * HEAD_DIM
        rows = jnp.concatenate([r[...] for r in sel_refs[g * nk:(g + 1) * nk]], axis=0).astype(BF16)
        bias = jnp.where(key_blk == new_ref[n * NSA_KV_HEADS + g], NEG, 0.0)
        o_sg = _one_token_attn(q, rows, g, sel_new[:, k0:k0 + HEAD_DIM], sel_new[:, v0:v0 + HEAD_DIM], bias)
        o_wg = _one_token_attn(q, win_rows, g, win_new[:, k0:k0 + HEAD_DIM], win_new[:, v0:v0 + HEAD_DIM], None)
        in_group = row_head // NSA_HPG == g
        o_s = jnp.where(in_group, o_sg, o_s)
        o_w = jnp.where(in_group, o_wg, o_w)
    o_ref[...] = (gt[:, 0:1] * oc_ref[...] + gt[:, 1:2] * o_s + gt[:, 2:3] * o_w).astype(BF16)


def _sel_win(blk_rows, new_slot, cache_sel, cache_win, q3, sel_new, win_new, o_c, gates3, layer):
    n_seq = q3.shape[0]
    nk = SEL_TOPK

    def sel_spec(g, k):
        return pl.BlockSpec((None, None, SEL_BLOCK * KV_ROWS, HEAD_DIM),
                            lambda n, br, ns: (layer, br[(n * NSA_KV_HEADS + g) * nk + k], 0, 0))

    sel_specs = [sel_spec(g, k) for g in range(NSA_KV_HEADS) for k in range(nk)]
    win_specs = [pl.BlockSpec((None, None, cache_win.shape[2], HEAD_DIM), lambda n, br, ns: (layer, n, 0, 0))]
    tok = lambda n, br, ns: (n, 0, 0)
    return pl.pallas_call(
        _sel_win_body,
        out_shape=jax.ShapeDtypeStruct((n_seq, NSA_HEADS, HEAD_DIM), BF16),
        grid_spec=pltpu.PrefetchScalarGridSpec(
            num_scalar_prefetch=2,
            grid=(n_seq,),
            in_specs=sel_specs + win_specs + [
                pl.BlockSpec((None, NSA_HEADS, HEAD_DIM), tok),
                pl.BlockSpec((None, 1, 2 * NSA_KV_W), tok),
                pl.BlockSpec((None, 1, 2 * NSA_KV_W), tok),
                pl.BlockSpec((None, NSA_HEADS, HEAD_DIM), tok),
                pl.BlockSpec((None, NSA_HEADS, 3), tok),
            ],
            out_specs=pl.BlockSpec((None, NSA_HEADS, HEAD_DIM), tok),
        ),
        compiler_params=_cparams(("parallel",)),
        name="sel_win_sample",
    )(blk_rows, new_slot, *([cache_sel] * len(sel_specs)), *([cache_win] * len(win_specs)),
      q3, sel_new, win_new, o_c, gates3)


def _pool_sample_body(uc_ref, pw_ref, ps_ref, y_ref):
    rows = uc_ref.shape[0]
    for gi, win in enumerate(POOL_WINDOWS):
        c0, c1 = gi * POOL_GROUP, (gi + 1) * POOL_GROUP
        acc = uc_ref[rows - 1, :, c0:c1]
        for r in range(rows - 2, rows - 1 - win, -1):
            acc = acc + uc_ref[r, :, c0:c1]
        d = acc / float(win) - uc_ref[rows - 1, :, c0:c1]
        y_ref[:, c0:c1] = (_dot(d.astype(BF16), pw_ref[gi].astype(BF16)) * ps_ref[:, c0:c1]).astype(BF16)


def _pool_sample(u_ctx_t, pool_w, pool_scale):
    rows, n_seq, width = u_ctx_t.shape
    assert rows >= max(POOL_WINDOWS)
    return pl.pallas_call(
        _pool_sample_body,
        out_shape=jax.ShapeDtypeStruct((n_seq, width), BF16),
        name="pool_sample",
    )(u_ctx_t, pool_w, pool_scale)


def _gla_sample_body(q_ref, k_ref, v_ref, ga_ref, gr_ref, s_ref, wa2t_ref, ba_ref, gain_ref, og_ref, so_ref):
    z = jnp.sum(wa2t_ref[...] * ga_ref[...], axis=-1, keepdims=True) + ba_ref[...]
    decay = jnp.exp(_log_sigmoid(z) / GLA_TAU)
    v = v_ref[...]
    v_rows = jnp.concatenate(
        [jnp.broadcast_to(v[:, hd * GLA_DV:(hd + 1) * GLA_DV], (GLA_DK, GLA_DV)) for hd in range(GLA_HEADS)], axis=0)
    state = decay * s_ref[...] + k_ref[...] * v_rows
    so_ref[...] = state
    qs = (q_ref[...] * (GLA_DK ** -0.5)) * state
    gr = gr_ref[...]
    for hd in range(GLA_HEADS):
        c0, c1 = hd * GLA_DV, (hd + 1) * GLA_DV
        o = jnp.sum(qs[hd * GLA_DK:(hd + 1) * GLA_DK], axis=0, keepdims=True)
        og_ref[:, c0:c1] = (_rms(o, gain_ref[:, c0:c1]) * _silu(gr[:, c0:c1])).astype(BF16)


def _gla_sample(q_col, k_col, v, ga, gr, state, wa2_t, ba_col, gain):
    n_seq = v.shape[0]
    tok = lambda n: (n, 0, 0)
    fixed = lambda n: (0, 0)
    return pl.pallas_call(
        _gla_sample_body,
        out_shape=(jax.ShapeDtypeStruct((n_seq, 1, GLA_WIDTH), BF16),
                   jax.ShapeDtypeStruct((n_seq, GLA_QK, GLA_DV), F32)),
        grid=(n_seq,),
        in_specs=[
            pl.BlockSpec((None, GLA_QK, 1), tok),
            pl.BlockSpec((None, GLA_QK, 1), tok),
            pl.BlockSpec((None, 1, GLA_WIDTH), tok),
            pl.BlockSpec((None, 1, LANE), tok),
            pl.BlockSpec((None, 1, GLA_WIDTH), tok),
            pl.BlockSpec((None, GLA_QK, GLA_DV), tok),
            pl.BlockSpec((GLA_QK, LANE), fixed),
            pl.BlockSpec((GLA_QK, 1), fixed),
            pl.BlockSpec((1, GLA_WIDTH), fixed),
        ],
        out_specs=(pl.BlockSpec((None, 1, GLA_WIDTH), tok), pl.BlockSpec((None, GLA_QK, GLA_DV), tok)),
        compiler_params=_cparams(("parallel",)),
        name="gla_sample",
    )(q_col, k_col, v, ga, gr, state, wa2_t, ba_col, gain)


def _rope_tables(pos):
    half = HEAD_DIM // 2
    inv = ROPE_THETA ** (-jnp.arange(half, dtype=F32) / half)
    ang = pos.astype(F32)[:, None] * inv[None, :]
    cos, sin = jnp.cos(ang), jnp.sin(ang)
    return jnp.concatenate([cos, cos], axis=1), jnp.concatenate([-sin, sin], axis=1)


def _layer_weights(p, l):
    w_in = p['w_in'][l]
    n_gate = 3 * NSA_HPG
    gate_cols = [_pad_cols(w_in[:, _C_GT + g * n_gate:_C_GT + (g + 1) * n_gate], LANE) for g in range(NSA_KV_HEADS)]
    gate_b = p['nsa_gate_bias'][l]
    ga_cols = _pad_cols(w_in[:, _C_GA:_C_GR], LANE)
    gla_cols = [w_in[:, _C_GQ:_C_GA], w_in[:, _C_GR:_C_END], ga_cols]
    row = lambda a: a.reshape(1, -1)
    return dict(
        ffn1_norm=row(p['ffn1_norm'][l]), ffn2_norm=row(p['ffn2_norm'][l]),
        mix_norm=row(p['mix_norm'][l]),
        w_qkv=jnp.concatenate([w_in[:, _C_Q:_C_GT]] + gate_cols, axis=1).astype(BF16),
        gate_b=jnp.concatenate([_pad_cols(row(gate_b[g * n_gate:(g + 1) * n_gate]), LANE) for g in range(NSA_KV_HEADS)], axis=1),
        q_gain=row(p['nsa_q_gain'][l]), k_gain=p['nsa_k_gain'][l],
        wk=p['nsa_cmp_wk'][l], wv=p['nsa_cmp_wv'][l],
        w_cmp=jnp.stack([p['nsa_cmp_wk'][l], p['nsa_cmp_wv'][l]]).transpose(2, 0, 1, 3).reshape(-1, HEAD_DIM),
        w_gla_pool=jnp.concatenate(gla_cols + [w_in[:, _C_U:_C_GQ]], axis=1).astype(BF16),
        pool_w=p['pool_w'][l], pool_scale=row(p['pool_scale'][l]),
        wa2=jnp.pad(p['gla_wa2'][l], ((0, LANE - GLA_RANK), (0, 0))).astype(BF16),
        wa2_t=_pad_cols(p['gla_wa2'][l].T, LANE),
        ba=row(p['gla_ba'][l]), ba_col=p['gla_ba'][l].reshape(-1, 1),
        gla_gain=row(p['gla_o_gain'][l]),
        w_out=p['w_out'][l].astype(BF16),
        mem_q_norm=row(p['mem_q_norm'][l]), mem_kv_norm=row(p['mem_kv_norm'][l]),
        mem_wq=p['mem_wq'][l].astype(BF16), mem_wkv=p['mem_wkv'][l].astype(BF16),
        mem_q_gain=row(p['mem_q_gain'][l]), mem_k_gain=row(p['mem_k_gain'][l]),
        mem_wo=p['mem_wo'][l].astype(BF16),
    )


def _prompt_mixer(x, mem, w, tables, *, n_seq, s_len, tm):
    cos, sin = tables
    q, kv_b, cmp_kv, sel_kv, win_kv, gates = _qkv_proj(
        x, w['mix_norm'], w['w_qkv'], cos, sin, w['q_gain'], w['k_gain'], w['gate_b'], tm=tm, tiles_per_seq=s_len // tm)
    o_nsa = _nsa_prompt(q, gates, kv_b, w['wk'], w['wv'], n_seq=n_seq, s_len=s_len, tq=tm)
    o_gla, state_t, u, y_pool = _gla_pool_prompt(
        x, w['mix_norm'], w['w_gla_pool'], w['wa2'], w['ba'], w['gla_gain'], w['pool_w'], w['pool_scale'],
        n_seq=n_seq, s_len=s_len, tm=tm)
    mkv = _norm_matmul(mem, w['mem_kv_norm'], w['mem_wkv'], w['mem_k_gain'], tm=tm, n_norm_heads=MEM_HEADS)
    mt = mem.shape[0] // n_seq
    x = _post_mix(x, o_nsa, y_pool, o_gla, w['w_out'], mkv.reshape(n_seq, mt, 2 * MEM_WIDTH), w['mem_q_norm'],
                  w['mem_wq'], w['mem_q_gain'], w['mem_wo'], n_seq=n_seq, t_len=s_len, tm=tm)
    w_keep = min(WINDOW, s_len)
    kv_shape = (n_seq, s_len, 2, NSA_KV_HEADS, HEAD_DIM)
    state = state_t.reshape(n_seq, GLA_DV, GLA_HEADS, GLA_DK).transpose(0, 2, 3, 1)
    outs = (cmp_kv.reshape(kv_shape), sel_kv.reshape(kv_shape), win_kv.reshape(kv_shape)[:, s_len - w_keep:],
            u.reshape(n_seq, s_len, POOL_WIDTH)[:, s_len - POOL_BUF:], state,
            mkv.reshape(n_seq, mt, 2, MEM_HEADS, HEAD_DIM))
    return x, outs


def _post_sample_body(x_ref, on_ref, yp_ref, og_ref, wout_ref, mem_ref, qn_ref, wq_ref, qg_ref, wo_ref, o_ref,
                      x1_ref, q_ref, att_ref):
    n = pl.program_id(0)

    @pl.when(n == 0)
    def _():
        c1, c2 = NSA_Q_W, NSA_Q_W + POOL_WIDTH
        x1 = (x_ref[...] + _dot(on_ref[...], wout_ref[0:c1, :]) + _dot(yp_ref[...], wout_ref[c1:c2, :])
              + _dot(og_ref[...], wout_ref[c2:, :]))
        x1_ref[...] = x1
        q = _dot(_rms(x1, qn_ref[...]).astype(BF16), wq_ref[...])
        for hd in range(MEM_HEADS):
            c0 = hd * HEAD_DIM
            q_ref[:, c0:c0 + HEAD_DIM] = _rms(q[:, c0:c0 + HEAD_DIM], qg_ref[...])

    rows = mem_ref[...].astype(BF16)
    q_row = q_ref[pl.ds(n, 1), :]
    q_heads = jnp.concatenate([q_row[:, hd * HEAD_DIM:(hd + 1) * HEAD_DIM] for hd in range(MEM_HEADS)] * 2, axis=0)
    kind = lax.broadcasted_iota(jnp.int32, (1, rows.shape[0]), 1) % (2 * MEM_HEADS)
    head = lax.broadcasted_iota(jnp.int32, (2 * MEM_HEADS, 1), 0) % MEM_HEADS
    s = _dot_nt(q_heads.astype(BF16), rows) * SCALE + jnp.where(kind == head, 0.0, NEG)
    e = jnp.exp(s - jnp.max(s, axis=-1, keepdims=True))
    e_val = pltpu.roll(e, MEM_HEADS, axis=1)
    o = _dot(e_val.astype(BF16), rows) / jnp.sum(e, axis=-1, keepdims=True)
    att_ref[pl.ds(n, 1), :] = jnp.concatenate([o[hd:hd + 1] for hd in range(MEM_HEADS)], axis=1)

    @pl.when(n == pl.num_programs(0) - 1)
    def _():
        o_ref[...] = x1_ref[...] + _dot(att_ref[...].astype(BF16), wo_ref[...])


def _post_mix_sample(x, o_nsa, y_pool, o_gla, w_out, mem_rows, q_norm, wq, q_gain, wo, layer):
    n_seq, d = x.shape
    fixed = lambda n: (0, 0)
    full = lambda a: pl.BlockSpec(a.shape, fixed)
    return pl.pallas_call(
        _post_sample_body,
        out_shape=jax.ShapeDtypeStruct((n_seq, d), F32),
        grid=(n_seq,),
        in_specs=[full(x), full(o_nsa), full(y_pool), full(o_gla), full(w_out),
                  pl.BlockSpec((None, None) + mem_rows.shape[2:], lambda n: (layer, n, 0, 0)),
                  full(q_norm), full(wq), full(q_gain), full(wo)],
        out_specs=pl.BlockSpec((n_seq, d), fixed),
        scratch_shapes=[pltpu.VMEM((n_seq, d), F32), pltpu.VMEM((n_seq, MEM_WIDTH), F32),
                        pltpu.VMEM((n_seq, MEM_WIDTH), F32)],
        compiler_params=_cparams(("arbitrary",)),
        name="mix_out_mem_attn_sample",
    )(x, o_nsa, y_pool, o_gla, w_out, mem_rows, q_norm, wq, q_gain, wo)


def _sample_mixer(x, w, tables, caches, page_sums, page_table, layer):
    cache_sel, cache_win, cache_pool, state_gla, cache_mem = caches
    cos, sin = tables
    n_seq = x.shape[0]
    n_pages = page_table.shape[1]
    bpp = PAGE_SIZE // SEL_BLOCK
    nbp = n_pages * bpp
    kvw = 2 * NSA_KV_W
    q, _, cmp_new, sel_new, win_new, gates = _qkv_proj(
        x, w['mix_norm'], w['w_qkv'], cos, sin, w['q_gain'], w['k_gain'], w['gate_b'], tm=n_seq, tiles_per_seq=1)
    raw = _norm_matmul(x, w['mix_norm'], w['w_gla_pool'], w['q_gain'], tm=n_seq)
    gq, gk = raw[:, :GLA_QK], raw[:, GLA_QK:2 * GLA_QK]
    c0 = 2 * GLA_QK
    gv, gr = raw[:, c0:c0 + GLA_WIDTH], raw[:, c0 + GLA_WIDTH:c0 + 2 * GLA_WIDTH]
    ga, u = raw[:, c0 + 2 * GLA_WIDTH:c0 + 2 * GLA_WIDTH + LANE], raw[:, c0 + 2 * GLA_WIDTH + LANE:]

    q3 = q.reshape(n_seq, NSA_HEADS, HEAD_DIM)
    n_pool = cache_sel.shape[1]
    o_c, idx = _cmp_select(q3, page_sums)
    idx = idx[:, :, :SEL_TOPK].transpose(1, 0, 2)
    pidx = jnp.minimum(idx, nbp - 1)
    phys = jnp.take_along_axis(page_table[:, None, :], pidx // bpp, axis=2)
    blk_rows = (phys * bpp + pidx % bpp).reshape(-1).astype(jnp.int32)
    new_slot = jnp.argmax(idx >= nbp, axis=-1).reshape(-1).astype(jnp.int32)
    gates3 = jnp.concatenate([gates[:, :3 * NSA_HPG], gates[:, LANE:LANE + 3 * NSA_HPG]], axis=1)
    w_buf = cache_win.shape[2]
    o_nsa = _sel_win(blk_rows, new_slot, cache_sel.reshape(-1, n_pool * bpp, SEL_BLOCK * KV_ROWS, HEAD_DIM),
                     cache_win.reshape(-1, n_seq, w_buf * KV_ROWS, HEAD_DIM), q3, sel_new.reshape(n_seq, 1, kvw),
                     win_new.reshape(n_seq, 1, kvw), o_c, gates3.reshape(n_seq, NSA_HEADS, 3), layer)

    u_ctx = jnp.concatenate([cache_pool[layer], u[:, None, :]], axis=1)
    y_pool = _pool_sample(u_ctx.transpose(1, 0, 2), w['pool_w'], w['pool_scale'])
    o_gla, new_state = _gla_sample(
        gq.reshape(n_seq, GLA_QK, 1), gk.reshape(n_seq, GLA_QK, 1), gv.reshape(n_seq, 1, GLA_WIDTH),
        ga.reshape(n_seq, 1, LANE), gr.reshape(n_seq, 1, GLA_WIDTH), state_gla[layer].reshape(n_seq, GLA_QK, GLA_DV),
        w['wa2_t'], w['ba_col'], w['gla_gain'])

    mt = cache_mem.shape[2]
    x = _post_mix_sample(x, o_nsa.reshape(n_seq, NSA_Q_W), y_pool, o_gla.reshape(n_seq, GLA_WIDTH), w['w_out'],
                         cache_mem.reshape(-1, n_seq, mt * 2 * MEM_HEADS, HEAD_DIM), w['mem_q_norm'], w['mem_wq'],
                         w['mem_q_gain'], w['mem_wo'], layer)
    kv_shape = (n_seq, 1, 2, NSA_KV_HEADS, HEAD_DIM)
    outs = (cmp_new.reshape(kv_shape), sel_new.reshape(kv_shape), win_new.reshape(kv_shape), u_ctx[:, 1:],
            new_state.reshape(n_seq, GLA_HEADS, GLA_DK, GLA_DV))
    return x, outs


def kernel(x_prompt, x_sample, cache_cmp_kv, cache_sel_kv, cache_win_kv, cache_pool, state_gla, cache_mem_kv, page_table, mem_prompt, ffn1_norm, ffn1_w_gu, ffn1_w_down, mix_norm, w_in, nsa_q_gain, nsa_k_gain, nsa_gate_bias, nsa_cmp_wk, nsa_cmp_wv, pool_w, pool_scale, gla_wa2, gla_ba, gla_o_gain, w_out, mem_q_norm, mem_kv_norm, mem_wq, mem_wkv, mem_q_gain, mem_k_gain, mem_wo, ffn2_norm, ffn2_w_gu, ffn2_w_down):
    params = dict(ffn1_norm=ffn1_norm, ffn1_w_gu=ffn1_w_gu, ffn1_w_down=ffn1_w_down, mix_norm=mix_norm, w_in=w_in,
                  nsa_q_gain=nsa_q_gain, nsa_k_gain=nsa_k_gain, nsa_gate_bias=nsa_gate_bias, nsa_cmp_wk=nsa_cmp_wk,
                  nsa_cmp_wv=nsa_cmp_wv, pool_w=pool_w, pool_scale=pool_scale, gla_wa2=gla_wa2, gla_ba=gla_ba,
                  gla_o_gain=gla_o_gain, w_out=w_out, mem_q_norm=mem_q_norm, mem_kv_norm=mem_kv_norm, mem_wq=mem_wq,
                  mem_wkv=mem_wkv, mem_q_gain=mem_q_gain, mem_k_gain=mem_k_gain, mem_wo=mem_wo, ffn2_norm=ffn2_norm,
                  ffn2_w_gu=ffn2_w_gu, ffn2_w_down=ffn2_w_down)
    depth = w_in.shape[0]
    n_seq, s_len, d = x_prompt.shape
    n_dec, dec_len, _ = x_sample.shape
    assert dec_len == 1
    past = page_table.shape[1] * PAGE_SIZE
    tables_p = _rope_tables(jnp.arange(s_len))
    tables_s = _rope_tables(jnp.full((n_dec,), past))
    caches = (cache_sel_kv, cache_win_kv, cache_pool, state_gla, cache_mem_kv)
    page_job = (cache_cmp_kv.reshape(depth, -1, PAGE_SIZE * KV_ROWS, HEAD_DIM), page_table)
    xp = x_prompt.reshape(n_seq * s_len, d)
    xs = x_sample.reshape(n_dec, d)
    mem = mem_prompt.reshape(-1, d)
    ffn_tm = min(512, n_seq * s_len)
    p_outs, s_outs = [], []
    for l in range(depth):
        w = _layer_weights(params, l)
        xs, *ffn1_w = _ffn_cast(xs, w['ffn1_norm'], ffn1_w_gu, ffn1_w_down, l)
        xp, page_sums = _ffn(xp, w['ffn1_norm'], *ffn1_w, l, tm=ffn_tm, pages=page_job + (w['w_cmp'],))
        xp, po = _prompt_mixer(xp, mem, w, tables_p, n_seq=n_seq, s_len=s_len, tm=256)
        xs, so = _sample_mixer(xs, w, tables_s, caches, page_sums, page_table, l)
        xs, *ffn2_w = _ffn_cast(xs, w['ffn2_norm'], ffn2_w_gu, ffn2_w_down, l)
        xp = _ffn(xp, w['ffn2_norm'], *ffn2_w, l, tm=ffn_tm)
        p_outs.append(po)
        s_outs.append(so)
    stack = lambda outs, k: jnp.stack([o[k] for o in outs])
    return ((xp.reshape(n_seq, s_len, d), xs.reshape(n_dec, 1, d))
            + tuple(stack(p_outs, k) for k in range(6)) + tuple(stack(s_outs, k) for k in range(2))
            + (jnp.concatenate([cache_win_kv[:, :, dec_len:], stack(s_outs, 2)], axis=2),)
            + tuple(stack(s_outs, k) for k in range(3, 5)))
```
